```python
import jax, jax.numpy as jnp
from jax import lax
import numpy as np

D_MODEL = 1024
BATCH = 8
SEQ = 4096
DEPTH = 1

HEAD_DIM = 64
A_HEADS = 8
A_WIDTH = A_HEADS * HEAD_DIM
DECAY_LORA = 64
ICLR_LORA = 64
GATE_LORA = 160
GN_EPS = 64e-5
B_HEADS = 8
B_WIDTH = B_HEADS * HEAD_DIM
MOBA_BLOCK = 256
MOBA_TOP_K = 3
MOBA_Q_CHUNK = 16
FFN_HIDDEN = ((8 * D_MODEL // 3 + 255) // 256) * 256
RMS_EPS = 1e-6
SHIFT_WIDTH = 3 * A_WIDTH + DECAY_LORA + ICLR_LORA + GATE_LORA
PROJ_WIDTH = SHIFT_WIDTH + 3 * B_WIDTH + 2 * D_MODEL

kernel_name = "hybrid_rwkv7_moba_gated_block"


def _split(t, sizes):
    return jnp.split(t, [int(s) for s in np.cumsum(sizes)[:-1]], axis=-1)


def rms_norm(x, g):
    xf = x.astype(jnp.float32)
    y = xf * lax.rsqrt(jnp.mean(xf * xf, axis=-1, keepdims=True) + RMS_EPS)
    return (y * g.astype(jnp.float32)).astype(x.dtype)


def token_shift_mix(p, mu):
    prev = jnp.pad(p, ((0, 0), (1, 0), (0, 0)))[:, :-1]
    return p + mu * (prev - p)


def alibi_slopes(n_heads):
    return jnp.exp2(-8.0 * jnp.arange(1, n_heads + 1, dtype=jnp.float32) / n_heads)


def rwkv7_time_mix(r, k, v, wd, ad, gd, w0, w_decay_up, a0, w_iclr_up, w_gate_up,
                   k_k, k_a, r_k, ln_w, ln_b):
    B_, S_, _ = r.shape
    w_log = -jax.nn.softplus(-(w0 + jnp.tanh(wd) @ w_decay_up)) - 0.5
    decay = jnp.exp(-jnp.exp(w_log.astype(jnp.float32))).astype(r.dtype)
    a = jax.nn.sigmoid(a0 + ad @ w_iclr_up)
    g = jax.nn.sigmoid(gd) @ w_gate_up

    def heads(t):
        return t.reshape(B_, S_, A_HEADS, HEAD_DIM)

    kkf = heads(k * k_k).astype(jnp.float32)
    kk = (kkf / jnp.maximum(jnp.linalg.norm(kkf, axis=-1, keepdims=True), 1e-12)).astype(k.dtype)
    k = k * (1.0 + (a - 1.0) * k_a)
    rh, wh, kh, vh, ah = heads(r), heads(decay), heads(k), heads(v), heads(a)

    def step(state, inp):
        r_t, w_t, k_t, v_t, kk_t, a_t = inp
        sa = jnp.einsum('bhij,bhj->bhi', state, -kk_t)
        state = (state * w_t[:, :, None, :]
                 + sa[..., None] * (kk_t * a_t)[:, :, None, :]
                 + v_t[..., None] * k_t[:, :, None, :])
        y_t = jnp.einsum('bhij,bhj->bhi', state, r_t)
        return state, y_t

    xs = tuple(jnp.moveaxis(t, 1, 0) for t in (rh, wh, kh, vh, kk, ah))
    s0 = jnp.zeros((B_, A_HEADS, HEAD_DIM, HEAD_DIM), r.dtype)
    _, y = lax.scan(step, s0, xs)
    y = jnp.moveaxis(y, 0, 1)
    yf = y.astype(jnp.float32)
    mean = jnp.mean(yf, axis=-1, keepdims=True)
    var = jnp.mean(jnp.square(yf - mean), axis=-1, keepdims=True)
    y = ((yf - mean) * lax.rsqrt(var + GN_EPS)).astype(r.dtype).reshape(B_, S_, A_WIDTH)
    y = y * ln_w + ln_b
    bonus = jnp.sum(rh * kh * r_k, axis=-1, keepdims=True) * vh
    return (y + bonus.reshape(B_, S_, A_WIDTH)) * g


def moba_attention(q, k, v):
    B_, S_, _ = q.shape
    scale = HEAD_DIM ** -0.5

    def heads(t):
        return t.reshape(B_, S_, B_HEADS, HEAD_DIM).transpose(0, 2, 1, 3)

    q, k, v = heads(q), heads(k), heads(v)
    nb = -(-S_ // MOBA_BLOCK)
    pad = ((0, 0), (0, 0), (0, nb * MOBA_BLOCK - S_), (0, 0))
    kb = jnp.pad(k, pad).reshape(B_, B_HEADS, nb, MOBA_BLOCK, HEAD_DIM)
    vb = jnp.pad(v, pad).reshape(B_, B_HEADS, nb, MOBA_BLOCK, HEAD_DIM)

    k_mean = jnp.mean(kb.astype(jnp.float32), axis=3)
    gate = jnp.einsum('bhsd,bhnd->bhsn', q.astype(jnp.float32), k_mean)
    q_block = jnp.arange(S_) // MOBA_BLOCK
    fully_past = jnp.arange(nb)[None, :] < q_block[:, None]
    gate = jnp.where(fully_past, gate, -jnp.inf)
    n_sel = min(MOBA_TOP_K, nb)
    _, idx = lax.top_k(gate, n_sel)
    slopes = alibi_slopes(B_HEADS)
    gather_blocks = jax.vmap(jax.vmap(lambda blocks, ids: blocks[ids]))

    def chunk_fn(args):
        q_c, idx_c, t0 = args
        t = t0 + jnp.arange(MOBA_Q_CHUNK, dtype=jnp.int32)
        blk = t0 // MOBA_BLOCK
        k_sel = gather_blocks(kb, idx_c)
        v_sel = gather_blocks(vb, idx_c)
        s_sel = jnp.einsum('bhcd,bhcjkd->bhcjk', q_c, k_sel).astype(jnp.float32) * scale
        pos_sel = idx_c[..., None] * MOBA_BLOCK + jnp.arange(MOBA_BLOCK, dtype=jnp.int32)
        dist_sel = (t[:, None, None] - pos_sel).astype(jnp.float32)
        valid = jnp.arange(n_sel) < jnp.minimum(blk, n_sel)
        s_sel = jnp.where(valid[:, None], s_sel - slopes.reshape(1, -1, 1, 1, 1) * dist_sel, -jnp.inf)
        k_own = lax.dynamic_index_in_dim(kb, blk, axis=2, keepdims=False)
        v_own = lax.dynamic_index_in_dim(vb, blk, axis=2, keepdims=False)
        s_own = jnp.einsum('bhcd,bhkd->bhck', q_c, k_own).astype(jnp.float32) * scale
        dist_own = t[:, None] - (blk * MOBA_BLOCK + jnp.arange(MOBA_BLOCK, dtype=jnp.int32))[None, :]
        s_own = jnp.where(dist_own >= 0,
                          s_own - slopes.reshape(1, -1, 1, 1) * dist_own.astype(jnp.float32),
                          -jnp.inf)
        n_past = n_sel * MOBA_BLOCK
        scores = jnp.concatenate([s_sel.reshape(*s_sel.shape[:3], n_past), s_own], axis=-1)
        p = jax.nn.softmax(scores, axis=-1).astype(v_sel.dtype)
        p_sel = p[..., :n_past].reshape(s_sel.shape)
        p_own = p[..., n_past:]
        return (jnp.einsum('bhcjk,bhcjkd->bhcd', p_sel, v_sel)
                + jnp.einsum('bhck,bhkd->bhcd', p_own, v_own))

    n_chunks = S_ // MOBA_Q_CHUNK

    def to_chunks(t):
        return jnp.moveaxis(t.reshape(B_, B_HEADS, n_chunks, MOBA_Q_CHUNK, *t.shape[3:]), 2, 0)

    t0s = jnp.arange(n_chunks, dtype=jnp.int32) * MOBA_Q_CHUNK
    out = lax.map(chunk_fn, (to_chunks(q), to_chunks(idx), t0s))
    out = jnp.moveaxis(out, 0, 2).reshape(B_, B_HEADS, S_, HEAD_DIM)
    return out.transpose(0, 2, 1, 3).reshape(B_, S_, B_WIDTH)


def setup_inputs(seed: int = 0) -> dict:
    key = jax.random.key(seed)
    ks = jax.random.split(key, 22)
    L = DEPTH
    f32 = jnp.float32

    def nrm(k, shape, scale):
        return jax.random.normal(k, shape, f32) * scale

    return {
        "x": nrm(ks[0], (BATCH, SEQ, D_MODEL), 1.0),
        "norm_mix_g": 1.0 + nrm(ks[1], (L, D_MODEL), 0.02),
        "w_in": nrm(ks[2], (L, D_MODEL, PROJ_WIDTH), D_MODEL ** -0.5),
        "mu_shift": jax.random.uniform(ks[3], (L, SHIFT_WIDTH), f32),
        "w0": jax.random.uniform(ks[4], (L, A_WIDTH), f32, -6.0, 0.0),
        "w_decay_up": nrm(ks[5], (L, DECAY_LORA, A_WIDTH), 0.5 * DECAY_LORA ** -0.5),
        "a0": nrm(ks[6], (L, A_WIDTH), 0.5),
        "w_iclr_up": nrm(ks[7], (L, ICLR_LORA, A_WIDTH), ICLR_LORA ** -0.5),
        "w_gate_up": nrm(ks[8], (L, GATE_LORA, A_WIDTH), GATE_LORA ** -0.5),
        "k_k": 0.85 + nrm(ks[9], (L, A_WIDTH), 0.05),
        "k_a": 1.0 + nrm(ks[10], (L, A_WIDTH), 0.05),
        "r_k": nrm(ks[11], (L, A_HEADS, HEAD_DIM), 0.1),
        "ln_x_w": 1.0 + nrm(ks[12], (L, A_WIDTH), 0.02),
        "ln_x_b": nrm(ks[13], (L, A_WIDTH), 0.02),
        "w_up_a": nrm(ks[14], (L, A_WIDTH, D_MODEL), A_WIDTH ** -0.5),
        "w_up_b": nrm(ks[15], (L, B_WIDTH, D_MODEL), B_WIDTH ** -0.5),
        "w_o": nrm(ks[16], (L, D_MODEL, D_MODEL), D_MODEL ** -0.5),
        "norm_ffn_g": 1.0 + nrm(ks[17], (L, D_MODEL), 0.02),
        "w_ffn_in": nrm(ks[18], (L, D_MODEL, 2 * FFN_HIDDEN), D_MODEL ** -0.5),
        "w_ffn_out": nrm(ks[19], (L, FFN_HIDDEN, D_MODEL), FFN_HIDDEN ** -0.5),
        "norm_final_g": 1.0 + nrm(ks[20], (D_MODEL,), 0.02),
    }


def reference(x, norm_mix_g, w_in, mu_shift, w0, w_decay_up, a0, w_iclr_up, w_gate_up,
              k_k, k_a, r_k, ln_x_w, ln_x_b, w_up_a, w_up_b, w_o, norm_ffn_g,
              w_ffn_in, w_ffn_out, norm_final_g):
    for l in range(DEPTH):
        h = rms_norm(x, norm_mix_g[l])
        proj = h @ w_in[l]
        shift_part, q_b, k_b, v_b, gate_a, gate_b = _split(
            proj, [SHIFT_WIDTH, B_WIDTH, B_WIDTH, B_WIDTH, D_MODEL, D_MODEL])
        shift_part = token_shift_mix(shift_part, mu_shift[l])
        r_a, k_a_in, v_a, wd, ad, gd = _split(
            shift_part, [A_WIDTH, A_WIDTH, A_WIDTH, DECAY_LORA, ICLR_LORA, GATE_LORA])
        y_a = rwkv7_time_mix(r_a, k_a_in, v_a, wd, ad, gd, w0[l], w_decay_up[l], a0[l],
                             w_iclr_up[l], w_gate_up[l], k_k[l], k_a[l], r_k[l],
                             ln_x_w[l], ln_x_b[l]) @ w_up_a[l]
        y_b = moba_attention(q_b, k_b, v_b) @ w_up_b[l]
        mixed = jax.nn.sigmoid(gate_a) * y_a + jax.nn.sigmoid(gate_b) * y_b
        x = x + mixed @ w_o[l]
        h = rms_norm(x, norm_ffn_g[l])
        g_ff, u_ff = _split(h @ w_ffn_in[l], [FFN_HIDDEN, FFN_HIDDEN])
        x = x + (jax.nn.silu(g_ff) * u_ff) @ w_ffn_out[l]
    return rms_norm(x, norm_final_g)
```

```python
import functools

import jax
import jax.numpy as jnp
from jax import lax
from jax.experimental import pallas as pl
from jax.experimental.pallas import tpu as pltpu

F32 = jnp.float32
BF16 = jnp.bfloat16

D_MODEL = 1024
HEAD_DIM = 64
N_HEADS = 8
WIDTH = N_HEADS * HEAD_DIM
N_PAIRS = N_HEADS // 2
LANES = 128
DECAY_LORA = 64
ICLR_LORA = 64
GATE_LORA = 160
GN_EPS = 64e-5
RMS_EPS = 1e-6
MOBA_BLOCK = 256
MOBA_TOP_K = 3
FFN_HIDDEN = 2816
SHIFT_WIDTH = 3 * WIDTH + DECAY_LORA + ICLR_LORA + GATE_LORA
SHIFT_PAD = 1920
QKV_B = 3 * WIDTH
GATES = 2 * D_MODEL
PROJ_PAD = SHIFT_PAD + QKV_B + GATES

CHUNK = 64
VMEM_LIMIT = 56 * 1024 * 1024


def _dot(a, b):
    return jnp.dot(a, b, preferred_element_type=F32)


def _dot_nt(a, b):
    return lax.dot_general(a, b, (((1,), (1,)), ((), ())), preferred_element_type=F32)


def _rms(x, g):
    return x * lax.rsqrt(jnp.mean(x * x, axis=-1, keepdims=True) + RMS_EPS) * g


def _proj_kernel(x_ref, g_ref, w_ref, sp_ref, qkv_ref, gate_ref):
    h = _rms(x_ref[...], g_ref[...]).astype(BF16)
    sp_ref[...] = _dot(h, w_ref[:, 0:SHIFT_PAD])
    qkv_ref[...] = _dot(h, w_ref[:, SHIFT_PAD:SHIFT_PAD + QKV_B]).astype(BF16)
    gate_ref[...] = _dot(h, w_ref[:, SHIFT_PAD + QKV_B:PROJ_PAD]).astype(BF16)


def _proj(x2d, g, w, tm):
    t = x2d.shape[0]
    const = lambda i: (0, 0)
    return pl.pallas_call(
        _proj_kernel,
        grid=(t // tm,),
        in_specs=[
            pl.BlockSpec((tm, D_MODEL), lambda i: (i, 0)),
            pl.BlockSpec((1, D_MODEL), const),
            pl.BlockSpec((D_MODEL, PROJ_PAD), const, pipeline_mode=pl.Buffered(1)),
        ],
        out_specs=[
            pl.BlockSpec((tm, SHIFT_PAD), lambda i: (i, 0)),
            pl.BlockSpec((tm, QKV_B), lambda i: (i, 0)),
            pl.BlockSpec((tm, GATES), lambda i: (i, 0)),
        ],
        out_shape=[
            jax.ShapeDtypeStruct((t, SHIFT_PAD), F32),
            jax.ShapeDtypeStruct((t, QKV_B), BF16),
            jax.ShapeDtypeStruct((t, GATES), BF16),
        ],
        compiler_params=pltpu.CompilerParams(
            dimension_semantics=("arbitrary",), vmem_limit_bytes=VMEM_LIMIT),
        name="proj",
    )(x2d, g, w)


def _split2(x):
    hi = x.astype(BF16)
    lo = (x - hi.astype(F32)).astype(BF16)
    return hi, lo


def _segsum(x, seg):
    hi, lo = _split2(x)
    return _dot(hi, seg) + _dot(lo, seg)


def _rwkv_kernel(sp_ref, mu_ref, w0_ref, wda_ref, a0_ref, wg_ref, kk_ref, ka_ref, rk_ref,
                 lnw_ref, lnb_ref, seg_ref, tri_ref, o_ref, h_ref, carry_ref, *, tt):
    n_chunks = tt // CHUNK

    @pl.when(pl.program_id(1) == 0)
    def _():
        carry_ref[...] = jnp.zeros_like(carry_ref)
        h_ref[...] = jnp.zeros_like(h_ref)

    p = sp_ref[0]
    row = lax.broadcasted_iota(jnp.int32, (tt, 1), 0)
    prev = jnp.where(row == 0, carry_ref[0:1, :], pltpu.roll(p, 1, 0))
    carry_ref[0:1, :] = p[tt - 1:tt, :]
    xs = p + mu_ref[...] * (prev - p)

    r = xs[:, 0:WIDTH]
    k = xs[:, WIDTH:2 * WIDTH]
    v = xs[:, 2 * WIDTH:3 * WIDTH]
    da = xs[:, 3 * WIDTH:3 * WIDTH + LANES]
    gd = xs[:, 3 * WIDTH + LANES:SHIFT_PAD]
    lane = lax.broadcasted_iota(jnp.int32, (1, LANES), 1)
    lo_half = lane < HEAD_DIM
    da_act = jnp.where(lo_half, jnp.tanh(da), da).astype(BF16)
    lora = _dot(da_act, wda_ref[...])
    g = _dot(jax.nn.sigmoid(gd).astype(BF16), wg_ref[...])

    z = -(w0_ref[...] + lora[:, 0:WIDTH])
    softplus = jnp.maximum(z, 0.0) + jnp.log1p(jnp.exp(-jnp.abs(z)))
    logdec = -jnp.exp(-softplus - 0.5)
    a = jax.nn.sigmoid(a0_ref[...] + lora[:, WIDTH:2 * WIDTH])

    seg = seg_ref[...]
    kkf = k * kk_ref[...]
    norm = jnp.sqrt(_segsum(kkf * kkf, seg))
    kk = kkf / jnp.maximum(norm, 1e-12)
    k2 = k * (1.0 + (a - 1.0) * ka_ref[...])
    bonus = _segsum(r * k2 * rk_ref[...], seg) * v

    l1 = logdec.astype(BF16)
    r1 = logdec - l1.astype(F32)
    l2 = r1.astype(BF16)
    l3 = (r1 - l2.astype(F32)).astype(BF16)
    tri = tri_ref[...]
    cum = _dot(tri, l1) + _dot(tri, l2) + _dot(tri, l3)

    kka = kk * a
    p_inv = jnp.exp(-cum)
    a_t = -kk * jnp.exp(cum - logdec)
    r_t = r * jnp.exp(cum)
    b_t = kka * p_inv
    k_t = k2 * p_inv

    r128 = lax.broadcasted_iota(jnp.int32, (LANES, LANES), 0)
    c128 = lax.broadcasted_iota(jnp.int32, (LANES, LANES), 1)
    cm = c128 & (CHUNK - 1)
    rm = r128 & (CHUNK - 1)
    score_mask = (cm < rm) | ((r128 >= CHUNK) & (cm == rm))
    eye128 = (r128 == c128).astype(F32)
    eye_top = eye128[0:CHUNK, :]
    row_lo = r128 < HEAD_DIM
    zeros_c = jnp.zeros((CHUNK, LANES), F32)
    half_masks = (lo_half, jnp.logical_not(lo_half))
    row_masks = (row_lo, jnp.logical_not(row_lo))

    phis, psis, qeffs, ylocs = [], [], [], []
    for c in range(n_chunks):
        rows = slice(c * CHUNK, (c + 1) * CHUNK)
        cum_c = cum[rows]
        cum_last = cum_c[CHUNK - 1:CHUNK, :]
        to_end = jnp.exp(cum_last - cum_c)
        b_h = kka[rows] * to_end
        k_h = k2[rows] * to_end
        p_end = jnp.exp(cum_last)
        phi_c, psi_c, qeff_c, yloc_c = [], [], [], []
        for pr in range(N_PAIRS):
            ln = slice(pr * LANES, (pr + 1) * LANES)
            a_p, r_p, v_p = a_t[rows, ln], r_t[rows, ln], v[rows, ln]
            lq = jnp.concatenate([a_p, r_p], axis=0)
            rk = jnp.concatenate([b_t[rows, ln], k_t[rows, ln]], axis=0).astype(BF16)
            qeff = r_p
            yloc = zeros_c
            g_heads = []
            for hm in half_masks:
                sc = _dot_nt(jnp.where(hm, lq, 0.0).astype(BF16), rk)
                sc = jnp.where(score_mask, sc, 0.0)
                m_top = sc[0:CHUNK]
                a_r = sc[CHUNK:2 * CHUNK]
                t_top = eye_top + m_top
                pw = m_top
                for _ in range(5):
                    pw_b = pw.astype(BF16)
                    pw = _dot(pw_b, jnp.concatenate([pw_b, zeros_c.astype(BF16)], axis=0))
                    pw_full = jnp.concatenate([pw, zeros_c], axis=0).astype(BF16)
                    t_top = t_top + _dot(t_top.astype(BF16), pw_full)
                a_m = jnp.where(hm, a_p, 0.0)
                v_m = jnp.where(hm, v_p, 0.0)
                rhs1 = jnp.concatenate(
                    [jnp.concatenate([a_m, zeros_c], axis=1),
                     jnp.concatenate([zeros_c, v_m], axis=1)], axis=0).astype(BF16)
                tw = _dot(t_top.astype(BF16), rhs1)
                g_h = jnp.concatenate(
                    [tw, jnp.concatenate([zeros_c, v_m], axis=1)], axis=0).astype(BF16)
                ag = _dot(a_r.astype(BF16), g_h)
                qeff = qeff + ag[:, 0:LANES]
                yloc = yloc + ag[:, LANES:2 * LANES]
                g_heads.append(g_h)
            kk_t = jnp.concatenate([b_h[:, ln], k_h[:, ln]], axis=0).T
            lhs = jnp.concatenate([jnp.where(rmask, kk_t, 0.0) for rmask in row_masks],
                                  axis=1).astype(BF16)
            pp = _dot(lhs, jnp.concatenate(g_heads, axis=0))
            phi_c.append(eye128 * p_end[:, ln] + pp[:, 0:LANES])
            psi_c.append(pp[:, LANES:2 * LANES])
            qeff_c.append(qeff)
            yloc_c.append(yloc)
        phis.append(phi_c); psis.append(psi_c); qeffs.append(qeff_c); ylocs.append(yloc_c)

    y_rows = []
    states = [h_ref[pr] for pr in range(N_PAIRS)]
    for c in range(n_chunks):
        y_c = []
        for pr in range(N_PAIRS):
            hb = states[pr].astype(BF16)
            y_c.append(_dot(qeffs[c][pr].astype(BF16), hb) + ylocs[c][pr])
            states[pr] = _dot(phis[c][pr].astype(BF16), hb) + psis[c][pr]
        y_rows.append(jnp.concatenate(y_c, axis=1))
    for pr in range(N_PAIRS):
        h_ref[pr] = states[pr]
    y = jnp.concatenate(y_rows, axis=0)

    mean = _segsum(y, seg) * (1.0 / HEAD_DIM)
    d = y - mean
    var = _segsum(d * d, seg) * (1.0 / HEAD_DIM)
    yn = d * lax.rsqrt(var + GN_EPS) * lnw_ref[...] + lnb_ref[...]
    o_ref[0] = ((yn + bonus) * g).astype(BF16)


def _rwkv(sp, mu, w0, wda, a0, wg, k_k, k_a, r_k, ln_w, ln_b, seg, tri, tt):
    b, s, _ = sp.shape
    const2 = lambda i, j: (0, 0)
    row = lambda n: pl.BlockSpec((1, n), const2)
    return pl.pallas_call(
        functools.partial(_rwkv_kernel, tt=tt),
        grid=(b, s // tt),
        in_specs=[
            pl.BlockSpec((1, tt, SHIFT_PAD), lambda i, j: (i, j, 0)),
            row(SHIFT_PAD), row(WIDTH),
            pl.BlockSpec((LANES, 2 * WIDTH), const2),
            row(WIDTH),
            pl.BlockSpec((2 * LANES, WIDTH), const2),
            row(WIDTH), row(WIDTH), row(WIDTH), row(WIDTH), row(WIDTH),
            pl.BlockSpec((WIDTH, WIDTH), const2),
            pl.BlockSpec((tt, tt), const2),
        ],
        out_specs=pl.BlockSpec((1, tt, WIDTH), lambda i, j: (i, j, 0)),
        out_shape=jax.ShapeDtypeStruct((b, s, WIDTH), BF16),
        scratch_shapes=[
            pltpu.VMEM((N_PAIRS, LANES, LANES), F32),
            pltpu.VMEM((8, SHIFT_PAD), F32),
        ],
        compiler_params=pltpu.CompilerParams(
            dimension_semantics=("arbitrary", "arbitrary"), vmem_limit_bytes=VMEM_LIMIT),
        name="rwkv",
    )(sp, mu, w0, wda, a0, wg, k_k, k_a, r_k, ln_w, ln_b, seg, tri)


NEG_BIG = -(2.0 ** 30)
N_BLOCKS_MAX = 16


def _moba_kernel(q_ref, k_ref, v_ref, slope_ref, o_ref, kx_ref, vt_ref, *, seq):
    nb = seq // MOBA_BLOCK
    bs = MOBA_BLOCK
    scale = HEAD_DIM ** -0.5

    rr = lax.broadcasted_iota(jnp.int32, (bs, LANES), 0)
    ll = lax.broadcasted_iota(jnp.int32, (bs, LANES), 1)
    for n in range(nb):
        aug = jnp.where(ll < 2, 1.0,
              jnp.where(ll == 2, rr.astype(F32),
              jnp.where(ll == 3, float(n * bs),
              jnp.where(ll == 8 + n, 1.0, 0.0))))
        rows = slice(n * bs, (n + 1) * bs)
        kx_ref[n, :, 0:LANES] = k_ref[0, rows, :]
        kx_ref[n, :, LANES:2 * LANES] = aug.astype(BF16)
        vt_ref[n] = v_ref[0, rows, :].astype(F32).T.astype(BF16)

    kall = k_ref[0]
    bi = lax.broadcasted_iota(jnp.int32, (N_BLOCKS_MAX, seq), 0)
    si = lax.broadcasted_iota(jnp.int32, (N_BLOCKS_MAX, seq), 1)
    avg = jnp.where((si >= bi * bs) & (si < (bi + 1) * bs), 1.0 / bs, 0.0).astype(BF16)
    kmean = _dot(avg, kall)

    lane = lax.broadcasted_iota(jnp.int32, (1, LANES), 1)
    lane_masks = (lane < HEAD_DIM, lane >= HEAD_DIM)
    r128 = lax.broadcasted_iota(jnp.int32, (LANES, 1), 0)
    row_masks = (r128 < HEAD_DIM, r128 >= HEAD_DIM)
    m_iota = lax.broadcasted_iota(jnp.int32, (N_BLOCKS_MAX, bs), 0)
    r8 = lax.broadcasted_iota(jnp.int32, (8, bs), 0)
    tl = lax.broadcasted_iota(jnp.int32, (8, bs), 1).astype(F32)
    key_row = lax.broadcasted_iota(jnp.int32, (bs, bs), 0)
    qry_col = lax.broadcasted_iota(jnp.int32, (bs, bs), 1)
    causal = key_row <= qry_col
    zeros_pad = jnp.zeros((LANES - 8 - N_BLOCKS_MAX, bs), F32)

    def q_block(i, carry):
        start = pl.multiple_of(i * bs, bs)
        q_t = q_ref[0, pl.ds(start, bs), :].astype(F32).T
        q_tb = q_t.astype(BF16)
        t0 = (i * bs).astype(F32)

        q_aug = []
        for h in range(2):
            km_hi, km_lo = _split2(jnp.where(lane_masks[h], kmean, 0.0))
            gate = _dot(km_hi, q_tb) + _dot(km_lo, q_tb)
            gm = jnp.where(m_iota < i, gate, -jnp.inf)
            bias = jnp.zeros((N_BLOCKS_MAX, bs), F32)
            for n in range(nb):
                g_n = gm[n:n + 1, :]
                beats = (gm > g_n) | ((gm == g_n) & (m_iota < n))
                cnt = jnp.sum(beats.astype(F32), axis=0, keepdims=True)
                keep = jnp.logical_or(cnt < MOBA_TOP_K, n >= i)
                bias = jnp.where(m_iota == n, jnp.where(keep, 0.0, NEG_BIG), bias)
            slope = slope_ref[0, h:h + 1, :]
            aug8 = jnp.where(r8 == 0, -slope * tl,
                   jnp.where(r8 == 1, -slope * t0,
                   jnp.where((r8 == 2) | (r8 == 3), slope, 0.0)))
            top = jnp.where(row_masks[h], q_t, 0.0) * scale
            q_aug.append(jnp.concatenate([top, aug8, bias, zeros_pad], axis=0).astype(BF16))

        vrows = (slice(0, HEAD_DIM), slice(HEAD_DIM, 2 * HEAD_DIM))

        kx_i = kx_ref[i]
        vt_i = vt_ref[i]
        state = []
        for h in range(2):
            s = jnp.where(causal, _dot(kx_i, q_aug[h]), NEG_BIG)
            m = jnp.max(s, axis=0, keepdims=True)
            p = jnp.exp(s - m)
            l = jnp.sum(p, axis=0, keepdims=True)
            acc = _dot(vt_i[vrows[h], :], p.astype(BF16))
            state += [m, l, acc]

        def kv_block(j, st):
            kx_j = kx_ref[j]
            vt_j = vt_ref[j]
            out = []
            for h in range(2):
                m, l, acc = st[3 * h:3 * h + 3]
                s = _dot(kx_j, q_aug[h])
                m_new = jnp.maximum(m, jnp.max(s, axis=0, keepdims=True))
                alpha = jnp.exp(m - m_new)
                p = jnp.exp(s - m_new)
                l = alpha * l + jnp.sum(p, axis=0, keepdims=True)
                acc = alpha * acc + _dot(vt_j[vrows[h], :], p.astype(BF16))
                out += [m_new, l, acc]
            return tuple(out)

        st = lax.fori_loop(0, i, kv_block, tuple(state))
        out_t = jnp.concatenate([st[2] / st[1], st[5] / st[4]], axis=0)
        o_ref[0, pl.ds(start, bs), :] = out_t.T.astype(BF16)
        return carry

    lax.fori_loop(0, nb, q_block, 0)


def _moba(qkv, slopes):
    b, s, _ = qkv.shape
    assert s % MOBA_BLOCK == 0 and s // MOBA_BLOCK <= N_BLOCKS_MAX
    nb = s // MOBA_BLOCK
    col = lambda off: pl.BlockSpec((1, s, LANES), lambda i, j: (i, 0, off + j))
    return pl.pallas_call(
        functools.partial(_moba_kernel, seq=s),
        grid=(b, N_PAIRS),
        in_specs=[col(0), col(N_PAIRS), col(2 * N_PAIRS),
                  pl.BlockSpec((1, 8, MOBA_BLOCK), lambda i, j: (j, 0, 0))],
        out_specs=pl.BlockSpec((1, s, LANES), lambda i, j: (i, 0, j)),
        out_shape=jax.ShapeDtypeStruct((b, s, WIDTH), BF16),
        scratch_shapes=[
            pltpu.VMEM((nb, MOBA_BLOCK, 2 * LANES), BF16),
            pltpu.VMEM((nb, LANES, MOBA_BLOCK), BF16),
        ],
        compiler_params=pltpu.CompilerParams(
            dimension_semantics=("arbitrary", "arbitrary"), vmem_limit_bytes=VMEM_LIMIT),
        name="moba",
    )(qkv, qkv, qkv, slopes)


FFN_CHUNKS = ((0, 1024), (1024, 1024), (2048, 768))


def _post_kernel(x_ref, ya_ref, yb_ref, gate_ref, wua_ref, wub_ref, wo_ref, gffn_ref,
                 wfi_ref, wfo_ref, gfin_ref, o_ref):
    y_a = _dot(ya_ref[...], wua_ref[...])
    y_b = _dot(yb_ref[...], wub_ref[...])
    ga = jax.nn.sigmoid(gate_ref[:, 0:D_MODEL].astype(F32))
    gb = jax.nn.sigmoid(gate_ref[:, D_MODEL:2 * D_MODEL].astype(F32))
    mixed = (ga * y_a + gb * y_b).astype(BF16)
    x1 = x_ref[...] + _dot(mixed, wo_ref[...])
    h = _rms(x1, gffn_ref[...]).astype(BF16)
    acc = x1
    for off, n in FFN_CHUNKS:
        gg = _dot(h, wfi_ref[:, off:off + n])
        uu = _dot(h, wfi_ref[:, FFN_HIDDEN + off:FFN_HIDDEN + off + n])
        act = (gg * jax.nn.sigmoid(gg) * uu).astype(BF16)
        acc = acc + _dot(act, wfo_ref[off:off + n, :])
    o_ref[...] = _rms(acc, gfin_ref[...])


def _post(x2d, ya, yb, gates, wua, wub, wo, gffn, wfi, wfo, gfin, tm):
    t = x2d.shape[0]
    const = lambda i: (0, 0)
    tile = lambda n: pl.BlockSpec((tm, n), lambda i: (i, 0))
    weight = lambda a: pl.BlockSpec(a.shape, const, pipeline_mode=pl.Buffered(1))
    return pl.pallas_call(
        _post_kernel,
        grid=(t // tm,),
        in_specs=[tile(D_MODEL), tile(WIDTH), tile(WIDTH), tile(GATES),
                  weight(wua), weight(wub), weight(wo), weight(gffn),
                  weight(wfi), weight(wfo), weight(gfin)],
        out_specs=tile(D_MODEL),
        out_shape=jax.ShapeDtypeStruct((t, D_MODEL), F32),
        compiler_params=pltpu.CompilerParams(
            dimension_semantics=("arbitrary",), vmem_limit_bytes=VMEM_LIMIT),
        name="post",
    )(x2d, ya, yb, gates, wua, wub, wo, gffn, wfi, wfo, gfin)


def _layer(x, norm_mix_g, w_in, mu_shift, w0, w_decay_up, a0, w_iclr_up, w_gate_up,
           k_k, k_a, r_k, ln_x_w, ln_x_b, w_up_a, w_up_b, w_o, norm_ffn_g,
           w_ffn_in, w_ffn_out, out_g):
    b, s, _ = x.shape
    t = b * s
    x2d = x.reshape(t, D_MODEL)

    pad = SHIFT_PAD - SHIFT_WIDTH
    w_pad = jnp.concatenate(
        [w_in[:, :SHIFT_WIDTH], jnp.zeros((D_MODEL, pad), F32), w_in[:, SHIFT_WIDTH:]],
        axis=1).astype(BF16)
    mu = jnp.pad(mu_shift, (0, pad)).reshape(1, SHIFT_PAD)
    wda = jnp.zeros((LANES, 2 * WIDTH), F32)
    wda = wda.at[:DECAY_LORA, :WIDTH].set(w_decay_up).at[DECAY_LORA:, WIDTH:].set(w_iclr_up)
    wg = jnp.pad(w_gate_up, ((0, 2 * LANES - GATE_LORA), (0, 0)))
    row = lambda a: a.reshape(1, -1)

    tt = 128
    hid = jnp.arange(WIDTH) // HEAD_DIM
    seg = (hid[:, None] == hid[None, :]).astype(BF16)
    ti = jnp.arange(tt)
    tri = ((ti[:, None] >= ti[None, :]) & (ti[:, None] // CHUNK == ti[None, :] // CHUNK)).astype(BF16)

    sp, qkv, gates = _proj(x2d, row(norm_mix_g), w_pad, tm=256)
    ya = _rwkv(sp.reshape(b, s, SHIFT_PAD), mu, row(w0), wda.astype(BF16), row(a0),
               wg.astype(BF16), row(k_k), row(k_a), row(r_k), row(ln_x_w), row(ln_x_b),
               seg, tri, tt)
    slopes = jnp.exp2(-8.0 * jnp.arange(1, N_HEADS + 1, dtype=F32) / N_HEADS)
    slopes = jnp.broadcast_to(
        jnp.pad(slopes.reshape(N_PAIRS, 2), ((0, 0), (0, 6)))[:, :, None], (N_PAIRS, 8, MOBA_BLOCK))
    yb = _moba(qkv.reshape(b, s, QKV_B), slopes)
    out = _post(x2d, ya.reshape(t, WIDTH), yb.reshape(t, WIDTH), gates,
                w_up_a.astype(BF16), w_up_b.astype(BF16), w_o.astype(BF16), row(norm_ffn_g),
                w_ffn_in.astype(BF16), w_ffn_out.astype(BF16), row(out_g), tm=256)
    return out.reshape(b, s, D_MODEL)


def kernel(x, norm_mix_g, w_in, mu_shift, w0, w_decay_up, a0, w_iclr_up, w_gate_up, k_k, k_a, r_k, ln_x_w, ln_x_b, w_up_a, w_up_b, w_o, norm_ffn_g, w_ffn_in, w_ffn_out, norm_final_g):
    depth = w_in.shape[0]
    assert depth == 1, "the fused post kernel applies the final norm after the only layer"
    return _layer(x, norm_mix_g[0], w_in[0], mu_shift[0], w0[0], w_decay_up[0], a0[0],
                  w_iclr_up[0], w_gate_up[0], k_k[0], k_a[0], r_k[0].reshape(-1), ln_x_w[0],
                  ln_x_b[0], w_up_a[0], w_up_b[0], w_o[0], norm_ffn_g[0], w_ffn_in[0],
                  w_ffn_out[0], norm_final_g)
```

```python
import functools

import jax
import jax.numpy as jnp
from jax import lax
from jax.experimental import pallas as pl
from jax.experimental.pallas import tpu as pltpu

F32 = jnp.float32
BF16 = jnp.bfloat16

D_MODEL = 1024
HEAD_DIM = 64
N_HEADS = 8
WIDTH = N_HEADS * HEAD_DIM
N_PAIRS = N_HEADS // 2
LANES = 128
DECAY_LORA = 64
ICLR_LORA = 64
GATE_LORA = 160
GN_EPS = 64e-5
RMS_EPS = 1e-6
MOBA_BLOCK = 256
MOBA_TOP_K = 3
FFN_HIDDEN = 2816
SHIFT_WIDTH = 3 * WIDTH + DECAY_LORA + ICLR_LORA + GATE_LORA
SHIFT_PAD = 1920
QKV_B = 3 * WIDTH
GATES = 2 * D_MODEL
PROJ_PAD = SHIFT_PAD + QKV_B + GATES

CHUNK = 64
VMEM_LIMIT = 56 * 1024 * 1024


def _dot(a, b):
    return jnp.dot(a, b, preferred_element_type=F32)


def _dot_nt(a, b):
    return lax.dot_general(a, b, (((1,), (1,)), ((), ())), preferred_element_type=F32)


def _rms(x, g):
    return x * lax.rsqrt(jnp.mean(x * x, axis=-1, keepdims=True) + RMS_EPS) * g


def _proj_kernel(x_ref, g_ref, w_ref, sp_ref, qkv_ref, gate_ref):
    h = _rms(x_ref[...], g_ref[...]).astype(BF16)
    sp_ref[...] = _dot(h, w_ref[:, 0:SHIFT_PAD])
    qkv_ref[...] = _dot(h, w_ref[:, SHIFT_PAD:SHIFT_PAD + QKV_B]).astype(BF16)
    gate_ref[...] = _dot(h, w_ref[:, SHIFT_PAD + QKV_B:PROJ_PAD]).astype(BF16)


def _proj(x2d, g, w, tm):
    t = x2d.shape[0]
    const = lambda i: (0, 0)
    return pl.pallas_call(
        _proj_kernel,
        grid=(t // tm,),
        in_specs=[
            pl.BlockSpec((tm, D_MODEL), lambda i: (i, 0)),
            pl.BlockSpec((1, D_MODEL), const),
            pl.BlockSpec((D_MODEL, PROJ_PAD), const, pipeline_mode=pl.Buffered(1)),
        ],
        out_specs=[
            pl.BlockSpec((tm, SHIFT_PAD), lambda i: (i, 0)),
            pl.BlockSpec((tm, QKV_B), lambda i: (i, 0)),
            pl.BlockSpec((tm, GATES), lambda i: (i, 0)),
        ],
        out_shape=[
            jax.ShapeDtypeStruct((t, SHIFT_PAD), F32),
            jax.ShapeDtypeStruct((t, QKV_B), BF16),
            jax.ShapeDtypeStruct((t, GATES), BF16),
        ],
        compiler_params=pltpu.CompilerParams(
            dimension_semantics=("arbitrary",), vmem_limit_bytes=VMEM_LIMIT),
        name="proj",
    )(x2d, g, w)


def _split2(x):
    hi = x.astype(BF16)
    lo = (x - hi.astype(F32)).astype(BF16)
    return hi, lo


def _segsum(x, seg):
    hi, lo = _split2(x)
    return _dot(hi, seg) + _dot(lo, seg)


def _rwkv_kernel(sp_ref, mu_ref, w0_ref, wda_ref, a0_ref, wg_ref, kk_ref, ka_ref, rk_ref,
                 lnw_ref, lnb_ref, seg_ref, tri_ref, o_ref, h_ref, carry_ref, *, tt):
    n_chunks = tt // CHUNK

    @pl.when(pl.program_id(1) == 0)
    def _():
        carry_ref[...] = jnp.zeros_like(carry_ref)
        h_ref[...] = jnp.zeros_like(h_ref)

    p = sp_ref[0]
    row = lax.broadcasted_iota(jnp.int32, (tt, 1), 0)
    prev = jnp.where(row == 0, carry_ref[0:1, :], pltpu.roll(p, 1, 0))
    carry_ref[0:1, :] = p[tt - 1:tt, :]
    xs = p + mu_ref[...] * (prev - p)

    r = xs[:, 0:WIDTH]
    k = xs[:, WIDTH:2 * WIDTH]
    v = xs[:, 2 * WIDTH:3 * WIDTH]
    da = xs[:, 3 * WIDTH:3 * WIDTH + LANES]
    gd = xs[:, 3 * WIDTH + LANES:SHIFT_PAD]
    lane = lax.broadcasted_iota(jnp.int32, (1, LANES), 1)
    lo_half = lane < HEAD_DIM
    da_act = jnp.where(lo_half, jnp.tanh(da), da).astype(BF16)
    lora = _dot(da_act, wda_ref[...])
    g = _dot(jax.nn.sigmoid(gd).astype(BF16), wg_ref[...])

    z = -(w0_ref[...] + lora[:, 0:WIDTH])
    softplus = jnp.maximum(z, 0.0) + jnp.log1p(jnp.exp(-jnp.abs(z)))
    logdec = -jnp.exp(-softplus - 0.5)
    a = jax.nn.sigmoid(a0_ref[...] + lora[:, WIDTH:2 * WIDTH])

    seg = seg_ref[...]
    kkf = k * kk_ref[...]
    norm = jnp.sqrt(_segsum(kkf * kkf, seg))
    kk = kkf / jnp.maximum(norm, 1e-12)
    k2 = k * (1.0 + (a - 1.0) * ka_ref[...])
    bonus = _segsum(r * k2 * rk_ref[...], seg) * v

    l1 = logdec.astype(BF16)
    r1 = logdec - l1.astype(F32)
    l2 = r1.astype(BF16)
    l3 = (r1 - l2.astype(F32)).astype(BF16)
    tri = tri_ref[...]
    cum = _dot(tri, l1) + _dot(tri, l2) + _dot(tri, l3)

    kka = kk * a
    p_inv = jnp.exp(-cum)
    a_t = -kk * jnp.exp(cum - logdec)
    r_t = r * jnp.exp(cum)
    b_t = kka * p_inv
    k_t = k2 * p_inv

    r64 = lax.broadcasted_iota(jnp.int32, (CHUNK, LANES), 0)
    c64 = lax.broadcasted_iota(jnp.int32, (CHUNK, LANES), 1) & (CHUNK - 1)
    strict = c64 < r64
    incl = c64 <= r64
    eye_side = (c64 == r64).astype(F32)
    r128 = lax.broadcasted_iota(jnp.int32, (LANES, LANES), 0)
    c128 = lax.broadcasted_iota(jnp.int32, (LANES, LANES), 1)
    eye128 = (r128 == c128).astype(F32)
    same_head = (r128 >> 6) == (c128 >> 6)
    hi_half = jnp.logical_not(lo_half)
    zeros_c = jnp.zeros((CHUNK, LANES), F32)

    def _stack2(zz):
        return jnp.concatenate([jnp.where(lo_half, zz, 0.0), jnp.where(hi_half, zz, 0.0)],
                               axis=0).astype(BF16)

    chains = [(c, pr) for c in range(n_chunks) for pr in range(N_PAIRS)]

    def _piece(arr, c, pr):
        return arr[c * CHUNK:(c + 1) * CHUNK, pr * LANES:(pr + 1) * LANES]

    x_ab, x_ak, x_rb, x_rk = {}, {}, {}, {}
    for ch in chains:
        lq = jnp.concatenate([_piece(a_t, *ch), _piece(r_t, *ch)], axis=0).astype(BF16)
        rhs = jnp.concatenate([_stack2(_piece(b_t, *ch)), _stack2(_piece(k_t, *ch))], axis=0)
        sc = _dot_nt(lq, rhs)
        x_ab[ch] = jnp.where(strict, sc[0:CHUNK, 0:LANES], 0.0)
        x_ak[ch] = jnp.where(strict, sc[0:CHUNK, LANES:2 * LANES], 0.0)
        x_rb[ch] = jnp.where(incl, sc[CHUNK:2 * CHUNK, 0:LANES], 0.0)
        x_rk[ch] = jnp.where(incl, sc[CHUNK:2 * CHUNK, LANES:2 * LANES], 0.0)

    x_t = {ch: eye_side + x_ab[ch] for ch in chains}
    x_pw = {ch: _dot(x_ab[ch].astype(BF16), _stack2(x_ab[ch])) for ch in chains}
    av = {ch: _dot(x_ak[ch].astype(BF16), _stack2(_piece(v, *ch))) for ch in chains}
    for _ in range(4):
        for ch in chains:
            both = jnp.concatenate([x_t[ch], x_pw[ch]], axis=0).astype(BF16)
            out = _dot(both, _stack2(x_pw[ch]))
            x_t[ch] = x_t[ch] + out[0:CHUNK]
            x_pw[ch] = out[CHUNK:2 * CHUNK]
    for ch in chains:
        x_t[ch] = x_t[ch] + _dot(x_t[ch].astype(BF16), _stack2(x_pw[ch]))

    tw = {}
    for ch in chains:
        rhs = jnp.concatenate([_stack2(_piece(a_t, *ch)), _stack2(av[ch])], axis=1)
        tw[ch] = _dot(x_t[ch].astype(BF16), rhs)
    qeff, yloc, phi, psi = {}, {}, {}, {}
    for ch in chains:
        ta, w_loc, v_p = tw[ch][:, 0:LANES], tw[ch][:, LANES:2 * LANES], _piece(v, *ch)
        rhs = jnp.concatenate(
            [jnp.concatenate([_stack2(ta), _stack2(w_loc)], axis=1),
             jnp.concatenate([jnp.zeros((LANES, LANES), BF16), _stack2(v_p)], axis=1)], axis=0)
        ag = _dot(jnp.concatenate([x_rb[ch], x_rk[ch]], axis=1).astype(BF16), rhs)
        qeff[ch] = _piece(r_t, *ch) + ag[:, 0:LANES]
        yloc[ch] = ag[:, LANES:2 * LANES]
        cum_c = _piece(cum, *ch)
        cum_last = cum_c[CHUNK - 1:CHUNK, :]
        to_end = jnp.exp(cum_last - cum_c)
        kk_t = jnp.concatenate([_piece(kka, *ch) * to_end, _piece(k2, *ch) * to_end], axis=0).T
        rhs = jnp.concatenate(
            [tw[ch], jnp.concatenate([zeros_c, v_p], axis=1)], axis=0).astype(BF16)
        pp = _dot(kk_t.astype(BF16), rhs)
        phi[ch] = eye128 * jnp.exp(cum_last) + jnp.where(same_head, pp[:, 0:LANES], 0.0)
        psi[ch] = jnp.where(same_head, pp[:, LANES:2 * LANES], 0.0)

    y_rows = []
    states = [h_ref[pr] for pr in range(N_PAIRS)]
    for c in range(n_chunks):
        y_c = []
        for pr in range(N_PAIRS):
            ch = (c, pr)
            lhs = jnp.concatenate([qeff[ch], phi[ch]], axis=0).astype(BF16)
            out = _dot(lhs, states[pr].astype(BF16))
            y_c.append(out[0:CHUNK] + yloc[ch])
            states[pr] = out[CHUNK:CHUNK + LANES] + psi[ch]
        y_rows.append(jnp.concatenate(y_c, axis=1))
    for pr in range(N_PAIRS):
        h_ref[pr] = states[pr]
    y = jnp.concatenate(y_rows, axis=0)

    mean = _segsum(y, seg) * (1.0 / HEAD_DIM)
    d = y - mean
    var = _segsum(d * d, seg) * (1.0 / HEAD_DIM)
    yn = d * lax.rsqrt(var + GN_EPS) * lnw_ref[...] + lnb_ref[...]
    o_ref[0] = ((yn + bonus) * g).astype(BF16)


def _rwkv(sp, mu, w0, wda, a0, wg, k_k, k_a, r_k, ln_w, ln_b, seg, tri, tt):
    b, s, _ = sp.shape
    const2 = lambda i, j: (0, 0)
    row = lambda n: pl.BlockSpec((1, n), const2)
    return pl.pallas_call(
        functools.partial(_rwkv_kernel, tt=tt),
        grid=(b, s // tt),
        in_specs=[
            pl.BlockSpec((1, tt, SHIFT_PAD), lambda i, j: (i, j, 0)),
            row(SHIFT_PAD), row(WIDTH),
            pl.BlockSpec((LANES, 2 * WIDTH), const2),
            row(WIDTH),
            pl.BlockSpec((2 * LANES, WIDTH), const2),
            row(WIDTH), row(WIDTH), row(WIDTH), row(WIDTH), row(WIDTH),
            pl.BlockSpec((WIDTH, WIDTH), const2),
            pl.BlockSpec((tt, tt), const2),
        ],
        out_specs=pl.BlockSpec((1, tt, WIDTH), lambda i, j: (i, j, 0)),
        out_shape=jax.ShapeDtypeStruct((b, s, WIDTH), BF16),
        scratch_shapes=[
            pltpu.VMEM((N_PAIRS, LANES, LANES), F32),
            pltpu.VMEM((8, SHIFT_PAD), F32),
        ],
        compiler_params=pltpu.CompilerParams(
            dimension_semantics=("arbitrary", "arbitrary"), vmem_limit_bytes=VMEM_LIMIT),
        name="rwkv",
    )(sp, mu, w0, wda, a0, wg, k_k, k_a, r_k, ln_w, ln_b, seg, tri)


NEG_BIG = -(2.0 ** 30)
N_BLOCKS_MAX = 16


def _moba_kernel(q_ref, k_ref, v_ref, slope_ref, o_ref, kx_ref, vt_ref, *, seq):
    nb = seq // MOBA_BLOCK
    bs = MOBA_BLOCK
    scale = HEAD_DIM ** -0.5

    rr = lax.broadcasted_iota(jnp.int32, (bs, LANES), 0)
    ll = lax.broadcasted_iota(jnp.int32, (bs, LANES), 1)
    for n in range(nb):
        aug = jnp.where(ll < 2, 1.0,
              jnp.where(ll == 2, rr.astype(F32),
              jnp.where(ll == 3, float(n * bs),
              jnp.where(ll == 8 + n, 1.0, 0.0))))
        rows = slice(n * bs, (n + 1) * bs)
        kx_ref[n, :, 0:LANES] = k_ref[0, rows, :]
        kx_ref[n, :, LANES:2 * LANES] = aug.astype(BF16)
        vt_ref[n] = v_ref[0, rows, :].astype(F32).T.astype(BF16)

    kall = k_ref[0]
    bi = lax.broadcasted_iota(jnp.int32, (N_BLOCKS_MAX, seq), 0)
    si = lax.broadcasted_iota(jnp.int32, (N_BLOCKS_MAX, seq), 1)
    avg = jnp.where((si >= bi * bs) & (si < (bi + 1) * bs), 1.0 / bs, 0.0).astype(BF16)
    kmean = _dot(avg, kall)

    lane = lax.broadcasted_iota(jnp.int32, (1, LANES), 1)
    lane_masks = (lane < HEAD_DIM, lane >= HEAD_DIM)
    r128 = lax.broadcasted_iota(jnp.int32, (LANES, 1), 0)
    row_masks = (r128 < HEAD_DIM, r128 >= HEAD_DIM)
    m_iota = lax.broadcasted_iota(jnp.int32, (N_BLOCKS_MAX, bs), 0)
    r8 = lax.broadcasted_iota(jnp.int32, (8, bs), 0)
    tl = lax.broadcasted_iota(jnp.int32, (8, bs), 1).astype(F32)
    key_row = lax.broadcasted_iota(jnp.int32, (bs, bs), 0)
    qry_col = lax.broadcasted_iota(jnp.int32, (bs, bs), 1)
    causal = key_row <= qry_col
    zeros_pad = jnp.zeros((LANES - 8 - N_BLOCKS_MAX, bs), F32)

    def q_block(i, carry):
        start = pl.multiple_of(i * bs, bs)
        q_t = q_ref[0, pl.ds(start, bs), :].astype(F32).T
        q_tb = q_t.astype(BF16)
        t0 = (i * bs).astype(F32)

        q_aug = []
        for h in range(2):
            km_hi, km_lo = _split2(jnp.where(lane_masks[h], kmean, 0.0))
            gate = _dot(km_hi, q_tb) + _dot(km_lo, q_tb)
            gm = jnp.where(m_iota < i, gate, -jnp.inf)
            bias = jnp.zeros((N_BLOCKS_MAX, bs), F32)
            for n in range(nb):
                g_n = gm[n:n + 1, :]
                beats = (gm > g_n) | ((gm == g_n) & (m_iota < n))
                cnt = jnp.sum(beats.astype(F32), axis=0, keepdims=True)
                keep = jnp.logical_or(cnt < MOBA_TOP_K, n >= i)
                bias = jnp.where(m_iota == n, jnp.where(keep, 0.0, NEG_BIG), bias)
            slope = slope_ref[0, h:h + 1, :]
            aug8 = jnp.where(r8 == 0, -slope * tl,
                   jnp.where(r8 == 1, -slope * t0,
                   jnp.where((r8 == 2) | (r8 == 3), slope, 0.0)))
            top = jnp.where(row_masks[h], q_t, 0.0) * scale
            q_aug.append(jnp.concatenate([top, aug8, bias, zeros_pad], axis=0).astype(BF16))

        vrows = (slice(0, HEAD_DIM), slice(HEAD_DIM, 2 * HEAD_DIM))

        kx_i = kx_ref[i]
        vt_i = vt_ref[i]
        state = []
        for h in range(2):
            s = jnp.where(causal, _dot(kx_i, q_aug[h]), NEG_BIG)
            m = jnp.max(s, axis=0, keepdims=True)
            p = jnp.exp(s - m)
            l = jnp.sum(p, axis=0, keepdims=True)
            acc = _dot(vt_i[vrows[h], :], p.astype(BF16))
            state += [m, l, acc]

        def kv_block(j, st):
            kx_j = kx_ref[j]
            vt_j = vt_ref[j]
            out = []
            for h in range(2):
                m, l, acc = st[3 * h:3 * h + 3]
                s = _dot(kx_j, q_aug[h])
                m_new = jnp.maximum(m, jnp.max(s, axis=0, keepdims=True))
                alpha = jnp.exp(m - m_new)
                p = jnp.exp(s - m_new)
                l = alpha * l + jnp.sum(p, axis=0, keepdims=True)
                acc = alpha * acc + _dot(vt_j[vrows[h], :], p.astype(BF16))
                out += [m_new, l, acc]
            return tuple(out)

        st = lax.fori_loop(0, i, kv_block, tuple(state))
        out_t = jnp.concatenate([st[2] / st[1], st[5] / st[4]], axis=0)
        o_ref[0, pl.ds(start, bs), :] = out_t.T.astype(BF16)
        return carry

    lax.fori_loop(0, nb, q_block, 0)


def _moba(qkv, slopes):
    b, s, _ = qkv.shape
    assert s % MOBA_BLOCK == 0 and s // MOBA_BLOCK <= N_BLOCKS_MAX
    nb = s // MOBA_BLOCK
    col = lambda off: pl.BlockSpec((1, s, LANES), lambda i, j: (i, 0, off + j))
    return pl.pallas_call(
        functools.partial(_moba_kernel, seq=s),
        grid=(b, N_PAIRS),
        in_specs=[col(0), col(N_PAIRS), col(2 * N_PAIRS),
                  pl.BlockSpec((1, 8, MOBA_BLOCK), lambda i, j: (j, 0, 0))],
        out_specs=pl.BlockSpec((1, s, LANES), lambda i, j: (i, 0, j)),
        out_shape=jax.ShapeDtypeStruct((b, s, WIDTH), BF16),
        scratch_shapes=[
            pltpu.VMEM((nb, MOBA_BLOCK, 2 * LANES), BF16),
            pltpu.VMEM((nb, LANES, MOBA_BLOCK), BF16),
        ],
        compiler_params=pltpu.CompilerParams(
            dimension_semantics=("arbitrary", "arbitrary"), vmem_limit_bytes=VMEM_LIMIT),
        name="moba",
    )(qkv, qkv, qkv, slopes)


FFN_CHUNKS = ((0, 1024), (1024, 1024), (2048, 768))


def _post_kernel(x_ref, ya_ref, yb_ref, gate_ref, wua_ref, wub_ref, wo_ref, gffn_ref,
                 wfi_ref, wfo_ref, gfin_ref, o_ref):
    y_a = _dot(ya_ref[...], wua_ref[...])
    y_b = _dot(yb_ref[...], wub_ref[...])
    ga = jax.nn.sigmoid(gate_ref[:, 0:D_MODEL].astype(F32))
    gb = jax.nn.sigmoid(gate_ref[:, D_MODEL:2 * D_MODEL].astype(F32))
    mixed = (ga * y_a + gb * y_b).astype(BF16)
    x1 = x_ref[...] + _dot(mixed, wo_ref[...])
    h = _rms(x1, gffn_ref[...]).astype(BF16)
    acc = x1
    for off, n in FFN_CHUNKS:
        gg = _dot(h, wfi_ref[:, off:off + n])
        uu = _dot(h, wfi_ref[:, FFN_HIDDEN + off:FFN_HIDDEN + off + n])
        act = (gg * jax.nn.sigmoid(gg) * uu).astype(BF16)
        acc = acc + _dot(act, wfo_ref[off:off + n, :])
    o_ref[...] = _rms(acc, gfin_ref[...])


def _post(x2d, ya, yb, gates, wua, wub, wo, gffn, wfi, wfo, gfin, tm):
    t = x2d.shape[0]
    const = lambda i: (0, 0)
    tile = lambda n: pl.BlockSpec((tm, n), lambda i: (i, 0))
    weight = lambda a: pl.BlockSpec(a.shape, const, pipeline_mode=pl.Buffered(1))
    return pl.pallas_call(
        _post_kernel,
        grid=(t // tm,),
        in_specs=[tile(D_MODEL), tile(WIDTH), tile(WIDTH), tile(GATES),
                  weight(wua), weight(wub), weight(wo), weight(gffn),
                  weight(wfi), weight(wfo), weight(gfin)],
        out_specs=tile(D_MODEL),
        out_shape=jax.ShapeDtypeStruct((t, D_MODEL), F32),
        compiler_params=pltpu.CompilerParams(
            dimension_semantics=("arbitrary",), vmem_limit_bytes=VMEM_LIMIT),
        name="post",
    )(x2d, ya, yb, gates, wua, wub, wo, gffn, wfi, wfo, gfin)


def _layer(x, norm_mix_g, w_in, mu_shift, w0, w_decay_up, a0, w_iclr_up, w_gate_up,
           k_k, k_a, r_k, ln_x_w, ln_x_b, w_up_a, w_up_b, w_o, norm_ffn_g,
           w_ffn_in, w_ffn_out, out_g):
    b, s, _ = x.shape
    t = b * s
    x2d = x.reshape(t, D_MODEL)

    pad = SHIFT_PAD - SHIFT_WIDTH
    w_pad = jnp.concatenate(
        [w_in[:, :SHIFT_WIDTH], jnp.zeros((D_MODEL, pad), F32), w_in[:, SHIFT_WIDTH:]],
        axis=1).astype(BF16)
    mu = jnp.pad(mu_shift, (0, pad)).reshape(1, SHIFT_PAD)
    wda = jnp.zeros((LANES, 2 * WIDTH), F32)
    wda = wda.at[:DECAY_LORA, :WIDTH].set(w_decay_up).at[DECAY_LORA:, WIDTH:].set(w_iclr_up)
    wg = jnp.pad(w_gate_up, ((0, 2 * LANES - GATE_LORA), (0, 0)))
    row = lambda a: a.reshape(1, -1)

    tt = 256
    hid = jnp.arange(WIDTH) // HEAD_DIM
    seg = (hid[:, None] == hid[None, :]).astype(BF16)
    ti = jnp.arange(tt)
    tri = ((ti[:, None] >= ti[None, :]) & (ti[:, None] // CHUNK == ti[None, :] // CHUNK)).astype(BF16)

    sp, qkv, gates = _proj(x2d, row(norm_mix_g), w_pad, tm=256)
    ya = _rwkv(sp.reshape(b, s, SHIFT_PAD), mu, row(w0), wda.astype(BF16), row(a0),
               wg.astype(BF16), row(k_k), row(k_a), row(r_k), row(ln_x_w), row(ln_x_b),
               seg, tri, tt)
    slopes = jnp.exp2(-8.0 * jnp.arange(1, N_HEADS + 1, dtype=F32) / N_HEADS)
    slopes = jnp.broadcast_to(
        jnp.pad(slopes.reshape(N_PAIRS, 2), ((0, 0), (0, 6)))[:, :, None], (N_PAIRS, 8, MOBA_BLOCK))
    yb = _moba(qkv.reshape(b, s, QKV_B), slopes)
    out = _post(x2d, ya.reshape(t, WIDTH), yb.reshape(t, WIDTH), gates,
                w_up_a.astype(BF16), w_up_b.astype(BF16), w_o.astype(BF16), row(norm_ffn_g),
                w_ffn_in.astype(BF16), w_ffn_out.astype(BF16), row(out_g), tm=256)
    return out.reshape(b, s, D_MODEL)


def kernel(x, norm_mix_g, w_in, mu_shift, w0, w_decay_up, a0, w_iclr_up, w_gate_up, k_k, k_a, r_k, ln_x_w, ln_x_b, w_up_a, w_up_b, w_o, norm_ffn_g, w_ffn_in, w_ffn_out, norm_final_g):
    depth = w_in.shape[0]
    assert depth == 1, "the fused post kernel applies the final norm after the only layer"
    return _layer(x, norm_mix_g[0], w_in[0], mu_shift[0], w0[0], w_decay_up[0], a0[0],
                  w_iclr_up[0], w_gate_up[0], k_k[0], k_a[0], r_k[0].reshape(-1), ln_x_w[0],
                  ln_x_b[0], w_up_a[0], w_up_b[0], w_o[0], norm_ffn_g[0], w_ffn_in[0],
                  w_ffn_out[0], norm_final_g)
```

```python
import functools

import jax
import jax.numpy as jnp
from jax import lax
from jax.experimental import pallas as pl
from jax.experimental.pallas import tpu as pltpu

F32 = jnp.float32
BF16 = jnp.bfloat16

D_MODEL = 1024
HEAD_DIM = 64
N_HEADS = 8
WIDTH = N_HEADS * HEAD_DIM
N_PAIRS = N_HEADS // 2
LANES = 128
DECAY_LORA = 64
ICLR_LORA = 64
GATE_LORA = 160
GN_EPS = 64e-5
RMS_EPS = 1e-6
MOBA_BLOCK = 256
MOBA_TOP_K = 3
FFN_HIDDEN = 2816
SHIFT_WIDTH = 3 * WIDTH + DECAY_LORA + ICLR_LORA + GATE_LORA
SHIFT_PAD = 1920
QKV_B = 3 * WIDTH
GATES = 2 * D_MODEL
PROJ_PAD = SHIFT_PAD + QKV_B + GATES

CHUNK = 64
VMEM_LIMIT = 56 * 1024 * 1024


def _dot(a, b):
    return jnp.dot(a, b, preferred_element_type=F32)


def _dot_nt(a, b):
    return lax.dot_general(a, b, (((1,), (1,)), ((), ())), preferred_element_type=F32)


def _rms(x, g):
    return x * lax.rsqrt(jnp.mean(x * x, axis=-1, keepdims=True) + RMS_EPS) * g


def _proj_kernel(x_ref, g_ref, w_ref, sp_ref, qkv_ref, gate_ref):
    h = _rms(x_ref[...], g_ref[...]).astype(BF16)
    sp_ref[...] = _dot(h, w_ref[:, 0:SHIFT_PAD])
    qkv_ref[...] = _dot(h, w_ref[:, SHIFT_PAD:SHIFT_PAD + QKV_B]).astype(BF16)
    gate_ref[...] = _dot(h, w_ref[:, SHIFT_PAD + QKV_B:PROJ_PAD]).astype(BF16)


def _proj(x2d, g, w, tm):
    t = x2d.shape[0]
    const = lambda i: (0, 0)
    return pl.pallas_call(
        _proj_kernel,
        grid=(t // tm,),
        in_specs=[
            pl.BlockSpec((tm, D_MODEL), lambda i: (i, 0)),
            pl.BlockSpec((1, D_MODEL), const),
            pl.BlockSpec((D_MODEL, PROJ_PAD), const, pipeline_mode=pl.Buffered(1)),
        ],
        out_specs=[
            pl.BlockSpec((tm, SHIFT_PAD), lambda i: (i, 0)),
            pl.BlockSpec((tm, QKV_B), lambda i: (i, 0)),
            pl.BlockSpec((tm, GATES), lambda i: (i, 0)),
        ],
        out_shape=[
            jax.ShapeDtypeStruct((t, SHIFT_PAD), F32),
            jax.ShapeDtypeStruct((t, QKV_B), BF16),
            jax.ShapeDtypeStruct((t, GATES), BF16),
        ],
        compiler_params=pltpu.CompilerParams(
            dimension_semantics=("arbitrary",), vmem_limit_bytes=VMEM_LIMIT),
        name="proj",
    )(x2d, g, w)


def _split2(x):
    hi = x.astype(BF16)
    lo = (x - hi.astype(F32)).astype(BF16)
    return hi, lo


def _segsum(x, seg):
    hi, lo = _split2(x)
    return _dot(hi, seg) + _dot(lo, seg)


def _rwkv_kernel(sp_ref, mu_ref, w0_ref, wda_ref, a0_ref, wg_ref, kk_ref, ka_ref, rk_ref,
                 lnw_ref, lnb_ref, seg_ref, tri_ref, o_ref, h_ref, carry_ref, *, tt):
    n_chunks = tt // CHUNK

    @pl.when(pl.program_id(1) == 0)
    def _():
        carry_ref[...] = jnp.zeros_like(carry_ref)
        h_ref[...] = jnp.zeros_like(h_ref)

    p = sp_ref[0]
    row = lax.broadcasted_iota(jnp.int32, (tt, 1), 0)
    prev = jnp.where(row == 0, carry_ref[0:1, :], pltpu.roll(p, 1, 0))
    carry_ref[0:1, :] = p[tt - 1:tt, :]
    xs = p + mu_ref[...] * (prev - p)

    r = xs[:, 0:WIDTH]
    k = xs[:, WIDTH:2 * WIDTH]
    v = xs[:, 2 * WIDTH:3 * WIDTH]
    da = xs[:, 3 * WIDTH:3 * WIDTH + LANES]
    gd = xs[:, 3 * WIDTH + LANES:SHIFT_PAD]
    lane = lax.broadcasted_iota(jnp.int32, (1, LANES), 1)
    lo_half = lane < HEAD_DIM
    da_act = jnp.where(lo_half, jnp.tanh(da), da).astype(BF16)
    lora = _dot(da_act, wda_ref[...])
    g = _dot(jax.nn.sigmoid(gd).astype(BF16), wg_ref[...])

    z = -(w0_ref[...] + lora[:, 0:WIDTH])
    softplus = jnp.maximum(z, 0.0) + jnp.log1p(jnp.exp(-jnp.abs(z)))
    logdec = -jnp.exp(-softplus - 0.5)
    a = jax.nn.sigmoid(a0_ref[...] + lora[:, WIDTH:2 * WIDTH])

    seg = seg_ref[...]
    kkf = k * kk_ref[...]
    norm = jnp.sqrt(_segsum(kkf * kkf, seg))
    kk = kkf / jnp.maximum(norm, 1e-12)
    k2 = k * (1.0 + (a - 1.0) * ka_ref[...])
    bonus = _segsum(r * k2 * rk_ref[...], seg) * v

    l1 = logdec.astype(BF16)
    r1 = logdec - l1.astype(F32)
    l2 = r1.astype(BF16)
    l3 = (r1 - l2.astype(F32)).astype(BF16)
    tri = tri_ref[...]
    cum = _dot(tri, l1) + _dot(tri, l2) + _dot(tri, l3)

    kka = kk * a
    p_inv = jnp.exp(-cum)
    a_t = -kk * jnp.exp(cum - logdec)
    r_t = r * jnp.exp(cum)
    b_t = kka * p_inv
    k_t = k2 * p_inv

    r64 = lax.broadcasted_iota(jnp.int32, (CHUNK, LANES), 0)
    c64 = lax.broadcasted_iota(jnp.int32, (CHUNK, LANES), 1) & (CHUNK - 1)
    strict = c64 < r64
    incl = c64 <= r64
    eye_side = (c64 == r64).astype(F32)
    r128 = lax.broadcasted_iota(jnp.int32, (LANES, LANES), 0)
    c128 = lax.broadcasted_iota(jnp.int32, (LANES, LANES), 1)
    eye128 = (r128 == c128).astype(F32)
    same_head = (r128 >> 6) == (c128 >> 6)
    hi_half = jnp.logical_not(lo_half)
    zeros_c = jnp.zeros((CHUNK, LANES), F32)

    def _stack2(zz):
        return jnp.concatenate([jnp.where(lo_half, zz, 0.0), jnp.where(hi_half, zz, 0.0)],
                               axis=0).astype(BF16)

    chains = [(c, pr) for c in range(n_chunks) for pr in range(N_PAIRS)]

    def _piece(arr, c, pr):
        return arr[c * CHUNK:(c + 1) * CHUNK, pr * LANES:(pr + 1) * LANES]

    x_ab, x_ak, x_rb, x_rk = {}, {}, {}, {}
    for ch in chains:
        lq = jnp.concatenate([_piece(a_t, *ch), _piece(r_t, *ch)], axis=0).astype(BF16)
        rhs = jnp.concatenate([_stack2(_piece(b_t, *ch)), _stack2(_piece(k_t, *ch))], axis=0)
        sc = _dot_nt(lq, rhs)
        x_ab[ch] = jnp.where(strict, sc[0:CHUNK, 0:LANES], 0.0)
        x_ak[ch] = jnp.where(strict, sc[0:CHUNK, LANES:2 * LANES], 0.0)
        x_rb[ch] = jnp.where(incl, sc[CHUNK:2 * CHUNK, 0:LANES], 0.0)
        x_rk[ch] = jnp.where(incl, sc[CHUNK:2 * CHUNK, LANES:2 * LANES], 0.0)

    x_t = {ch: eye_side + x_ab[ch] for ch in chains}
    x_pw = {ch: _dot(x_ab[ch].astype(BF16), _stack2(x_ab[ch])) for ch in chains}
    av = {ch: _dot(x_ak[ch].astype(BF16), _stack2(_piece(v, *ch))) for ch in chains}
    for _ in range(4):
        for ch in chains:
            both = jnp.concatenate([x_t[ch], x_pw[ch]], axis=0).astype(BF16)
            out = _dot(both, _stack2(x_pw[ch]))
            x_t[ch] = x_t[ch] + out[0:CHUNK]
            x_pw[ch] = out[CHUNK:2 * CHUNK]
    for ch in chains:
        x_t[ch] = x_t[ch] + _dot(x_t[ch].astype(BF16), _stack2(x_pw[ch]))

    tw = {}
    for ch in chains:
        rhs = jnp.concatenate([_stack2(_piece(a_t, *ch)), _stack2(av[ch])], axis=1)
        tw[ch] = _dot(x_t[ch].astype(BF16), rhs)
    qeff, yloc, phi, psi = {}, {}, {}, {}
    for ch in chains:
        ta, w_loc, v_p = tw[ch][:, 0:LANES], tw[ch][:, LANES:2 * LANES], _piece(v, *ch)
        rhs = jnp.concatenate(
            [jnp.concatenate([_stack2(ta), _stack2(w_loc)], axis=1),
             jnp.concatenate([jnp.zeros((LANES, LANES), BF16), _stack2(v_p)], axis=1)], axis=0)
        ag = _dot(jnp.concatenate([x_rb[ch], x_rk[ch]], axis=1).astype(BF16), rhs)
        qeff[ch] = _piece(r_t, *ch) + ag[:, 0:LANES]
        yloc[ch] = ag[:, LANES:2 * LANES]
        cum_c = _piece(cum, *ch)
        cum_last = cum_c[CHUNK - 1:CHUNK, :]
        to_end = jnp.exp(cum_last - cum_c)
        kk_t = jnp.concatenate([_piece(kka, *ch) * to_end, _piece(k2, *ch) * to_end], axis=0).T
        rhs = jnp.concatenate(
            [tw[ch], jnp.concatenate([zeros_c, v_p], axis=1)], axis=0).astype(BF16)
        pp = _dot(kk_t.astype(BF16), rhs)
        phi[ch] = eye128 * jnp.exp(cum_last) + jnp.where(same_head, pp[:, 0:LANES], 0.0)
        psi[ch] = jnp.where(same_head, pp[:, LANES:2 * LANES], 0.0)

    y_rows = []
    states = [h_ref[pr] for pr in range(N_PAIRS)]
    for c in range(n_chunks):
        y_c = []
        for pr in range(N_PAIRS):
            ch = (c, pr)
            lhs = jnp.concatenate([qeff[ch], phi[ch]], axis=0).astype(BF16)
            out = _dot(lhs, states[pr].astype(BF16))
            y_c.append(out[0:CHUNK] + yloc[ch])
            states[pr] = out[CHUNK:CHUNK + LANES] + psi[ch]
        y_rows.append(jnp.concatenate(y_c, axis=1))
    for pr in range(N_PAIRS):
        h_ref[pr] = states[pr]
    y = jnp.concatenate(y_rows, axis=0)

    mean = _segsum(y, seg) * (1.0 / HEAD_DIM)
    d = y - mean
    var = _segsum(d * d, seg) * (1.0 / HEAD_DIM)
    yn = d * lax.rsqrt(var + GN_EPS) * lnw_ref[...] + lnb_ref[...]
    o_ref[0] = ((yn + bonus) * g).astype(BF16)


def _rwkv(sp, mu, w0, wda, a0, wg, k_k, k_a, r_k, ln_w, ln_b, seg, tri, tt):
    b, s, _ = sp.shape
    const2 = lambda i, j: (0, 0)
    row = lambda n: pl.BlockSpec((1, n), const2)
    return pl.pallas_call(
        functools.partial(_rwkv_kernel, tt=tt),
        grid=(b, s // tt),
        in_specs=[
            pl.BlockSpec((1, tt, SHIFT_PAD), lambda i, j: (i, j, 0)),
            row(SHIFT_PAD), row(WIDTH),
            pl.BlockSpec((LANES, 2 * WIDTH), const2),
            row(WIDTH),
            pl.BlockSpec((2 * LANES, WIDTH), const2),
            row(WIDTH), row(WIDTH), row(WIDTH), row(WIDTH), row(WIDTH),
            pl.BlockSpec((WIDTH, WIDTH), const2),
            pl.BlockSpec((tt, tt), const2),
        ],
        out_specs=pl.BlockSpec((1, tt, WIDTH), lambda i, j: (i, j, 0)),
        out_shape=jax.ShapeDtypeStruct((b, s, WIDTH), BF16),
        scratch_shapes=[
            pltpu.VMEM((N_PAIRS, LANES, LANES), F32),
            pltpu.VMEM((8, SHIFT_PAD), F32),
        ],
        compiler_params=pltpu.CompilerParams(
            dimension_semantics=("arbitrary", "arbitrary"), vmem_limit_bytes=VMEM_LIMIT),
        name="rwkv",
    )(sp, mu, w0, wda, a0, wg, k_k, k_a, r_k, ln_w, ln_b, seg, tri)


NEG_BIG = -(2.0 ** 30)
N_BLOCKS_MAX = 16


def _aligned(i, m):
    return i * m if isinstance(i, int) else pl.multiple_of(i * m, m)


def _moba_kernel(q_ref, k_ref, v_ref, slope_ref, o_ref, kx_ref, vt_ref, km_ref, *, seq):
    nb = seq // MOBA_BLOCK
    bs = MOBA_BLOCK
    scale = HEAD_DIM ** -0.5
    heads = range(N_HEADS)

    rr = lax.broadcasted_iota(jnp.int32, (bs, LANES), 0)
    ll = lax.broadcasted_iota(jnp.int32, (bs, LANES), 1)
    for n in range(nb):
        aug = jnp.where(ll < 2, 1.0,
              jnp.where(ll == 2, rr.astype(F32),
              jnp.where(ll == 3, float(n * bs),
              jnp.where(ll == 8 + n, 1.0, 0.0)))).astype(BF16)
        rows = slice(n * bs, (n + 1) * bs)
        for pr in range(N_PAIRS):
            ln = slice(pr * LANES, (pr + 1) * LANES)
            kx_ref[pr, n, :, 0:LANES] = k_ref[0, rows, ln]
            kx_ref[pr, n, :, LANES:2 * LANES] = aug
            vt_ref[pr, n] = v_ref[0, rows, ln].astype(F32).T.astype(BF16)

    bi = lax.broadcasted_iota(jnp.int32, (N_BLOCKS_MAX, seq), 0)
    si = lax.broadcasted_iota(jnp.int32, (N_BLOCKS_MAX, seq), 1)
    avg = jnp.where((si >= bi * bs) & (si < (bi + 1) * bs), 1.0 / bs, 0.0).astype(BF16)
    kmean = _dot(avg, k_ref[0])
    km = jnp.concatenate([jnp.broadcast_to(kmean[n:n + 1, :], (N_HEADS, WIDTH))
                          for n in range(N_BLOCKS_MAX)], axis=0)
    row_head = lax.broadcasted_iota(jnp.int32, (LANES, WIDTH), 0) & (N_HEADS - 1)
    lane_head = lax.broadcasted_iota(jnp.int32, (LANES, WIDTH), 1) >> 6
    km_hi, km_lo = _split2(jnp.where(row_head == lane_head, km, 0.0))
    km_ref[0] = km_hi
    km_ref[1] = km_lo

    r128 = lax.broadcasted_iota(jnp.int32, (LANES, 1), 0)
    row_masks = (r128 < HEAD_DIM, r128 >= HEAD_DIM)
    pr_i = lax.broadcasted_iota(jnp.int32, (LANES, LANES), 0)
    pc_i = lax.broadcasted_iota(jnp.int32, (LANES, LANES), 1)
    perm = (pc_i == ((pr_i & (N_BLOCKS_MAX - 1)) << 3) + (pr_i >> 4)).astype(BF16)
    r8 = lax.broadcasted_iota(jnp.int32, (8, bs), 0)
    tl = lax.broadcasted_iota(jnp.int32, (8, bs), 1).astype(F32)
    key_row = lax.broadcasted_iota(jnp.int32, (bs, bs), 0)
    qry_col = lax.broadcasted_iota(jnp.int32, (bs, bs), 1)
    causal = key_row <= qry_col
    zeros_pad = jnp.zeros((LANES - 8 - N_BLOCKS_MAX, bs), F32)
    zeros_cnt = jnp.zeros((N_HEADS, bs), F32)
    vrows = (slice(0, HEAD_DIM), slice(HEAD_DIM, 2 * HEAD_DIM))

    def q_block(i, carry):
        start = _aligned(i, bs)
        q_blk = q_ref[0, pl.ds(start, bs), :].astype(F32)
        q_t = [q_blk[:, pr * LANES:(pr + 1) * LANES].T for pr in range(N_PAIRS)]
        q_tb = jnp.concatenate(q_t, axis=0).astype(BF16)
        t0 = jnp.asarray(i * bs, F32)

        gate = _dot(km_ref[0], q_tb) + _dot(km_ref[1], q_tb)
        slabs = [jnp.where(n < i, gate[n * N_HEADS:(n + 1) * N_HEADS], -jnp.inf)
                 for n in range(nb)]
        cnt = [zeros_cnt] * nb
        for n in range(nb):
            for m in range(n):
                m_wins = slabs[m] >= slabs[n]
                cnt[n] = cnt[n] + jnp.where(m_wins, 1.0, 0.0)
                cnt[m] = cnt[m] + jnp.where(m_wins, 0.0, 1.0)
        bias = [jnp.where(jnp.logical_or(cnt[n] < MOBA_TOP_K, n >= i), 0.0, NEG_BIG)
                for n in range(nb)]
        bias += [zeros_cnt] * (N_BLOCKS_MAX - nb)
        bias = _dot(perm, jnp.concatenate(bias, axis=0).astype(BF16))

        q_aug = []
        for h in heads:
            slope = slope_ref[h:h + 1, :]
            aug8 = jnp.where(r8 == 0, -slope * tl,
                   jnp.where(r8 == 1, -slope * t0,
                   jnp.where((r8 == 2) | (r8 == 3), slope, 0.0)))
            top = jnp.where(row_masks[h % 2], q_t[h // 2], 0.0) * scale
            sel = bias[h * N_BLOCKS_MAX:(h + 1) * N_BLOCKS_MAX]
            q_aug.append(jnp.concatenate([top, aug8, sel, zeros_pad], axis=0).astype(BF16))

        def attend(j, first, st):
            kx = [kx_ref[pr, j] for pr in range(N_PAIRS)]
            vt = [vt_ref[pr, j] for pr in range(N_PAIRS)]
            s = [_dot(kx[h // 2], q_aug[h]) for h in heads]
            ps, out = [], []
            for h in heads:
                if first:
                    sh = jnp.where(causal, s[h], NEG_BIG)
                    m_new = jnp.max(sh, axis=0, keepdims=True)
                    p = jnp.exp(sh - m_new)
                    l = jnp.sum(p, axis=0, keepdims=True)
                    out.append([m_new, l, None])
                else:
                    m, l, acc = st[3 * h:3 * h + 3]
                    m_new = jnp.maximum(m, jnp.max(s[h], axis=0, keepdims=True))
                    alpha = jnp.exp(m - m_new)
                    p = jnp.exp(s[h] - m_new)
                    l = alpha * l + jnp.sum(p, axis=0, keepdims=True)
                    out.append([m_new, l, alpha * acc])
                ps.append(p.astype(BF16))
            for h in heads:
                pv = _dot(vt[h // 2][vrows[h % 2], :], ps[h])
                out[h][2] = pv if first else out[h][2] + pv
            return tuple(x for o in out for x in o)

        st = attend(i, True, None)
        st = lax.fori_loop(0, i, lambda j, st: attend(j, False, st), st)
        pairs = []
        for pr in range(N_PAIRS):
            a, b = 2 * pr, 2 * pr + 1
            out_t = jnp.concatenate([st[3 * a + 2] / st[3 * a + 1],
                                     st[3 * b + 2] / st[3 * b + 1]], axis=0)
            pairs.append(out_t.T)
        o_ref[0, pl.ds(start, bs), :] = jnp.concatenate(pairs, axis=1).astype(BF16)
        return carry

    lax.fori_loop(0, nb, q_block, 0)


def _moba(qkv, slopes):
    b, s, _ = qkv.shape
    assert s % MOBA_BLOCK == 0 and s // MOBA_BLOCK <= N_BLOCKS_MAX
    nb = s // MOBA_BLOCK
    col = lambda c: pl.BlockSpec((1, s, WIDTH), lambda i: (i, 0, c), pipeline_mode=pl.Buffered(1))
    return pl.pallas_call(
        functools.partial(_moba_kernel, seq=s),
        grid=(b,),
        in_specs=[col(0), col(1), col(2), pl.BlockSpec((N_HEADS, MOBA_BLOCK), lambda i: (0, 0))],
        out_specs=pl.BlockSpec((1, s, WIDTH), lambda i: (i, 0, 0)),
        out_shape=jax.ShapeDtypeStruct((b, s, WIDTH), BF16),
        scratch_shapes=[
            pltpu.VMEM((N_PAIRS, nb, MOBA_BLOCK, 2 * LANES), BF16),
            pltpu.VMEM((N_PAIRS, nb, LANES, MOBA_BLOCK), BF16),
            pltpu.VMEM((2, N_BLOCKS_MAX * N_HEADS, WIDTH), BF16),
        ],
        compiler_params=pltpu.CompilerParams(
            dimension_semantics=("arbitrary",), vmem_limit_bytes=VMEM_LIMIT),
        name="moba",
    )(qkv, qkv, qkv, slopes)


FFN_CHUNKS = ((0, 1024), (1024, 1024), (2048, 768))


def _post_kernel(x_ref, ya_ref, yb_ref, gate_ref, wua_ref, wub_ref, wo_ref, gffn_ref,
                 wfi_ref, wfo_ref, gfin_ref, o_ref):
    y_a = _dot(ya_ref[...], wua_ref[...])
    y_b = _dot(yb_ref[...], wub_ref[...])
    ga = jax.nn.sigmoid(gate_ref[:, 0:D_MODEL].astype(F32))
    gb = jax.nn.sigmoid(gate_ref[:, D_MODEL:2 * D_MODEL].astype(F32))
    mixed = (ga * y_a + gb * y_b).astype(BF16)
    x1 = x_ref[...] + _dot(mixed, wo_ref[...])
    h = _rms(x1, gffn_ref[...]).astype(BF16)
    acc = x1
    for off, n in FFN_CHUNKS:
        gg = _dot(h, wfi_ref[:, off:off + n])
        uu = _dot(h, wfi_ref[:, FFN_HIDDEN + off:FFN_HIDDEN + off + n])
        act = (gg * jax.nn.sigmoid(gg) * uu).astype(BF16)
        acc = acc + _dot(act, wfo_ref[off:off + n, :])
    o_ref[...] = _rms(acc, gfin_ref[...])


def _post(x2d, ya, yb, gates, wua, wub, wo, gffn, wfi, wfo, gfin, tm):
    t = x2d.shape[0]
    const = lambda i: (0, 0)
    tile = lambda n: pl.BlockSpec((tm, n), lambda i: (i, 0))
    weight = lambda a: pl.BlockSpec(a.shape, const, pipeline_mode=pl.Buffered(1))
    return pl.pallas_call(
        _post_kernel,
        grid=(t // tm,),
        in_specs=[tile(D_MODEL), tile(WIDTH), tile(WIDTH), tile(GATES),
                  weight(wua), weight(wub), weight(wo), weight(gffn),
                  weight(wfi), weight(wfo), weight(gfin)],
        out_specs=tile(D_MODEL),
        out_shape=jax.ShapeDtypeStruct((t, D_MODEL), F32),
        compiler_params=pltpu.CompilerParams(
            dimension_semantics=("arbitrary",), vmem_limit_bytes=VMEM_LIMIT),
        name="post",
    )(x2d, ya, yb, gates, wua, wub, wo, gffn, wfi, wfo, gfin)


def _layer(x, norm_mix_g, w_in, mu_shift, w0, w_decay_up, a0, w_iclr_up, w_gate_up,
           k_k, k_a, r_k, ln_x_w, ln_x_b, w_up_a, w_up_b, w_o, norm_ffn_g,
           w_ffn_in, w_ffn_out, out_g):
    b, s, _ = x.shape
    t = b * s
    x2d = x.reshape(t, D_MODEL)

    pad = SHIFT_PAD - SHIFT_WIDTH
    w_pad = jnp.concatenate(
        [w_in[:, :SHIFT_WIDTH], jnp.zeros((D_MODEL, pad), F32), w_in[:, SHIFT_WIDTH:]],
        axis=1).astype(BF16)
    mu = jnp.pad(mu_shift, (0, pad)).reshape(1, SHIFT_PAD)
    wda = jnp.zeros((LANES, 2 * WIDTH), F32)
    wda = wda.at[:DECAY_LORA, :WIDTH].set(w_decay_up).at[DECAY_LORA:, WIDTH:].set(w_iclr_up)
    wg = jnp.pad(w_gate_up, ((0, 2 * LANES - GATE_LORA), (0, 0)))
    row = lambda a: a.reshape(1, -1)

    tt = 256
    hid = jnp.arange(WIDTH) // HEAD_DIM
    seg = (hid[:, None] == hid[None, :]).astype(BF16)
    ti = jnp.arange(tt)
    tri = ((ti[:, None] >= ti[None, :]) & (ti[:, None] // CHUNK == ti[None, :] // CHUNK)).astype(BF16)

    sp, qkv, gates = _proj(x2d, row(norm_mix_g), w_pad, tm=256)
    ya = _rwkv(sp.reshape(b, s, SHIFT_PAD), mu, row(w0), wda.astype(BF16), row(a0),
               wg.astype(BF16), row(k_k), row(k_a), row(r_k), row(ln_x_w), row(ln_x_b),
               seg, tri, tt)
    slopes = jnp.exp2(-8.0 * jnp.arange(1, N_HEADS + 1, dtype=F32) / N_HEADS)
    slopes = jnp.broadcast_to(slopes[:, None], (N_HEADS, MOBA_BLOCK))
    yb = _moba(qkv.reshape(b, s, QKV_B), slopes)
    out = _post(x2d, ya.reshape(t, WIDTH), yb.reshape(t, WIDTH), gates,
                w_up_a.astype(BF16), w_up_b.astype(BF16), w_o.astype(BF16), row(norm_ffn_g),
                w_ffn_in.astype(BF16), w_ffn_out.astype(BF16), row(out_g), tm=256)
    return out.reshape(b, s, D_MODEL)


def kernel(x, norm_mix_g, w_in, mu_shift, w0, w_decay_up, a0, w_iclr_up, w_gate_up, k_k, k_a, r_k, ln_x_w, ln_x_b, w_up_a, w_up_b, w_o, norm_ffn_g, w_ffn_in, w_ffn_out, norm_final_g):
    depth = w_in.shape[0]
    assert depth == 1, "the fused post kernel applies the final norm after the only layer"
    return _layer(x, norm_mix_g[0], w_in[0], mu_shift[0], w0[0], w_decay_up[0], a0[0],
                  w_iclr_up[0], w_gate_up[0], k_k[0], k_a[0], r_k[0].reshape(-1), ln_x_w[0],
                  ln_x_b[0], w_up_a[0], w_up_b[0], w_o[0], norm_ffn_g[0], w_ffn_in[0],
                  w_ffn_out[0], norm_final_g)
```

```python
import functools

import jax
import jax.numpy as jnp
from jax import lax
from jax.experimental import pallas as pl
from jax.experimental.pallas import tpu as pltpu

F32 = jnp.float32
BF16 = jnp.bfloat16

D_MODEL = 1024
HEAD_DIM = 64
N_HEADS = 8
WIDTH = N_HEADS * HEAD_DIM
N_PAIRS = N_HEADS // 2
LANES = 128
DECAY_LORA = 64
ICLR_LORA = 64
GATE_LORA = 160
GN_EPS = 64e-5
RMS_EPS = 1e-6
MOBA_BLOCK = 256
MOBA_TOP_K = 3
FFN_HIDDEN = 2816
SHIFT_WIDTH = 3 * WIDTH + DECAY_LORA + ICLR_LORA + GATE_LORA
SHIFT_PAD = 1920
QKV_B = 3 * WIDTH
GATES = 2 * D_MODEL
PROJ_PAD = SHIFT_PAD + QKV_B + GATES

CHUNK = 64
VMEM_LIMIT = 56 * 1024 * 1024


def _dot(a, b):
    return jnp.dot(a, b, preferred_element_type=F32)


def _dot_nt(a, b):
    return lax.dot_general(a, b, (((1,), (1,)), ((), ())), preferred_element_type=F32)


def _rms(x, g):
    return x * lax.rsqrt(jnp.mean(x * x, axis=-1, keepdims=True) + RMS_EPS) * g


def _proj_kernel(x_ref, g_ref, w_ref, sp_ref, qkv_ref, gate_ref):
    h = _rms(x_ref[...], g_ref[...]).astype(BF16)
    sp_ref[...] = _dot(h, w_ref[:, 0:SHIFT_PAD])
    qkv_ref[...] = _dot(h, w_ref[:, SHIFT_PAD:SHIFT_PAD + QKV_B]).astype(BF16)
    gate_ref[...] = _dot(h, w_ref[:, SHIFT_PAD + QKV_B:PROJ_PAD]).astype(BF16)


def _proj(x2d, g, w, tm):
    t = x2d.shape[0]
    const = lambda i: (0, 0)
    return pl.pallas_call(
        _proj_kernel,
        grid=(t // tm,),
        in_specs=[
            pl.BlockSpec((tm, D_MODEL), lambda i: (i, 0)),
            pl.BlockSpec((1, D_MODEL), const),
            pl.BlockSpec((D_MODEL, PROJ_PAD), const, pipeline_mode=pl.Buffered(1)),
        ],
        out_specs=[
            pl.BlockSpec((tm, SHIFT_PAD), lambda i: (i, 0)),
            pl.BlockSpec((tm, QKV_B), lambda i: (i, 0)),
            pl.BlockSpec((tm, GATES), lambda i: (i, 0)),
        ],
        out_shape=[
            jax.ShapeDtypeStruct((t, SHIFT_PAD), F32),
            jax.ShapeDtypeStruct((t, QKV_B), BF16),
            jax.ShapeDtypeStruct((t, GATES), BF16),
        ],
        compiler_params=pltpu.CompilerParams(
            dimension_semantics=("arbitrary",), vmem_limit_bytes=VMEM_LIMIT),
        name="proj",
    )(x2d, g, w)


def _split2(x):
    hi = x.astype(BF16)
    lo = (x - hi.astype(F32)).astype(BF16)
    return hi, lo


def _segsum(x, seg):
    hi, lo = _split2(x)
    return _dot(hi, seg) + _dot(lo, seg)


def _rwkv_kernel(sp_ref, mu_ref, w0_ref, wda_ref, a0_ref, wg_ref, kk_ref, ka_ref, rk_ref,
                 lnw_ref, lnb_ref, seg_ref, tri_ref, o_ref, h_ref, carry_ref, *, tt):
    n_chunks = tt // CHUNK

    @pl.when(pl.program_id(1) == 0)
    def _():
        carry_ref[...] = jnp.zeros_like(carry_ref)
        h_ref[...] = jnp.zeros_like(h_ref)

    p = sp_ref[0]
    row = lax.broadcasted_iota(jnp.int32, (tt, 1), 0)
    prev = jnp.where(row == 0, carry_ref[0:1, :], pltpu.roll(p, 1, 0))
    carry_ref[0:1, :] = p[tt - 1:tt, :]
    xs = p + mu_ref[...] * (prev - p)

    r = xs[:, 0:WIDTH]
    k = xs[:, WIDTH:2 * WIDTH]
    v = xs[:, 2 * WIDTH:3 * WIDTH]
    da = xs[:, 3 * WIDTH:3 * WIDTH + LANES]
    gd = xs[:, 3 * WIDTH + LANES:SHIFT_PAD]
    lane = lax.broadcasted_iota(jnp.int32, (1, LANES), 1)
    lo_half = lane < HEAD_DIM
    da_act = jnp.where(lo_half, jnp.tanh(da), da).astype(BF16)
    lora = _dot(da_act, wda_ref[...])
    g = _dot(jax.nn.sigmoid(gd).astype(BF16), wg_ref[...])

    z = -(w0_ref[...] + lora[:, 0:WIDTH])
    softplus = jnp.maximum(z, 0.0) + jnp.log1p(jnp.exp(-jnp.abs(z)))
    logdec = -jnp.exp(-softplus - 0.5)
    a = jax.nn.sigmoid(a0_ref[...] + lora[:, WIDTH:2 * WIDTH])

    seg = seg_ref[...]
    kkf = k * kk_ref[...]
    norm = jnp.sqrt(_segsum(kkf * kkf, seg))
    kk = kkf / jnp.maximum(norm, 1e-12)
    k2 = k * (1.0 + (a - 1.0) * ka_ref[...])
    bonus = _segsum(r * k2 * rk_ref[...], seg) * v

    l1 = logdec.astype(BF16)
    r1 = logdec - l1.astype(F32)
    l2 = r1.astype(BF16)
    l3 = (r1 - l2.astype(F32)).astype(BF16)
    tri = tri_ref[...]
    cum = _dot(tri, l1) + _dot(tri, l2) + _dot(tri, l3)

    kka = kk * a
    p_inv = jnp.exp(-cum)
    a_t = -kk * jnp.exp(cum - logdec)
    r_t = r * jnp.exp(cum)
    b_t = kka * p_inv
    k_t = k2 * p_inv

    r64 = lax.broadcasted_iota(jnp.int32, (CHUNK, LANES), 0)
    c64 = lax.broadcasted_iota(jnp.int32, (CHUNK, LANES), 1) & (CHUNK - 1)
    strict = c64 < r64
    incl = c64 <= r64
    eye_side = (c64 == r64).astype(F32)
    r128 = lax.broadcasted_iota(jnp.int32, (LANES, LANES), 0)
    c128 = lax.broadcasted_iota(jnp.int32, (LANES, LANES), 1)
    eye128 = (r128 == c128).astype(F32)
    same_head = (r128 >> 6) == (c128 >> 6)
    hi_half = jnp.logical_not(lo_half)
    zeros_c = jnp.zeros((CHUNK, LANES), F32)

    def _stack2(zz):
        return jnp.concatenate([jnp.where(lo_half, zz, 0.0), jnp.where(hi_half, zz, 0.0)],
                               axis=0).astype(BF16)

    chains = [(c, pr) for c in range(n_chunks) for pr in range(N_PAIRS)]

    def _piece(arr, c, pr):
        return arr[c * CHUNK:(c + 1) * CHUNK, pr * LANES:(pr + 1) * LANES]

    x_ab, x_ak, x_rb, x_rk = {}, {}, {}, {}
    for ch in chains:
        lq = jnp.concatenate([_piece(a_t, *ch), _piece(r_t, *ch)], axis=0).astype(BF16)
        rhs = jnp.concatenate([_stack2(_piece(b_t, *ch)), _stack2(_piece(k_t, *ch))], axis=0)
        sc = _dot_nt(lq, rhs)
        x_ab[ch] = jnp.where(strict, sc[0:CHUNK, 0:LANES], 0.0)
        x_ak[ch] = jnp.where(strict, sc[0:CHUNK, LANES:2 * LANES], 0.0)
        x_rb[ch] = jnp.where(incl, sc[CHUNK:2 * CHUNK, 0:LANES], 0.0)
        x_rk[ch] = jnp.where(incl, sc[CHUNK:2 * CHUNK, LANES:2 * LANES], 0.0)

    x_t = {ch: eye_side + x_ab[ch] for ch in chains}
    x_pw = {ch: _dot(x_ab[ch].astype(BF16), _stack2(x_ab[ch])) for ch in chains}
    av = {ch: _dot(x_ak[ch].astype(BF16), _stack2(_piece(v, *ch))) for ch in chains}
    for _ in range(4):
        for ch in chains:
            both = jnp.concatenate([x_t[ch], x_pw[ch]], axis=0).astype(BF16)
            out = _dot(both, _stack2(x_pw[ch]))
            x_t[ch] = x_t[ch] + out[0:CHUNK]
            x_pw[ch] = out[CHUNK:2 * CHUNK]
    for ch in chains:
        x_t[ch] = x_t[ch] + _dot(x_t[ch].astype(BF16), _stack2(x_pw[ch]))

    tw = {}
    for ch in chains:
        rhs = jnp.concatenate([_stack2(_piece(a_t, *ch)), _stack2(av[ch])], axis=1)
        tw[ch] = _dot(x_t[ch].astype(BF16), rhs)
    qeff, yloc, phi, psi = {}, {}, {}, {}
    for ch in chains:
        ta, w_loc, v_p = tw[ch][:, 0:LANES], tw[ch][:, LANES:2 * LANES], _piece(v, *ch)
        rhs = jnp.concatenate(
            [jnp.concatenate([_stack2(ta), _stack2(w_loc)], axis=1),
             jnp.concatenate([jnp.zeros((LANES, LANES), BF16), _stack2(v_p)], axis=1)], axis=0)
        ag = _dot(jnp.concatenate([x_rb[ch], x_rk[ch]], axis=1).astype(BF16), rhs)
        qeff[ch] = _piece(r_t, *ch) + ag[:, 0:LANES]
        yloc[ch] = ag[:, LANES:2 * LANES]
        cum_c = _piece(cum, *ch)
        cum_last = cum_c[CHUNK - 1:CHUNK, :]
        to_end = jnp.exp(cum_last - cum_c)
        kk_t = jnp.concatenate([_piece(kka, *ch) * to_end, _piece(k2, *ch) * to_end], axis=0).T
        rhs = jnp.concatenate(
            [tw[ch], jnp.concatenate([zeros_c, v_p], axis=1)], axis=0).astype(BF16)
        pp = _dot(kk_t.astype(BF16), rhs)
        phi[ch] = eye128 * jnp.exp(cum_last) + jnp.where(same_head, pp[:, 0:LANES], 0.0)
        psi[ch] = jnp.where(same_head, pp[:, LANES:2 * LANES], 0.0)

    y_rows = []
    states = [h_ref[pr] for pr in range(N_PAIRS)]
    for c in range(n_chunks):
        y_c = []
        for pr in range(N_PAIRS):
            ch = (c, pr)
            lhs = jnp.concatenate([qeff[ch], phi[ch]], axis=0).astype(BF16)
            out = _dot(lhs, states[pr].astype(BF16))
            y_c.append(out[0:CHUNK] + yloc[ch])
            states[pr] = out[CHUNK:CHUNK + LANES] + psi[ch]
        y_rows.append(jnp.concatenate(y_c, axis=1))
    for pr in range(N_PAIRS):
        h_ref[pr] = states[pr]
    y = jnp.concatenate(y_rows, axis=0)

    mean = _segsum(y, seg) * (1.0 / HEAD_DIM)
    d = y - mean
    var = _segsum(d * d, seg) * (1.0 / HEAD_DIM)
    yn = d * lax.rsqrt(var + GN_EPS) * lnw_ref[...] + lnb_ref[...]
    o_ref[0] = ((yn + bonus) * g).astype(BF16)


def _rwkv(sp, mu, w0, wda, a0, wg, k_k, k_a, r_k, ln_w, ln_b, seg, tri, tt):
    b, s, _ = sp.shape
    const2 = lambda i, j: (0, 0)
    row = lambda n: pl.BlockSpec((1, n), const2)
    return pl.pallas_call(
        functools.partial(_rwkv_kernel, tt=tt),
        grid=(b, s // tt),
        in_specs=[
            pl.BlockSpec((1, tt, SHIFT_PAD), lambda i, j: (i, j, 0)),
            row(SHIFT_PAD), row(WIDTH),
            pl.BlockSpec((LANES, 2 * WIDTH), const2),
            row(WIDTH),
            pl.BlockSpec((2 * LANES, WIDTH), const2),
            row(WIDTH), row(WIDTH), row(WIDTH), row(WIDTH), row(WIDTH),
            pl.BlockSpec((WIDTH, WIDTH), const2),
            pl.BlockSpec((tt, tt), const2),
        ],
        out_specs=pl.BlockSpec((1, tt, WIDTH), lambda i, j: (i, j, 0)),
        out_shape=jax.ShapeDtypeStruct((b, s, WIDTH), BF16),
        scratch_shapes=[
            pltpu.VMEM((N_PAIRS, LANES, LANES), F32),
            pltpu.VMEM((8, SHIFT_PAD), F32),
        ],
        compiler_params=pltpu.CompilerParams(
            dimension_semantics=("arbitrary", "arbitrary"), vmem_limit_bytes=VMEM_LIMIT),
        name="rwkv",
    )(sp, mu, w0, wda, a0, wg, k_k, k_a, r_k, ln_w, ln_b, seg, tri)


NEG_BIG = -(2.0 ** 30)
N_BLOCKS_MAX = 16
V_ROWS = HEAD_DIM + 16
LOG2E = 1.4426950408889634


def _aligned(i, m):
    return i * m if isinstance(i, int) else pl.multiple_of(i * m, m)


def _moba_kernel(q_ref, k_ref, v_ref, slope_ref, o_ref, kx_ref, vt_ref, km_ref, *, seq):
    nb = seq // MOBA_BLOCK
    bs = MOBA_BLOCK
    scale = HEAD_DIM ** -0.5
    heads = range(N_HEADS)

    rr = lax.broadcasted_iota(jnp.int32, (bs, LANES), 0)
    ll = lax.broadcasted_iota(jnp.int32, (bs, LANES), 1)
    ones_rows = jnp.ones((V_ROWS - HEAD_DIM, bs), BF16)
    for n in range(nb):
        aug = jnp.where(ll < 4, 1.0,
              jnp.where(ll < 6, rr.astype(F32),
              jnp.where(ll < 8, float(n * bs),
              jnp.where(ll == 8 + n, 1.0, 0.0)))).astype(BF16)
        rows = slice(n * bs, (n + 1) * bs)
        for pr in range(N_PAIRS):
            ln = slice(pr * LANES, (pr + 1) * LANES)
            kx_ref[pr, n, :, 0:LANES] = k_ref[0, rows, ln]
            kx_ref[pr, n, :, LANES:2 * LANES] = aug
            v_t = v_ref[0, rows, ln].astype(F32).T.astype(BF16)
            for half in range(2):
                vt_ref[2 * pr + half, n] = jnp.concatenate(
                    [v_t[half * HEAD_DIM:(half + 1) * HEAD_DIM], ones_rows], axis=0)

    bi = lax.broadcasted_iota(jnp.int32, (N_BLOCKS_MAX, seq), 0)
    si = lax.broadcasted_iota(jnp.int32, (N_BLOCKS_MAX, seq), 1)
    avg = jnp.where((si >= bi * bs) & (si < (bi + 1) * bs), 1.0 / bs, 0.0).astype(BF16)
    kmean = _dot(avg, k_ref[0])
    km = jnp.concatenate([jnp.broadcast_to(kmean[n:n + 1, :], (N_HEADS, WIDTH))
                          for n in range(N_BLOCKS_MAX)], axis=0)
    row_head = lax.broadcasted_iota(jnp.int32, (LANES, WIDTH), 0) & (N_HEADS - 1)
    lane_head = lax.broadcasted_iota(jnp.int32, (LANES, WIDTH), 1) >> 6
    km_hi, km_lo = _split2(jnp.where(row_head == lane_head, km, 0.0))
    km_ref[0] = km_hi
    km_ref[1] = km_lo

    r128 = lax.broadcasted_iota(jnp.int32, (LANES, 1), 0)
    row_masks = (r128 < HEAD_DIM, r128 >= HEAD_DIM)
    pr_i = lax.broadcasted_iota(jnp.int32, (LANES, LANES), 0)
    pc_i = lax.broadcasted_iota(jnp.int32, (LANES, LANES), 1)
    perm = (pc_i == ((pr_i & (N_BLOCKS_MAX - 1)) << 3) + (pr_i >> 4)).astype(BF16)
    r8 = lax.broadcasted_iota(jnp.int32, (8, bs), 0)
    tl = lax.broadcasted_iota(jnp.int32, (8, bs), 1).astype(F32)
    key_row = lax.broadcasted_iota(jnp.int32, (bs, bs), 0)
    qry_col = lax.broadcasted_iota(jnp.int32, (bs, bs), 1)
    causal = key_row <= qry_col
    zeros_pad = jnp.zeros((LANES - 8 - N_BLOCKS_MAX, bs), F32)
    zeros_cnt = jnp.zeros((N_HEADS, bs), F32)

    def q_block(i, carry):
        start = _aligned(i, bs)
        q_blk = q_ref[0, pl.ds(start, bs), :].astype(F32)
        q_t = [q_blk[:, pr * LANES:(pr + 1) * LANES].T for pr in range(N_PAIRS)]
        q_tb = jnp.concatenate(q_t, axis=0).astype(BF16)
        t0 = jnp.asarray(i * bs, F32)

        gate = _dot(km_ref[0], q_tb) + _dot(km_ref[1], q_tb)
        slabs = [jnp.where(n < i, gate[n * N_HEADS:(n + 1) * N_HEADS], -jnp.inf)
                 for n in range(nb)]
        cnt = [zeros_cnt] * nb
        for n in range(nb):
            for m in range(n):
                m_wins = slabs[m] >= slabs[n]
                cnt[n] = cnt[n] + jnp.where(m_wins, 1.0, 0.0)
                cnt[m] = cnt[m] + jnp.where(m_wins, 0.0, 1.0)
        bias = [jnp.where(jnp.logical_or(cnt[n] < MOBA_TOP_K, n >= i), 0.0, NEG_BIG)
                for n in range(nb)]
        bias += [zeros_cnt] * (N_BLOCKS_MAX - nb)
        bias = _dot(perm, jnp.concatenate(bias, axis=0).astype(BF16))

        q_aug = []
        for h in heads:
            c = slope_ref[h:h + 1, :] * LOG2E
            c_hi = c.astype(BF16).astype(F32)
            ctl = c * tl
            ctl_hi = ctl.astype(BF16).astype(F32)
            ct0 = c * t0
            ct0_hi = ct0.astype(BF16).astype(F32)
            aug8 = jnp.where(r8 == 0, -ctl_hi,
                   jnp.where(r8 == 1, ctl_hi - ctl,
                   jnp.where(r8 == 2, -ct0_hi,
                   jnp.where(r8 == 3, ct0_hi - ct0,
                   jnp.where((r8 == 4) | (r8 == 6), c_hi, c - c_hi)))))
            top = jnp.where(row_masks[h % 2], q_t[h // 2], 0.0) * (scale * LOG2E)
            sel = bias[h * N_BLOCKS_MAX:(h + 1) * N_BLOCKS_MAX]
            q_aug.append(jnp.concatenate([top, aug8, sel, zeros_pad], axis=0).astype(BF16))

        def scores(j):
            kx = [kx_ref[pr, j] for pr in range(N_PAIRS)]
            return [_dot(kx[h // 2], q_aug[h]) for h in heads]

        def weighted_values(j, ps):
            return [_dot(vt_ref[h, j], ps[h]) for h in heads]

        s_own = scores(i)
        m0, p0 = [], []
        for h in heads:
            sh = jnp.where(causal, s_own[h], NEG_BIG)
            m0.append(jnp.max(sh, axis=0, keepdims=True))
            p0.append(jnp.exp2(sh - m0[h]).astype(BF16))
        pv = weighted_values(i, p0)
        acc0 = [pv[h][0:HEAD_DIM] for h in heads]
        l0 = [pv[h][HEAD_DIM:HEAD_DIM + 1] for h in heads]

        def step(js, st):
            m, l, acc = (list(x) for x in st)
            s_all = [scores(j) for j in js]
            ps, alphas = [], []
            for s in s_all:
                p_blk, a_blk = [], []
                for h in heads:
                    mh = jnp.maximum(m[h], jnp.max(s[h], axis=0, keepdims=True))
                    a_blk.append(jnp.exp2(m[h] - mh))
                    p_blk.append(jnp.exp2(s[h] - mh).astype(BF16))
                    m[h] = mh
                ps.append(p_blk)
                alphas.append(a_blk)
            pvs = [weighted_values(j, p_blk) for j, p_blk in zip(js, ps)]
            for a_blk, pv in zip(alphas, pvs):
                for h in heads:
                    acc[h] = a_blk[h] * acc[h] + pv[h][0:HEAD_DIM]
                    l[h] = a_blk[h] * l[h] + pv[h][HEAD_DIM:HEAD_DIM + 1]
            return tuple(tuple(x) for x in (m, l, acc))

        st = (tuple(m0), tuple(l0), tuple(acc0))
        st = lax.cond((i & 1) == 1, lambda st: step([i - 1], st), lambda st: st, st)
        _, l, acc = lax.fori_loop(0, i >> 1, lambda t, st: step([2 * t, 2 * t + 1], st), st)
        pairs = []
        for pr in range(N_PAIRS):
            a, b = 2 * pr, 2 * pr + 1
            out_t = jnp.concatenate([acc[a] / l[a], acc[b] / l[b]], axis=0)
            pairs.append(out_t.T)
        o_ref[0, pl.ds(start, bs), :] = jnp.concatenate(pairs, axis=1).astype(BF16)
        return carry

    lax.fori_loop(0, nb, q_block, 0)


def _moba(qkv, slopes):
    b, s, _ = qkv.shape
    assert s % MOBA_BLOCK == 0 and s // MOBA_BLOCK <= N_BLOCKS_MAX
    nb = s // MOBA_BLOCK
    col = lambda c: pl.BlockSpec((1, s, WIDTH), lambda i: (i, 0, c), pipeline_mode=pl.Buffered(1))
    return pl.pallas_call(
        functools.partial(_moba_kernel, seq=s),
        grid=(b,),
        in_specs=[col(0), col(1), col(2), pl.BlockSpec((N_HEADS, MOBA_BLOCK), lambda i: (0, 0))],
        out_specs=pl.BlockSpec((1, s, WIDTH), lambda i: (i, 0, 0)),
        out_shape=jax.ShapeDtypeStruct((b, s, WIDTH), BF16),
        scratch_shapes=[
            pltpu.VMEM((N_PAIRS, nb, MOBA_BLOCK, 2 * LANES), BF16),
            pltpu.VMEM((N_HEADS, nb, V_ROWS, MOBA_BLOCK), BF16),
            pltpu.VMEM((2, N_BLOCKS_MAX * N_HEADS, WIDTH), BF16),
        ],
        compiler_params=pltpu.CompilerParams(
            dimension_semantics=("arbitrary",), vmem_limit_bytes=VMEM_LIMIT),
        name="moba",
    )(qkv, qkv, qkv, slopes)


FFN_CHUNKS = ((0, 1024), (1024, 1024), (2048, 768))


def _post_kernel(x_ref, ya_ref, yb_ref, gate_ref, wua_ref, wub_ref, wo_ref, gffn_ref,
                 wfi_ref, wfo_ref, gfin_ref, o_ref):
    y_a = _dot(ya_ref[...], wua_ref[...])
    y_b = _dot(yb_ref[...], wub_ref[...])
    ga = jax.nn.sigmoid(gate_ref[:, 0:D_MODEL].astype(F32))
    gb = jax.nn.sigmoid(gate_ref[:, D_MODEL:2 * D_MODEL].astype(F32))
    mixed = (ga * y_a + gb * y_b).astype(BF16)
    x1 = x_ref[...] + _dot(mixed, wo_ref[...])
    h = _rms(x1, gffn_ref[...]).astype(BF16)
    acc = x1
    for off, n in FFN_CHUNKS:
        gg = _dot(h, wfi_ref[:, off:off + n])
        uu = _dot(h, wfi_ref[:, FFN_HIDDEN + off:FFN_HIDDEN + off + n])
        act = (gg * jax.nn.sigmoid(gg) * uu).astype(BF16)
        acc = acc + _dot(act, wfo_ref[off:off + n, :])
    o_ref[...] = _rms(acc, gfin_ref[...])


def _post(x2d, ya, yb, gates, wua, wub, wo, gffn, wfi, wfo, gfin, tm):
    t = x2d.shape[0]
    const = lambda i: (0, 0)
    tile = lambda n: pl.BlockSpec((tm, n), lambda i: (i, 0))
    weight = lambda a: pl.BlockSpec(a.shape, const, pipeline_mode=pl.Buffered(1))
    return pl.pallas_call(
        _post_kernel,
        grid=(t // tm,),
        in_specs=[tile(D_MODEL), tile(WIDTH), tile(WIDTH), tile(GATES),
                  weight(wua), weight(wub), weight(wo), weight(gffn),
                  weight(wfi), weight(wfo), weight(gfin)],
        out_specs=tile(D_MODEL),
        out_shape=jax.ShapeDtypeStruct((t, D_MODEL), F32),
        compiler_params=pltpu.CompilerParams(
            dimension_semantics=("arbitrary",), vmem_limit_bytes=VMEM_LIMIT),
        name="post",
    )(x2d, ya, yb, gates, wua, wub, wo, gffn, wfi, wfo, gfin)


def _layer(x, norm_mix_g, w_in, mu_shift, w0, w_decay_up, a0, w_iclr_up, w_gate_up,
           k_k, k_a, r_k, ln_x_w, ln_x_b, w_up_a, w_up_b, w_o, norm_ffn_g,
           w_ffn_in, w_ffn_out, out_g):
    b, s, _ = x.shape
    t = b * s
    x2d = x.reshape(t, D_MODEL)

    pad = SHIFT_PAD - SHIFT_WIDTH
    w_pad = jnp.concatenate(
        [w_in[:, :SHIFT_WIDTH], jnp.zeros((D_MODEL, pad), F32), w_in[:, SHIFT_WIDTH:]],
        axis=1).astype(BF16)
    mu = jnp.pad(mu_shift, (0, pad)).reshape(1, SHIFT_PAD)
    wda = jnp.zeros((LANES, 2 * WIDTH), F32)
    wda = wda.at[:DECAY_LORA, :WIDTH].set(w_decay_up).at[DECAY_LORA:, WIDTH:].set(w_iclr_up)
    wg = jnp.pad(w_gate_up, ((0, 2 * LANES - GATE_LORA), (0, 0)))
    row = lambda a: a.reshape(1, -1)

    tt = 256
    hid = jnp.arange(WIDTH) // HEAD_DIM
    seg = (hid[:, None] == hid[None, :]).astype(BF16)
    ti = jnp.arange(tt)
    tri = ((ti[:, None] >= ti[None, :]) & (ti[:, None] // CHUNK == ti[None, :] // CHUNK)).astype(BF16)

    sp, qkv, gates = _proj(x2d, row(norm_mix_g), w_pad, tm=256)
    ya = _rwkv(sp.reshape(b, s, SHIFT_PAD), mu, row(w0), wda.astype(BF16), row(a0),
               wg.astype(BF16), row(k_k), row(k_a), row(r_k), row(ln_x_w), row(ln_x_b),
               seg, tri, tt)
    slopes = jnp.exp2(-8.0 * jnp.arange(1, N_HEADS + 1, dtype=F32) / N_HEADS)
    slopes = jnp.broadcast_to(slopes[:, None], (N_HEADS, MOBA_BLOCK))
    yb = _moba(qkv.reshape(b, s, QKV_B), slopes)
    out = _post(x2d, ya.reshape(t, WIDTH), yb.reshape(t, WIDTH), gates,
                w_up_a.astype(BF16), w_up_b.astype(BF16), w_o.astype(BF16), row(norm_ffn_g),
                w_ffn_in.astype(BF16), w_ffn_out.astype(BF16), row(out_g), tm=256)
    return out.reshape(b, s, D_MODEL)


def kernel(x, norm_mix_g, w_in, mu_shift, w0, w_decay_up, a0, w_iclr_up, w_gate_up, k_k, k_a, r_k, ln_x_w, ln_x_b, w_up_a, w_up_b, w_o, norm_ffn_g, w_ffn_in, w_ffn_out, norm_final_g):
    depth = w_in.shape[0]
    assert depth == 1, "the fused post kernel applies the final norm after the only layer"
    return _layer(x, norm_mix_g[0], w_in[0], mu_shift[0], w0[0], w_decay_up[0], a0[0],
                  w_iclr_up[0], w_gate_up[0], k_k[0], k_a[0], r_k[0].reshape(-1), ln_x_w[0],
                  ln_x_b[0], w_up_a[0], w_up_b[0], w_o[0], norm_ffn_g[0], w_ffn_in[0],
                  w_ffn_out[0], norm_final_g)
```

```python
import functools

import jax
import jax.numpy as jnp
from jax import lax
from jax.experimental import pallas as pl
from jax.experimental.pallas import tpu as pltpu

F32 = jnp.float32
BF16 = jnp.bfloat16

D_MODEL = 1024
HEAD_DIM = 64
N_HEADS = 8
WIDTH = N_HEADS * HEAD_DIM
N_PAIRS = N_HEADS // 2
LANES = 128
DECAY_LORA = 64
ICLR_LORA = 64
GATE_LORA = 160
GN_EPS = 64e-5
EXP_NEG_HALF = 0.6065306597126334
RMS_EPS = 1e-6
MOBA_BLOCK = 256
MOBA_TOP_K = 3
FFN_HIDDEN = 2816
SHIFT_WIDTH = 3 * WIDTH + DECAY_LORA + ICLR_LORA + GATE_LORA
SHIFT_PAD = 1920
QKV_B = 3 * WIDTH
GATES = 2 * D_MODEL
PROJ_PAD = SHIFT_PAD + QKV_B + GATES

CHUNK = 64
VMEM_LIMIT = 56 * 1024 * 1024


def _dot(a, b):
    return jnp.dot(a, b, preferred_element_type=F32)


def _dot_nt(a, b):
    return lax.dot_general(a, b, (((1,), (1,)), ((), ())), preferred_element_type=F32)


def _rms(x, g):
    return x * lax.rsqrt(jnp.mean(x * x, axis=-1, keepdims=True) + RMS_EPS) * g


def _proj_kernel(x_ref, g_ref, w_ref, mu_ref, sp_ref, qkv_ref, gate_ref, carry_ref, *,
                 tiles_per_seq):
    tm = x_ref.shape[0]
    step = pl.program_id(0)

    @pl.when(step == 0)
    def _():
        carry_ref[...] = jnp.zeros_like(carry_ref)

    h = _rms(x_ref[...], g_ref[...]).astype(BF16)
    qkv_ref[...] = _dot(h, w_ref[:, SHIFT_PAD:SHIFT_PAD + QKV_B]).astype(BF16)
    gate_ref[...] = _dot(h, w_ref[:, SHIFT_PAD + QKV_B:PROJ_PAD]).astype(BF16)

    p = _dot(h, w_ref[:, 0:SHIFT_PAD])
    seq_start = lax.rem(step, tiles_per_seq) == 0
    carry = jnp.where(seq_start, 0.0, carry_ref[0:1, :])
    row = lax.broadcasted_iota(jnp.int32, (tm, 1), 0)
    prev = jnp.where(row == 0, carry, pltpu.roll(p, 1, 0))
    carry_ref[0:1, :] = p[tm - 1:tm, :]
    sp_ref[...] = p + mu_ref[...] * (prev - p)


def _proj(x2d, g, w, mu, tm, seq):
    t = x2d.shape[0]
    const = lambda i: (0, 0)
    return pl.pallas_call(
        functools.partial(_proj_kernel, tiles_per_seq=seq // tm),
        grid=(t // tm,),
        in_specs=[
            pl.BlockSpec((tm, D_MODEL), lambda i: (i, 0)),
            pl.BlockSpec((1, D_MODEL), const),
            pl.BlockSpec((D_MODEL, PROJ_PAD), const, pipeline_mode=pl.Buffered(1)),
            pl.BlockSpec((1, SHIFT_PAD), const),
        ],
        scratch_shapes=[pltpu.VMEM((8, SHIFT_PAD), F32)],
        out_specs=[
            pl.BlockSpec((tm, SHIFT_PAD), lambda i: (i, 0)),
            pl.BlockSpec((tm, QKV_B), lambda i: (i, 0)),
            pl.BlockSpec((tm, GATES), lambda i: (i, 0)),
        ],
        out_shape=[
            jax.ShapeDtypeStruct((t, SHIFT_PAD), F32),
            jax.ShapeDtypeStruct((t, QKV_B), BF16),
            jax.ShapeDtypeStruct((t, GATES), BF16),
        ],
        compiler_params=pltpu.CompilerParams(
            dimension_semantics=("arbitrary",), vmem_limit_bytes=VMEM_LIMIT),
        name="proj",
    )(x2d, g, w, mu)


def _split2(x):
    hi = x.astype(BF16)
    lo = (x - hi.astype(F32)).astype(BF16)
    return hi, lo


SEG_LANES = 2 * LANES


def _segsum(x, seg):
    xb = x.astype(BF16)
    return jnp.concatenate(
        [_dot(xb[:, o:o + SEG_LANES], seg) for o in range(0, x.shape[1], SEG_LANES)], axis=1)


def _rwkv_kernel(sp_ref, w0_ref, wda_ref, a0_ref, wg_ref, kk_ref, ka_ref, rk_ref,
                 lnw_ref, lnb_ref, seg_ref, tri_ref, o_ref, h_ref, *, tt):
    n_chunks = tt // CHUNK

    @pl.when(pl.program_id(1) == 0)
    def _():
        h_ref[...] = jnp.zeros_like(h_ref)

    xs = sp_ref[0]
    r = xs[:, 0:WIDTH]
    k = xs[:, WIDTH:2 * WIDTH]
    v = xs[:, 2 * WIDTH:3 * WIDTH]
    da = xs[:, 3 * WIDTH:3 * WIDTH + LANES]
    gd = xs[:, 3 * WIDTH + LANES:SHIFT_PAD]
    lane = lax.broadcasted_iota(jnp.int32, (1, LANES), 1)
    lo_half = lane < HEAD_DIM
    da_act = jnp.where(lo_half, jnp.tanh(da), da).astype(BF16)
    lora = _dot(da_act, wda_ref[...])
    g = _dot(jax.nn.sigmoid(gd).astype(BF16), wg_ref[...])

    logdec = -EXP_NEG_HALF * jax.nn.sigmoid(w0_ref[...] + lora[:, 0:WIDTH])
    a = jax.nn.sigmoid(a0_ref[...] + lora[:, WIDTH:2 * WIDTH])

    seg = seg_ref[...]
    kkf = k * kk_ref[...]
    kk = kkf * lax.rsqrt(jnp.maximum(_segsum(kkf * kkf, seg), 1e-24))
    k2 = k * (1.0 + (a - 1.0) * ka_ref[...])
    bonus = _segsum(r * k2 * rk_ref[...], seg) * v

    l1 = logdec.astype(BF16)
    r1 = logdec - l1.astype(F32)
    l2 = r1.astype(BF16)
    l3 = (r1 - l2.astype(F32)).astype(BF16)
    tri = tri_ref[...]
    cum = _dot(tri, l1) + _dot(tri, l2) + _dot(tri, l3)

    kka = kk * a
    p_inv = jnp.exp(-cum)
    a_t = -kk * jnp.exp(cum - logdec)
    r_t = r * jnp.exp(cum)
    b_t = kka * p_inv
    k_t = k2 * p_inv

    r64 = lax.broadcasted_iota(jnp.int32, (CHUNK, LANES), 0)
    c64 = lax.broadcasted_iota(jnp.int32, (CHUNK, LANES), 1) & (CHUNK - 1)
    strict = c64 < r64
    incl = c64 <= r64
    eye_side = (c64 == r64).astype(F32)
    r128 = lax.broadcasted_iota(jnp.int32, (LANES, LANES), 0)
    c128 = lax.broadcasted_iota(jnp.int32, (LANES, LANES), 1)
    eye128 = (r128 == c128).astype(F32)
    same_head = (r128 >> 6) == (c128 >> 6)
    hi_half = jnp.logical_not(lo_half)
    zeros_c = jnp.zeros((CHUNK, LANES), F32)

    def _stack2(zz):
        zb = zz.astype(BF16)
        return jnp.concatenate([jnp.where(lo_half, zb, 0), jnp.where(hi_half, zb, 0)], axis=0)

    chains = [(c, pr) for c in range(n_chunks) for pr in range(N_PAIRS)]

    def _piece(arr, c, pr):
        return arr[c * CHUNK:(c + 1) * CHUNK, pr * LANES:(pr + 1) * LANES]

    x_ab, x_ak, x_rb, x_rk = {}, {}, {}, {}
    for ch in chains:
        lq = jnp.concatenate([_piece(a_t, *ch), _piece(r_t, *ch)], axis=0).astype(BF16)
        rhs = jnp.concatenate([_stack2(_piece(b_t, *ch)), _stack2(_piece(k_t, *ch))], axis=0)
        sc = _dot_nt(lq, rhs)
        x_ab[ch] = jnp.where(strict, sc[0:CHUNK, 0:LANES], 0.0)
        x_ak[ch] = jnp.where(strict, sc[0:CHUNK, LANES:2 * LANES], 0.0)
        x_rb[ch] = jnp.where(incl, sc[CHUNK:2 * CHUNK, 0:LANES], 0.0)
        x_rk[ch] = jnp.where(incl, sc[CHUNK:2 * CHUNK, LANES:2 * LANES], 0.0)

    x_t = {ch: eye_side + x_ab[ch] for ch in chains}
    x_pw = {ch: _dot(x_ab[ch].astype(BF16), _stack2(x_ab[ch])) for ch in chains}
    av = {ch: _dot(x_ak[ch].astype(BF16), _stack2(_piece(v, *ch))) for ch in chains}
    for _ in range(4):
        for ch in chains:
            both = jnp.concatenate([x_t[ch], x_pw[ch]], axis=0).astype(BF16)
            out = _dot(both, _stack2(x_pw[ch]))
            x_t[ch] = x_t[ch] + out[0:CHUNK]
            x_pw[ch] = out[CHUNK:2 * CHUNK]
    for ch in chains:
        x_t[ch] = x_t[ch] + _dot(x_t[ch].astype(BF16), _stack2(x_pw[ch]))

    tw = {}
    for ch in chains:
        rhs = jnp.concatenate([_stack2(_piece(a_t, *ch)), _stack2(av[ch])], axis=1)
        tw[ch] = _dot(x_t[ch].astype(BF16), rhs)
    qeff, yloc, phi, psi = {}, {}, {}, {}
    for ch in chains:
        ta, w_loc, v_p = tw[ch][:, 0:LANES], tw[ch][:, LANES:2 * LANES], _piece(v, *ch)
        rhs = jnp.concatenate(
            [jnp.concatenate([_stack2(ta), _stack2(w_loc)], axis=1),
             jnp.concatenate([jnp.zeros((LANES, LANES), BF16), _stack2(v_p)], axis=1)], axis=0)
        ag = _dot(jnp.concatenate([x_rb[ch], x_rk[ch]], axis=1).astype(BF16), rhs)
        qeff[ch] = _piece(r_t, *ch) + ag[:, 0:LANES]
        yloc[ch] = ag[:, LANES:2 * LANES]
        cum_c = _piece(cum, *ch)
        cum_last = cum_c[CHUNK - 1:CHUNK, :]
        to_end = jnp.exp(cum_last - cum_c)
        kk_t = jnp.concatenate([_piece(kka, *ch) * to_end, _piece(k2, *ch) * to_end], axis=0).T
        rhs = jnp.concatenate(
            [tw[ch], jnp.concatenate([zeros_c, v_p], axis=1)], axis=0).astype(BF16)
        pp = _dot(kk_t.astype(BF16), rhs)
        phi[ch] = eye128 * jnp.exp(cum_last) + jnp.where(same_head, pp[:, 0:LANES], 0.0)
        psi[ch] = jnp.where(same_head, pp[:, LANES:2 * LANES], 0.0)

    y_rows = []
    states = [h_ref[pr] for pr in range(N_PAIRS)]
    for c in range(n_chunks):
        y_c = []
        for pr in range(N_PAIRS):
            ch = (c, pr)
            lhs = jnp.concatenate([qeff[ch], phi[ch]], axis=0).astype(BF16)
            out = _dot(lhs, states[pr].astype(BF16))
            y_c.append(out[0:CHUNK] + yloc[ch])
            states[pr] = out[CHUNK:CHUNK + LANES] + psi[ch]
        y_rows.append(jnp.concatenate(y_c, axis=1))
    for pr in range(N_PAIRS):
        h_ref[pr] = states[pr]
    y = jnp.concatenate(y_rows, axis=0)

    mean = _segsum(y, seg) * (1.0 / HEAD_DIM)
    d = y - mean
    var = _segsum(d * d, seg) * (1.0 / HEAD_DIM)
    yn = d * lax.rsqrt(var + GN_EPS) * lnw_ref[...] + lnb_ref[...]
    o_ref[0] = ((yn + bonus) * g).astype(BF16)


def _rwkv(sp, w0, wda, a0, wg, k_k, k_a, r_k, ln_w, ln_b, seg, tri, tt):
    b, s, _ = sp.shape
    const2 = lambda i, j: (0, 0)
    row = lambda n: pl.BlockSpec((1, n), const2)
    return pl.pallas_call(
        functools.partial(_rwkv_kernel, tt=tt),
        grid=(b, s // tt),
        in_specs=[
            pl.BlockSpec((1, tt, SHIFT_PAD), lambda i, j: (i, j, 0)),
            row(WIDTH),
            pl.BlockSpec((LANES, 2 * WIDTH), const2),
            row(WIDTH),
            pl.BlockSpec((2 * LANES, WIDTH), const2),
            row(WIDTH), row(WIDTH), row(WIDTH), row(WIDTH), row(WIDTH),
            pl.BlockSpec((SEG_LANES, SEG_LANES), const2),
            pl.BlockSpec((tt, tt), const2),
        ],
        out_specs=pl.BlockSpec((1, tt, WIDTH), lambda i, j: (i, j, 0)),
        out_shape=jax.ShapeDtypeStruct((b, s, WIDTH), BF16),
        scratch_shapes=[
            pltpu.VMEM((N_PAIRS, LANES, LANES), F32),
        ],
        compiler_params=pltpu.CompilerParams(
            dimension_semantics=("arbitrary", "arbitrary"), vmem_limit_bytes=VMEM_LIMIT),
        name="rwkv",
    )(sp, w0, wda, a0, wg, k_k, k_a, r_k, ln_w, ln_b, seg, tri)


NEG_BIG = -(2.0 ** 30)
N_BLOCKS_MAX = 16
V_ROWS = HEAD_DIM + 16
LOG2E = 1.4426950408889634


def _aligned(i, m):
    return i * m if isinstance(i, int) else pl.multiple_of(i * m, m)


def _moba_kernel(q_ref, k_ref, v_ref, slope_ref, o_ref, kx_ref, vt_ref, km_ref, *, seq):
    nb = seq // MOBA_BLOCK
    bs = MOBA_BLOCK
    scale = HEAD_DIM ** -0.5
    heads = range(N_HEADS)

    rr = lax.broadcasted_iota(jnp.int32, (bs, LANES), 0)
    ll = lax.broadcasted_iota(jnp.int32, (bs, LANES), 1)
    ones_rows = jnp.ones((V_ROWS - HEAD_DIM, bs), BF16)
    for n in range(nb):
        aug = jnp.where(ll < 4, 1.0,
              jnp.where(ll < 6, rr.astype(F32),
              jnp.where(ll < 8, float(n * bs),
              jnp.where(ll == 8 + n, 1.0, 0.0)))).astype(BF16)
        rows = slice(n * bs, (n + 1) * bs)
        for pr in range(N_PAIRS):
            ln = slice(pr * LANES, (pr + 1) * LANES)
            kx_ref[pr, n, :, 0:LANES] = k_ref[0, rows, ln]
            kx_ref[pr, n, :, LANES:2 * LANES] = aug
            v_t = v_ref[0, rows, ln].astype(F32).T.astype(BF16)
            for half in range(2):
                vt_ref[2 * pr + half, n] = jnp.concatenate(
                    [v_t[half * HEAD_DIM:(half + 1) * HEAD_DIM], ones_rows], axis=0)

    bi = lax.broadcasted_iota(jnp.int32, (N_BLOCKS_MAX, seq), 0)
    si = lax.broadcasted_iota(jnp.int32, (N_BLOCKS_MAX, seq), 1)
    avg = jnp.where((si >= bi * bs) & (si < (bi + 1) * bs), 1.0 / bs, 0.0).astype(BF16)
    kmean = _dot(avg, k_ref[0])
    km = jnp.concatenate([jnp.broadcast_to(kmean[n:n + 1, :], (N_HEADS, WIDTH))
                          for n in range(N_BLOCKS_MAX)], axis=0)
    row_head = lax.broadcasted_iota(jnp.int32, (LANES, WIDTH), 0) & (N_HEADS - 1)
    lane_head = lax.broadcasted_iota(jnp.int32, (LANES, WIDTH), 1) >> 6
    km_hi, km_lo = _split2(jnp.where(row_head == lane_head, km, 0.0))
    km_ref[0] = km_hi
    km_ref[1] = km_lo

    r128 = lax.broadcasted_iota(jnp.int32, (LANES, 1), 0)
    row_masks = (r128 < HEAD_DIM, r128 >= HEAD_DIM)
    pr_i = lax.broadcasted_iota(jnp.int32, (LANES, LANES), 0)
    pc_i = lax.broadcasted_iota(jnp.int32, (LANES, LANES), 1)
    perm = (pc_i == ((pr_i & (N_BLOCKS_MAX - 1)) << 3) + (pr_i >> 4)).astype(BF16)
    r8 = lax.broadcasted_iota(jnp.int32, (8, bs), 0)
    tl = lax.broadcasted_iota(jnp.int32, (8, bs), 1).astype(F32)
    key_row = lax.broadcasted_iota(jnp.int32, (bs, bs), 0)
    qry_col = lax.broadcasted_iota(jnp.int32, (bs, bs), 1)
    causal = key_row <= qry_col
    zeros_pad = jnp.zeros((LANES - 8 - N_BLOCKS_MAX, bs), F32)
    zeros_cnt = jnp.zeros((N_HEADS, bs), F32)

    def q_block(i, carry):
        start = _aligned(i, bs)
        q_blk = q_ref[0, pl.ds(start, bs), :].astype(F32)
        q_t = [q_blk[:, pr * LANES:(pr + 1) * LANES].T for pr in range(N_PAIRS)]
        q_tb = jnp.concatenate(q_t, axis=0).astype(BF16)
        t0 = jnp.asarray(i * bs, F32)

        gate = _dot(km_ref[0], q_tb) + _dot(km_ref[1], q_tb)
        slabs = [jnp.where(n < i, gate[n * N_HEADS:(n + 1) * N_HEADS], -jnp.inf)
                 for n in range(nb)]
        cnt = [zeros_cnt] * nb
        for n in range(nb):
            for m in range(n):
                m_wins = slabs[m] >= slabs[n]
                cnt[n] = cnt[n] + jnp.where(m_wins, 1.0, 0.0)
                cnt[m] = cnt[m] + jnp.where(m_wins, 0.0, 1.0)
        bias = [jnp.where(jnp.logical_or(cnt[n] < MOBA_TOP_K, n >= i), 0.0, NEG_BIG)
                for n in range(nb)]
        bias += [zeros_cnt] * (N_BLOCKS_MAX - nb)
        bias = _dot(perm, jnp.concatenate(bias, axis=0).astype(BF16))

        q_aug = []
        for h in heads:
            c = slope_ref[h:h + 1, :] * LOG2E
            c_hi = c.astype(BF16).astype(F32)
            ctl = c * tl
            ctl_hi = ctl.astype(BF16).astype(F32)
            ct0 = c * t0
            ct0_hi = ct0.astype(BF16).astype(F32)
            aug8 = jnp.where(r8 == 0, -ctl_hi,
                   jnp.where(r8 == 1, ctl_hi - ctl,
                   jnp.where(r8 == 2, -ct0_hi,
                   jnp.where(r8 == 3, ct0_hi - ct0,
                   jnp.where((r8 == 4) | (r8 == 6), c_hi, c - c_hi)))))
            top = jnp.where(row_masks[h % 2], q_t[h // 2], 0.0) * (scale * LOG2E)
            sel = bias[h * N_BLOCKS_MAX:(h + 1) * N_BLOCKS_MAX]
            q_aug.append(jnp.concatenate([top, aug8, sel, zeros_pad], axis=0).astype(BF16))

        def scores(j):
            kx = [kx_ref[pr, j] for pr in range(N_PAIRS)]
            return [_dot(kx[h // 2], q_aug[h]) for h in heads]

        def weighted_values(j, ps):
            return [_dot(vt_ref[h, j], ps[h]) for h in heads]

        s_own = scores(i)
        m0, p0 = [], []
        for h in heads:
            sh = jnp.where(causal, s_own[h], NEG_BIG)
            m0.append(jnp.max(sh, axis=0, keepdims=True))
            p0.append(jnp.exp2(sh - m0[h]).astype(BF16))
        pv = weighted_values(i, p0)
        acc0 = [pv[h][0:HEAD_DIM] for h in heads]
        l0 = [pv[h][HEAD_DIM:HEAD_DIM + 1] for h in heads]

        def step(js, st):
            m, l, acc = (list(x) for x in st)
            s_all = [scores(j) for j in js]
            ps, alphas = [], []
            for s in s_all:
                p_blk, a_blk = [], []
                for h in heads:
                    mh = jnp.maximum(m[h], jnp.max(s[h], axis=0, keepdims=True))
                    a_blk.append(jnp.exp2(m[h] - mh))
                    p_blk.append(jnp.exp2(s[h] - mh).astype(BF16))
                    m[h] = mh
                ps.append(p_blk)
                alphas.append(a_blk)
            pvs = [weighted_values(j, p_blk) for j, p_blk in zip(js, ps)]
            for a_blk, pv in zip(alphas, pvs):
                for h in heads:
                    acc[h] = a_blk[h] * acc[h] + pv[h][0:HEAD_DIM]
                    l[h] = a_blk[h] * l[h] + pv[h][HEAD_DIM:HEAD_DIM + 1]
            return tuple(tuple(x) for x in (m, l, acc))

        st = (tuple(m0), tuple(l0), tuple(acc0))
        st = lax.cond((i & 1) == 1, lambda st: step([i - 1], st), lambda st: st, st)
        _, l, acc = lax.fori_loop(0, i >> 1, lambda t, st: step([2 * t, 2 * t + 1], st), st)
        pairs = []
        for pr in range(N_PAIRS):
            a, b = 2 * pr, 2 * pr + 1
            out_t = jnp.concatenate([acc[a] / l[a], acc[b] / l[b]], axis=0)
            pairs.append(out_t.T)
        o_ref[0, pl.ds(start, bs), :] = jnp.concatenate(pairs, axis=1).astype(BF16)
        return carry

    lax.fori_loop(0, nb, q_block, 0)


def _moba(qkv, slopes):
    b, s, _ = qkv.shape
    assert s % MOBA_BLOCK == 0 and s // MOBA_BLOCK <= N_BLOCKS_MAX
    nb = s // MOBA_BLOCK
    col = lambda c: pl.BlockSpec((1, s, WIDTH), lambda i: (i, 0, c), pipeline_mode=pl.Buffered(1))
    return pl.pallas_call(
        functools.partial(_moba_kernel, seq=s),
        grid=(b,),
        in_specs=[col(0), col(1), col(2), pl.BlockSpec((N_HEADS, MOBA_BLOCK), lambda i: (0, 0))],
        out_specs=pl.BlockSpec((1, s, WIDTH), lambda i: (i, 0, 0)),
        out_shape=jax.ShapeDtypeStruct((b, s, WIDTH), BF16),
        scratch_shapes=[
            pltpu.VMEM((N_PAIRS, nb, MOBA_BLOCK, 2 * LANES), BF16),
            pltpu.VMEM((N_HEADS, nb, V_ROWS, MOBA_BLOCK), BF16),
            pltpu.VMEM((2, N_BLOCKS_MAX * N_HEADS, WIDTH), BF16),
        ],
        compiler_params=pltpu.CompilerParams(
            dimension_semantics=("arbitrary",), vmem_limit_bytes=VMEM_LIMIT),
        name="moba",
    )(qkv, qkv, qkv, slopes)


FFN_CHUNKS = ((0, 1024), (1024, 1024), (2048, 768))


def _post_kernel(x_ref, ya_ref, yb_ref, gate_ref, wua_ref, wub_ref, wo_ref, gffn_ref,
                 wfi_ref, wfo_ref, gfin_ref, o_ref):
    y_a = _dot(ya_ref[...], wua_ref[...])
    y_b = _dot(yb_ref[...], wub_ref[...])
    ga = jax.nn.sigmoid(gate_ref[:, 0:D_MODEL].astype(F32))
    gb = jax.nn.sigmoid(gate_ref[:, D_MODEL:2 * D_MODEL].astype(F32))
    mixed = (ga * y_a + gb * y_b).astype(BF16)
    x1 = x_ref[...] + _dot(mixed, wo_ref[...])
    h = _rms(x1, gffn_ref[...]).astype(BF16)
    acc = x1
    for off, n in FFN_CHUNKS:
        gg = _dot(h, wfi_ref[:, off:off + n])
        uu = _dot(h, wfi_ref[:, FFN_HIDDEN + off:FFN_HIDDEN + off + n])
        act = (gg * jax.nn.sigmoid(gg) * uu).astype(BF16)
        acc = acc + _dot(act, wfo_ref[off:off + n, :])
    o_ref[...] = _rms(acc, gfin_ref[...])


def _post(x2d, ya, yb, gates, wua, wub, wo, gffn, wfi, wfo, gfin, tm):
    t = x2d.shape[0]
    const = lambda i: (0, 0)
    tile = lambda n: pl.BlockSpec((tm, n), lambda i: (i, 0))
    weight = lambda a: pl.BlockSpec(a.shape, const, pipeline_mode=pl.Buffered(1))
    return pl.pallas_call(
        _post_kernel,
        grid=(t // tm,),
        in_specs=[tile(D_MODEL), tile(WIDTH), tile(WIDTH), tile(GATES),
                  weight(wua), weight(wub), weight(wo), weight(gffn),
                  weight(wfi), weight(wfo), weight(gfin)],
        out_specs=tile(D_MODEL),
        out_shape=jax.ShapeDtypeStruct((t, D_MODEL), F32),
        compiler_params=pltpu.CompilerParams(
            dimension_semantics=("arbitrary",), vmem_limit_bytes=VMEM_LIMIT),
        name="post",
    )(x2d, ya, yb, gates, wua, wub, wo, gffn, wfi, wfo, gfin)


def _layer(x, norm_mix_g, w_in, mu_shift, w0, w_decay_up, a0, w_iclr_up, w_gate_up,
           k_k, k_a, r_k, ln_x_w, ln_x_b, w_up_a, w_up_b, w_o, norm_ffn_g,
           w_ffn_in, w_ffn_out, out_g):
    b, s, _ = x.shape
    t = b * s
    x2d = x.reshape(t, D_MODEL)

    pad = SHIFT_PAD - SHIFT_WIDTH
    w_b = w_in.astype(BF16)
    w_pad = jnp.concatenate(
        [w_b[:, :SHIFT_WIDTH], jnp.zeros((D_MODEL, pad), BF16), w_b[:, SHIFT_WIDTH:]], axis=1)
    mu = jnp.pad(mu_shift, (0, pad)).reshape(1, SHIFT_PAD)
    wda = jnp.zeros((LANES, 2 * WIDTH), F32)
    wda = wda.at[:DECAY_LORA, :WIDTH].set(w_decay_up).at[DECAY_LORA:, WIDTH:].set(w_iclr_up)
    wg = jnp.pad(w_gate_up, ((0, 2 * LANES - GATE_LORA), (0, 0)))
    row = lambda a: a.reshape(1, -1)

    tt = 256
    hid = jnp.arange(SEG_LANES) // HEAD_DIM
    seg = (hid[:, None] == hid[None, :]).astype(BF16)
    ti = jnp.arange(tt)
    tri = ((ti[:, None] >= ti[None, :]) & (ti[:, None] // CHUNK == ti[None, :] // CHUNK)).astype(BF16)

    sp, qkv, gates = _proj(x2d, row(norm_mix_g), w_pad, mu, tm=256, seq=s)
    ya = _rwkv(sp.reshape(b, s, SHIFT_PAD), row(w0), wda.astype(BF16), row(a0),
               wg.astype(BF16), row(k_k), row(k_a), row(r_k), row(ln_x_w), row(ln_x_b),
               seg, tri, tt)
    slopes = jnp.exp2(-8.0 * jnp.arange(1, N_HEADS + 1, dtype=F32) / N_HEADS)
    slopes = jnp.broadcast_to(slopes[:, None], (N_HEADS, MOBA_BLOCK))
    yb = _moba(qkv.reshape(b, s, QKV_B), slopes)
    out = _post(x2d, ya.reshape(t, WIDTH), yb.reshape(t, WIDTH), gates,
                w_up_a.astype(BF16), w_up_b.astype(BF16), w_o.astype(BF16), row(norm_ffn_g),
                w_ffn_in.astype(BF16), w_ffn_out.astype(BF16), row(out_g), tm=256)
    return out.reshape(b, s, D_MODEL)


def kernel(x, norm_mix_g, w_in, mu_shift, w0, w_decay_up, a0, w_iclr_up, w_gate_up, k_k, k_a, r_k, ln_x_w, ln_x_b, w_up_a, w_up_b, w_o, norm_ffn_g, w_ffn_in, w_ffn_out, norm_final_g):
    depth = w_in.shape[0]
    assert depth == 1, "the fused post kernel applies the final norm after the only layer"
    return _layer(x, norm_mix_g[0], w_in[0], mu_shift[0], w0[0], w_decay_up[0], a0[0],
                  w_iclr_up[0], w_gate_up[0], k_k[0], k_a[0], r_k[0].reshape(-1), ln_x_w[0],
                  ln_x_b[0], w_up_a[0], w_up_b[0], w_o[0], norm_ffn_g[0], w_ffn_in[0],
                  w_ffn_out[0], norm_final_g)
```

```python
import functools

import jax
import jax.numpy as jnp
from jax import lax
from jax.experimental import pallas as pl
from jax.experimental.pallas import tpu as pltpu

F32 = jnp.float32
BF16 = jnp.bfloat16

D_MODEL = 1024
HEAD_DIM = 64
N_HEADS = 8
WIDTH = N_HEADS * HEAD_DIM
N_PAIRS = N_HEADS // 2
LANES = 128
DECAY_LORA = 64
ICLR_LORA = 64
GATE_LORA = 160
GN_EPS = 64e-5
EXP_NEG_HALF = 0.6065306597126334
RMS_EPS = 1e-6
MOBA_BLOCK = 256
MOBA_TOP_K = 3
FFN_HIDDEN = 2816
SHIFT_WIDTH = 3 * WIDTH + DECAY_LORA + ICLR_LORA + GATE_LORA
SHIFT_PAD = 1920
QKV_B = 3 * WIDTH
GATES = 2 * D_MODEL
PROJ_PAD = SHIFT_PAD + QKV_B + GATES

CHUNK = 64
VMEM_LIMIT = 56 * 1024 * 1024


def _dot(a, b):
    return jnp.dot(a, b, preferred_element_type=F32)


def _dot_nt(a, b):
    return lax.dot_general(a, b, (((1,), (1,)), ((), ())), preferred_element_type=F32)


def _rms(x, g):
    return x * lax.rsqrt(jnp.mean(x * x, axis=-1, keepdims=True) + RMS_EPS) * g


def _proj_kernel(x_ref, g_ref, wsp_ref, wqkv_ref, wgate_ref, mu_ref, sp_ref, qkv_ref, gate_ref,
                 carry_ref, *, tiles_per_seq):
    tm = x_ref.shape[0]
    step = pl.program_id(0)

    @pl.when(step == 0)
    def _():
        carry_ref[...] = jnp.zeros_like(carry_ref)

    h = _rms(x_ref[...], g_ref[...]).astype(BF16)
    qkv_ref[...] = _dot(h, wqkv_ref[...]).astype(BF16)
    gate_ref[...] = _dot(h, wgate_ref[...]).astype(BF16)

    p = _dot(h, wsp_ref[...])
    seq_start = lax.rem(step, tiles_per_seq) == 0
    carry = jnp.where(seq_start, 0.0, carry_ref[0:1, :])
    row = lax.broadcasted_iota(jnp.int32, (tm, 1), 0)
    prev = jnp.where(row == 0, carry, pltpu.roll(p, 1, 0))
    carry_ref[0:1, :] = p[tm - 1:tm, :]
    sp_ref[...] = p + mu_ref[...] * (prev - p)


def _proj(x2d, g, w_sp, w_qkv, w_gate, mu, tm, seq):
    t = x2d.shape[0]
    const = lambda i: (0, 0)
    weight = lambda n: pl.BlockSpec((D_MODEL, n), const, pipeline_mode=pl.Buffered(1))
    return pl.pallas_call(
        functools.partial(_proj_kernel, tiles_per_seq=seq // tm),
        grid=(t // tm,),
        in_specs=[
            pl.BlockSpec((tm, D_MODEL), lambda i: (i, 0)),
            pl.BlockSpec((1, D_MODEL), const),
            weight(SHIFT_PAD), weight(QKV_B), weight(GATES),
            pl.BlockSpec((1, SHIFT_PAD), const),
        ],
        scratch_shapes=[pltpu.VMEM((8, SHIFT_PAD), F32)],
        out_specs=[
            pl.BlockSpec((tm, SHIFT_PAD), lambda i: (i, 0)),
            pl.BlockSpec((tm, QKV_B), lambda i: (i, 0)),
            pl.BlockSpec((tm, GATES), lambda i: (i, 0)),
        ],
        out_shape=[
            jax.ShapeDtypeStruct((t, SHIFT_PAD), F32),
            jax.ShapeDtypeStruct((t, QKV_B), BF16),
            jax.ShapeDtypeStruct((t, GATES), BF16),
        ],
        compiler_params=pltpu.CompilerParams(
            dimension_semantics=("arbitrary",), vmem_limit_bytes=VMEM_LIMIT),
        name="proj",
    )(x2d, g, w_sp, w_qkv, w_gate, mu)


def _split2(x):
    hi = x.astype(BF16)
    lo = (x - hi.astype(F32)).astype(BF16)
    return hi, lo


SEG_LANES = 2 * LANES


def _segsum(x, seg):
    xb = x.astype(BF16)
    return jnp.concatenate(
        [_dot(xb[:, o:o + SEG_LANES], seg) for o in range(0, x.shape[1], SEG_LANES)], axis=1)


def _rwkv_kernel(sp_ref, w0_ref, wda_ref, a0_ref, wg_ref, kk_ref, ka_ref, rk_ref,
                 lnw_ref, lnb_ref, seg_ref, tri_ref, o_ref, h_ref, *, tt):
    n_chunks = tt // CHUNK

    @pl.when(pl.program_id(1) == 0)
    def _():
        h_ref[...] = jnp.zeros_like(h_ref)

    xs = sp_ref[0]
    r = xs[:, 0:WIDTH]
    k = xs[:, WIDTH:2 * WIDTH]
    v = xs[:, 2 * WIDTH:3 * WIDTH]
    da = xs[:, 3 * WIDTH:3 * WIDTH + LANES]
    gd = xs[:, 3 * WIDTH + LANES:SHIFT_PAD]
    lane = lax.broadcasted_iota(jnp.int32, (1, LANES), 1)
    lo_half = lane < HEAD_DIM
    da_act = jnp.where(lo_half, jnp.tanh(da), da).astype(BF16)
    lora = _dot(da_act, wda_ref[...])
    g = _dot(jax.nn.sigmoid(gd).astype(BF16), wg_ref[...])

    logdec = -EXP_NEG_HALF * jax.nn.sigmoid(w0_ref[...] + lora[:, 0:WIDTH])
    a = jax.nn.sigmoid(a0_ref[...] + lora[:, WIDTH:2 * WIDTH])

    seg = seg_ref[...]
    kkf = k * kk_ref[...]
    kk = kkf * lax.rsqrt(jnp.maximum(_segsum(kkf * kkf, seg), 1e-24))
    k2 = k * (1.0 + (a - 1.0) * ka_ref[...])
    bonus = _segsum(r * k2 * rk_ref[...], seg) * v

    l1 = logdec.astype(BF16)
    r1 = logdec - l1.astype(F32)
    l2 = r1.astype(BF16)
    l3 = (r1 - l2.astype(F32)).astype(BF16)
    tri = tri_ref[...]
    cum = _dot(tri, l1) + _dot(tri, l2) + _dot(tri, l3)

    kka = kk * a
    p_inv = jnp.exp(-cum)
    a_t = -kk * jnp.exp(cum - logdec)
    r_t = r * jnp.exp(cum)
    b_t = kka * p_inv
    k_t = k2 * p_inv

    r64 = lax.broadcasted_iota(jnp.int32, (CHUNK, LANES), 0)
    c64 = lax.broadcasted_iota(jnp.int32, (CHUNK, LANES), 1) & (CHUNK - 1)
    strict = c64 < r64
    incl = c64 <= r64
    eye_side = (c64 == r64).astype(F32)
    r128 = lax.broadcasted_iota(jnp.int32, (LANES, LANES), 0)
    c128 = lax.broadcasted_iota(jnp.int32, (LANES, LANES), 1)
    eye128 = (r128 == c128).astype(F32)
    same_head = (r128 >> 6) == (c128 >> 6)
    hi_half = jnp.logical_not(lo_half)
    zeros_c = jnp.zeros((CHUNK, LANES), F32)

    def _stack2(zz):
        zb = zz.astype(BF16)
        return jnp.concatenate([jnp.where(lo_half, zb, 0), jnp.where(hi_half, zb, 0)], axis=0)

    chains = [(c, pr) for c in range(n_chunks) for pr in range(N_PAIRS)]

    def _piece(arr, c, pr):
        return arr[c * CHUNK:(c + 1) * CHUNK, pr * LANES:(pr + 1) * LANES]

    x_ab, x_ak, x_rb, x_rk = {}, {}, {}, {}
    for ch in chains:
        lq = jnp.concatenate([_piece(a_t, *ch), _piece(r_t, *ch)], axis=0).astype(BF16)
        rhs = jnp.concatenate([_stack2(_piece(b_t, *ch)), _stack2(_piece(k_t, *ch))], axis=0)
        sc = _dot_nt(lq, rhs)
        x_ab[ch] = jnp.where(strict, sc[0:CHUNK, 0:LANES], 0.0)
        x_ak[ch] = jnp.where(strict, sc[0:CHUNK, LANES:2 * LANES], 0.0)
        x_rb[ch] = jnp.where(incl, sc[CHUNK:2 * CHUNK, 0:LANES], 0.0)
        x_rk[ch] = jnp.where(incl, sc[CHUNK:2 * CHUNK, LANES:2 * LANES], 0.0)

    x_t = {ch: eye_side + x_ab[ch] for ch in chains}
    x_pw = {ch: _dot(x_ab[ch].astype(BF16), _stack2(x_ab[ch])) for ch in chains}
    av = {ch: _dot(x_ak[ch].astype(BF16), _stack2(_piece(v, *ch))) for ch in chains}
    for _ in range(4):
        for ch in chains:
            both = jnp.concatenate([x_t[ch], x_pw[ch]], axis=0).astype(BF16)
            out = _dot(both, _stack2(x_pw[ch]))
            x_t[ch] = x_t[ch] + out[0:CHUNK]
            x_pw[ch] = out[CHUNK:2 * CHUNK]
    for ch in chains:
        x_t[ch] = x_t[ch] + _dot(x_t[ch].astype(BF16), _stack2(x_pw[ch]))

    tw = {}
    for ch in chains:
        rhs = jnp.concatenate([_stack2(_piece(a_t, *ch)), _stack2(av[ch])], axis=1)
        tw[ch] = _dot(x_t[ch].astype(BF16), rhs)
    qeff, yloc, phi, psi = {}, {}, {}, {}
    for ch in chains:
        ta, w_loc, v_p = tw[ch][:, 0:LANES], tw[ch][:, LANES:2 * LANES], _piece(v, *ch)
        rhs = jnp.concatenate(
            [jnp.concatenate([_stack2(ta), _stack2(w_loc)], axis=1),
             jnp.concatenate([jnp.zeros((LANES, LANES), BF16), _stack2(v_p)], axis=1)], axis=0)
        ag = _dot(jnp.concatenate([x_rb[ch], x_rk[ch]], axis=1).astype(BF16), rhs)
        qeff[ch] = _piece(r_t, *ch) + ag[:, 0:LANES]
        yloc[ch] = ag[:, LANES:2 * LANES]
        cum_c = _piece(cum, *ch)
        cum_last = cum_c[CHUNK - 1:CHUNK, :]
        to_end = jnp.exp(cum_last - cum_c)
        kk_t = jnp.concatenate([_piece(kka, *ch) * to_end, _piece(k2, *ch) * to_end], axis=0).T
        rhs = jnp.concatenate(
            [tw[ch], jnp.concatenate([zeros_c, v_p], axis=1)], axis=0).astype(BF16)
        pp = _dot(kk_t.astype(BF16), rhs)
        phi[ch] = eye128 * jnp.exp(cum_last) + jnp.where(same_head, pp[:, 0:LANES], 0.0)
        psi[ch] = jnp.where(same_head, pp[:, LANES:2 * LANES], 0.0)

    y_rows = []
    states = [h_ref[pr] for pr in range(N_PAIRS)]
    for c in range(n_chunks):
        y_c = []
        for pr in range(N_PAIRS):
            ch = (c, pr)
            lhs = jnp.concatenate([qeff[ch], phi[ch]], axis=0).astype(BF16)
            out = _dot(lhs, states[pr].astype(BF16))
            y_c.append(out[0:CHUNK] + yloc[ch])
            states[pr] = out[CHUNK:CHUNK + LANES] + psi[ch]
        y_rows.append(jnp.concatenate(y_c, axis=1))
    for pr in range(N_PAIRS):
        h_ref[pr] = states[pr]
    y = jnp.concatenate(y_rows, axis=0)

    mean = _segsum(y, seg) * (1.0 / HEAD_DIM)
    d = y - mean
    var = _segsum(d * d, seg) * (1.0 / HEAD_DIM)
    yn = d * lax.rsqrt(var + GN_EPS) * lnw_ref[...] + lnb_ref[...]
    o_ref[0] = ((yn + bonus) * g).astype(BF16)


def _rwkv(sp, w0, wda, a0, wg, k_k, k_a, r_k, ln_w, ln_b, seg, tri, tt):
    b, s, _ = sp.shape
    const2 = lambda i, j: (0, 0)
    row = lambda n: pl.BlockSpec((1, n), const2)
    return pl.pallas_call(
        functools.partial(_rwkv_kernel, tt=tt),
        grid=(b, s // tt),
        in_specs=[
            pl.BlockSpec((1, tt, SHIFT_PAD), lambda i, j: (i, j, 0)),
            row(WIDTH),
            pl.BlockSpec((LANES, 2 * WIDTH), const2),
            row(WIDTH),
            pl.BlockSpec((2 * LANES, WIDTH), const2),
            row(WIDTH), row(WIDTH), row(WIDTH), row(WIDTH), row(WIDTH),
            pl.BlockSpec((SEG_LANES, SEG_LANES), const2),
            pl.BlockSpec((tt, tt), const2),
        ],
        out_specs=pl.BlockSpec((1, tt, WIDTH), lambda i, j: (i, j, 0)),
        out_shape=jax.ShapeDtypeStruct((b, s, WIDTH), BF16),
        scratch_shapes=[
            pltpu.VMEM((N_PAIRS, LANES, LANES), F32),
        ],
        compiler_params=pltpu.CompilerParams(
            dimension_semantics=("arbitrary", "arbitrary"), vmem_limit_bytes=VMEM_LIMIT),
        name="rwkv",
    )(sp, w0, wda, a0, wg, k_k, k_a, r_k, ln_w, ln_b, seg, tri)


NEG_BIG = -(2.0 ** 30)
N_BLOCKS_MAX = 16
V_ROWS = HEAD_DIM + 16
LOG2E = 1.4426950408889634


def _aligned(i, m):
    return i * m if isinstance(i, int) else pl.multiple_of(i * m, m)


def _moba_kernel(q_ref, k_ref, v_ref, slope_ref, o_ref, kx_ref, vt_ref, qt_ref, bias_ref, *, seq):
    nb = seq // MOBA_BLOCK
    bs = MOBA_BLOCK
    scale = HEAD_DIM ** -0.5
    heads = range(N_HEADS)

    rr = lax.broadcasted_iota(jnp.int32, (bs, LANES), 0)
    ll = lax.broadcasted_iota(jnp.int32, (bs, LANES), 1)
    ones_rows = jnp.ones((V_ROWS - HEAD_DIM, bs), BF16)
    for n in range(nb):
        aug = jnp.where(ll < 4, 1.0,
              jnp.where(ll < 6, rr.astype(F32),
              jnp.where(ll < 8, float(n * bs),
              jnp.where(ll == 8 + n, 1.0, 0.0)))).astype(BF16)
        rows = slice(n * bs, (n + 1) * bs)
        for pr in range(N_PAIRS):
            ln = slice(pr * LANES, (pr + 1) * LANES)
            kx_ref[pr, n, :, 0:LANES] = k_ref[0, rows, ln]
            kx_ref[pr, n, :, LANES:2 * LANES] = aug
            v_t = v_ref[0, rows, ln].astype(F32).T.astype(BF16)
            for half in range(2):
                vt_ref[2 * pr + half, n] = jnp.concatenate(
                    [v_t[half * HEAD_DIM:(half + 1) * HEAD_DIM], ones_rows], axis=0)

    bi = lax.broadcasted_iota(jnp.int32, (N_BLOCKS_MAX, seq), 0)
    si = lax.broadcasted_iota(jnp.int32, (N_BLOCKS_MAX, seq), 1)
    avg = jnp.where((si >= bi * bs) & (si < (bi + 1) * bs), 1.0 / bs, 0.0).astype(BF16)
    kmean = _dot(avg, k_ref[0])
    km = jnp.concatenate([jnp.broadcast_to(kmean[n:n + 1, :], (N_HEADS, WIDTH))
                          for n in range(N_BLOCKS_MAX)], axis=0)
    row_head = lax.broadcasted_iota(jnp.int32, (LANES, WIDTH), 0) & (N_HEADS - 1)
    lane_head = lax.broadcasted_iota(jnp.int32, (LANES, WIDTH), 1) >> 6
    km_hi, km_lo = _split2(jnp.where(row_head == lane_head, km, 0.0))

    pr_i = lax.broadcasted_iota(jnp.int32, (LANES, LANES), 0)
    pc_i = lax.broadcasted_iota(jnp.int32, (LANES, LANES), 1)
    perm = (pc_i == ((pr_i & (N_BLOCKS_MAX - 1)) << 3) + (pr_i >> 4)).astype(BF16)
    zeros_cnt = jnp.zeros((N_HEADS, bs), F32)

    for i in range(nb):
        q_blk = q_ref[0, i * bs:(i + 1) * bs, :].astype(F32)
        q_t = [q_blk[:, pr * LANES:(pr + 1) * LANES].T for pr in range(N_PAIRS)]
        for pr in range(N_PAIRS):
            qt_ref[pr, i] = q_t[pr].astype(BF16)
        if i <= MOBA_TOP_K:
            bias_ref[i] = jnp.zeros((N_BLOCKS_MAX * N_HEADS, bs), F32)
            continue
        q_tb = jnp.concatenate(q_t, axis=0).astype(BF16)
        gate = _dot(km_hi, q_tb) + _dot(km_lo, q_tb)
        slabs = [gate[n * N_HEADS:(n + 1) * N_HEADS] for n in range(i)]
        cnt = [zeros_cnt] * i
        for n in range(i):
            for m in range(n):
                m_wins = slabs[m] >= slabs[n]
                cnt[n] = cnt[n] + jnp.where(m_wins, 1.0, 0.0)
                cnt[m] = cnt[m] + jnp.where(m_wins, 0.0, 1.0)
        bias = [jnp.where(cnt[n] < MOBA_TOP_K, 0.0, NEG_BIG) for n in range(i)]
        bias += [zeros_cnt] * (N_BLOCKS_MAX - i)
        bias_ref[i] = _dot(perm, jnp.concatenate(bias, axis=0).astype(BF16))

    r128 = lax.broadcasted_iota(jnp.int32, (LANES, 1), 0)
    row_masks = (r128 < HEAD_DIM, r128 >= HEAD_DIM)
    r8 = lax.broadcasted_iota(jnp.int32, (8, bs), 0)
    tl = lax.broadcasted_iota(jnp.int32, (8, bs), 1).astype(F32)
    key_row = lax.broadcasted_iota(jnp.int32, (bs, bs), 0)
    qry_col = lax.broadcasted_iota(jnp.int32, (bs, bs), 1)
    causal = key_row <= qry_col
    zeros_pad = jnp.zeros((LANES - 8 - N_BLOCKS_MAX, bs), F32)

    def q_block(i, carry):
        start = _aligned(i, bs)
        q_t = [qt_ref[pr, i].astype(F32) for pr in range(N_PAIRS)]
        bias = bias_ref[i]
        t0 = jnp.asarray(i * bs, F32)

        q_aug = []
        for h in heads:
            c = slope_ref[h:h + 1, :] * LOG2E
            c_hi = c.astype(BF16).astype(F32)
            ctl = c * tl
            ctl_hi = ctl.astype(BF16).astype(F32)
            ct0 = c * t0
            ct0_hi = ct0.astype(BF16).astype(F32)
            aug8 = jnp.where(r8 == 0, -ctl_hi,
                   jnp.where(r8 == 1, ctl_hi - ctl,
                   jnp.where(r8 == 2, -ct0_hi,
                   jnp.where(r8 == 3, ct0_hi - ct0,
                   jnp.where((r8 == 4) | (r8 == 6), c_hi, c - c_hi)))))
            top = jnp.where(row_masks[h % 2], q_t[h // 2], 0.0) * (scale * LOG2E)
            sel = bias[h * N_BLOCKS_MAX:(h + 1) * N_BLOCKS_MAX]
            q_aug.append(jnp.concatenate([top, aug8, sel, zeros_pad], axis=0).astype(BF16))

        def scores(j):
            kx = [kx_ref[pr, j] for pr in range(N_PAIRS)]
            return [_dot(kx[h // 2], q_aug[h]) for h in heads]

        def weighted_values(j, ps):
            return [_dot(vt_ref[h, j], ps[h]) for h in heads]

        s_own = scores(i)
        m0, p0 = [], []
        for h in heads:
            sh = jnp.where(causal, s_own[h], NEG_BIG)
            m0.append(jnp.max(sh, axis=0, keepdims=True))
            p0.append(jnp.exp2(sh - m0[h]).astype(BF16))
        pv = weighted_values(i, p0)
        acc0 = [pv[h][0:HEAD_DIM] for h in heads]
        l0 = [pv[h][HEAD_DIM:HEAD_DIM + 1] for h in heads]

        def step(js, st):
            m, l, acc = (list(x) for x in st)
            s_all = [scores(j) for j in js]
            ps, alphas = [], []
            for s in s_all:
                p_blk, a_blk = [], []
                for h in heads:
                    mh = jnp.maximum(m[h], jnp.max(s[h], axis=0, keepdims=True))
                    a_blk.append(jnp.exp2(m[h] - mh))
                    p_blk.append(jnp.exp2(s[h] - mh).astype(BF16))
                    m[h] = mh
                ps.append(p_blk)
                alphas.append(a_blk)
            pvs = [weighted_values(j, p_blk) for j, p_blk in zip(js, ps)]
            for a_blk, pv in zip(alphas, pvs):
                for h in heads:
                    acc[h] = a_blk[h] * acc[h] + pv[h][0:HEAD_DIM]
                    l[h] = a_blk[h] * l[h] + pv[h][HEAD_DIM:HEAD_DIM + 1]
            return tuple(tuple(x) for x in (m, l, acc))

        st = (tuple(m0), tuple(l0), tuple(acc0))
        st = lax.cond((i & 1) == 1, lambda st: step([i - 1], st), lambda st: st, st)
        _, l, acc = lax.fori_loop(0, i >> 1, lambda t, st: step([2 * t, 2 * t + 1], st), st)
        pairs = []
        for pr in range(N_PAIRS):
            a, b = 2 * pr, 2 * pr + 1
            out_t = jnp.concatenate([acc[a] / l[a], acc[b] / l[b]], axis=0)
            pairs.append(out_t.T)
        o_ref[0, pl.ds(start, bs), :] = jnp.concatenate(pairs, axis=1).astype(BF16)
        return carry

    lax.fori_loop(0, nb, q_block, 0)


def _moba(qkv, slopes):
    b, s, _ = qkv.shape
    assert s % MOBA_BLOCK == 0 and s // MOBA_BLOCK <= N_BLOCKS_MAX
    nb = s // MOBA_BLOCK
    col = lambda c: pl.BlockSpec((1, s, WIDTH), lambda i: (i, 0, c), pipeline_mode=pl.Buffered(1))
    return pl.pallas_call(
        functools.partial(_moba_kernel, seq=s),
        grid=(b,),
        in_specs=[col(0), col(1), col(2), pl.BlockSpec((N_HEADS, MOBA_BLOCK), lambda i: (0, 0))],
        out_specs=pl.BlockSpec((1, s, WIDTH), lambda i: (i, 0, 0)),
        out_shape=jax.ShapeDtypeStruct((b, s, WIDTH), BF16),
        scratch_shapes=[
            pltpu.VMEM((N_PAIRS, nb, MOBA_BLOCK, 2 * LANES), BF16),
            pltpu.VMEM((N_HEADS, nb, V_ROWS, MOBA_BLOCK), BF16),
            pltpu.VMEM((N_PAIRS, nb, LANES, MOBA_BLOCK), BF16),
            pltpu.VMEM((nb, N_BLOCKS_MAX * N_HEADS, MOBA_BLOCK), F32),
        ],
        compiler_params=pltpu.CompilerParams(
            dimension_semantics=("arbitrary",), vmem_limit_bytes=VMEM_LIMIT),
        name="moba",
    )(qkv, qkv, qkv, slopes)


FFN_CHUNKS = ((0, 1024), (1024, 1024), (2048, 768))


def _post_kernel(x_ref, ya_ref, yb_ref, gate_ref, wua_ref, wub_ref, wo_ref, gffn_ref,
                 wfi_ref, wfo_ref, gfin_ref, o_ref):
    y_a = _dot(ya_ref[...], wua_ref[...])
    y_b = _dot(yb_ref[...], wub_ref[...])
    ga = jax.nn.sigmoid(gate_ref[:, 0:D_MODEL].astype(F32))
    gb = jax.nn.sigmoid(gate_ref[:, D_MODEL:2 * D_MODEL].astype(F32))
    mixed = (ga * y_a + gb * y_b).astype(BF16)
    x1 = x_ref[...] + _dot(mixed, wo_ref[...])
    h = _rms(x1, gffn_ref[...]).astype(BF16)
    acc = x1
    for off, n in FFN_CHUNKS:
        gg = _dot(h, wfi_ref[:, off:off + n])
        uu = _dot(h, wfi_ref[:, FFN_HIDDEN + off:FFN_HIDDEN + off + n])
        act = (gg * jax.nn.sigmoid(gg) * uu).astype(BF16)
        acc = acc + _dot(act, wfo_ref[off:off + n, :])
    o_ref[...] = _rms(acc, gfin_ref[...])


def _post(x2d, ya, yb, gates, wua, wub, wo, gffn, wfi, wfo, gfin, tm):
    t = x2d.shape[0]
    const = lambda i: (0, 0)
    tile = lambda n: pl.BlockSpec((tm, n), lambda i: (i, 0))
    weight = lambda a: pl.BlockSpec(a.shape, const, pipeline_mode=pl.Buffered(1))
    return pl.pallas_call(
        _post_kernel,
        grid=(t // tm,),
        in_specs=[tile(D_MODEL), tile(WIDTH), tile(WIDTH), tile(GATES),
                  weight(wua), weight(wub), weight(wo), weight(gffn),
                  weight(wfi), weight(wfo), weight(gfin)],
        out_specs=tile(D_MODEL),
        out_shape=jax.ShapeDtypeStruct((t, D_MODEL), F32),
        compiler_params=pltpu.CompilerParams(
            dimension_semantics=("arbitrary",), vmem_limit_bytes=VMEM_LIMIT),
        name="post",
    )(x2d, ya, yb, gates, wua, wub, wo, gffn, wfi, wfo, gfin)


def _layer(x, norm_mix_g, w_in, mu_shift, w0, w_decay_up, a0, w_iclr_up, w_gate_up,
           k_k, k_a, r_k, ln_x_w, ln_x_b, w_up_a, w_up_b, w_o, norm_ffn_g,
           w_ffn_in, w_ffn_out, out_g):
    b, s, _ = x.shape
    t = b * s
    x2d = x.reshape(t, D_MODEL)

    pad = SHIFT_PAD - SHIFT_WIDTH
    w_sp = jnp.pad(w_in[:, :SHIFT_WIDTH].astype(BF16), ((0, 0), (0, pad)))
    w_qkv = w_in[:, SHIFT_WIDTH:SHIFT_WIDTH + QKV_B].astype(BF16)
    w_gate = w_in[:, SHIFT_WIDTH + QKV_B:].astype(BF16)
    mu = jnp.pad(mu_shift, (0, pad)).reshape(1, SHIFT_PAD)
    wda = jnp.zeros((LANES, 2 * WIDTH), F32)
    wda = wda.at[:DECAY_LORA, :WIDTH].set(w_decay_up).at[DECAY_LORA:, WIDTH:].set(w_iclr_up)
    wg = jnp.pad(w_gate_up, ((0, 2 * LANES - GATE_LORA), (0, 0)))
    row = lambda a: a.reshape(1, -1)

    tt = 256
    hid = jnp.arange(SEG_LANES) // HEAD_DIM
    seg = (hid[:, None] == hid[None, :]).astype(BF16)
    ti = jnp.arange(tt)
    tri = ((ti[:, None] >= ti[None, :]) & (ti[:, None] // CHUNK == ti[None, :] // CHUNK)).astype(BF16)

    sp, qkv, gates = _proj(x2d, row(norm_mix_g), w_sp, w_qkv, w_gate, mu, tm=512, seq=s)
    ya = _rwkv(sp.reshape(b, s, SHIFT_PAD), row(w0), wda.astype(BF16), row(a0),
               wg.astype(BF16), row(k_k), row(k_a), row(r_k), row(ln_x_w), row(ln_x_b),
               seg, tri, tt)
    slopes = jnp.exp2(-8.0 * jnp.arange(1, N_HEADS + 1, dtype=F32) / N_HEADS)
    slopes = jnp.broadcast_to(slopes[:, None], (N_HEADS, MOBA_BLOCK))
    yb = _moba(qkv.reshape(b, s, QKV_B), slopes)
    out = _post(x2d, ya.reshape(t, WIDTH), yb.reshape(t, WIDTH), gates,
                w_up_a.astype(BF16), w_up_b.astype(BF16), w_o.astype(BF16), row(norm_ffn_g),
                w_ffn_in.astype(BF16), w_ffn_out.astype(BF16), row(out_g), tm=512)
    return out.reshape(b, s, D_MODEL)


def kernel(x, norm_mix_g, w_in, mu_shift, w0, w_decay_up, a0, w_iclr_up, w_gate_up, k_k, k_a, r_k, ln_x_w, ln_x_b, w_up_a, w_up_b, w_o, norm_ffn_g, w_ffn_in, w_ffn_out, norm_final_g):
    depth = w_in.shape[0]
    assert depth == 1, "the fused post kernel applies the final norm after the only layer"
    return _layer(x, norm_mix_g[0], w_in[0], mu_shift[0], w0[0], w_decay_up[0], a0[0],
                  w_iclr_up[0], w_gate_up[0], k_k[0], k_a[0], r_k[0].reshape(-1), ln_x_w[0],
                  ln_x_b[0], w_up_a[0], w_up_b[0], w_o[0], norm_ffn_g[0], w_ffn_in[0],
                  w_ffn_out[0], norm_final_g)
```

```python
import functools

import jax
import jax.numpy as jnp
from jax import lax
from jax.experimental import pallas as pl
from jax.experimental.pallas import tpu as pltpu

F32 = jnp.float32
BF16 = jnp.bfloat16

D_MODEL = 1024
HEAD_DIM = 64
N_HEADS = 8
WIDTH = N_HEADS * HEAD_DIM
N_PAIRS = N_HEADS // 2
LANES = 128
DECAY_LORA = 64
ICLR_LORA = 64
GATE_LORA = 160
GN_EPS = 64e-5
EXP_NEG_HALF = 0.6065306597126334
RMS_EPS = 1e-6
MOBA_BLOCK = 256
MOBA_TOP_K = 3
FFN_HIDDEN = 2816
SHIFT_WIDTH = 3 * WIDTH + DECAY_LORA + ICLR_LORA + GATE_LORA
SHIFT_PAD = 1920
QKV_B = 3 * WIDTH
GATES = 2 * D_MODEL
PROJ_PAD = SHIFT_PAD + QKV_B + GATES

CHUNK = 64
VMEM_LIMIT = 56 * 1024 * 1024
MOBA_VMEM_LIMIT = 60 * 1024 * 1024


def _dot(a, b):
    return jnp.dot(a, b, preferred_element_type=F32)


def _dot_nt(a, b):
    return lax.dot_general(a, b, (((1,), (1,)), ((), ())), preferred_element_type=F32)


def _rms(x, g):
    return x * lax.rsqrt(jnp.mean(x * x, axis=-1, keepdims=True) + RMS_EPS) * g


def _proj_kernel(x_ref, g_ref, wsp_ref, wqkv_ref, wgate_ref, mu_ref, sp_ref, qkv_ref, gate_ref,
                 carry_ref, *, tiles_per_seq):
    tm = x_ref.shape[0]
    step = pl.program_id(0)

    @pl.when(step == 0)
    def _():
        carry_ref[...] = jnp.zeros_like(carry_ref)

    h = _rms(x_ref[...], g_ref[...]).astype(BF16)
    qkv_ref[...] = _dot(h, wqkv_ref[...]).astype(BF16)
    gate_ref[...] = _dot(h, wgate_ref[...]).astype(BF16)

    p = _dot(h, wsp_ref[...])
    seq_start = lax.rem(step, tiles_per_seq) == 0
    carry = jnp.where(seq_start, 0.0, carry_ref[0:1, :])
    row = lax.broadcasted_iota(jnp.int32, (tm, 1), 0)
    prev = jnp.where(row == 0, carry, pltpu.roll(p, 1, 0))
    carry_ref[0:1, :] = p[tm - 1:tm, :]
    sp_ref[...] = p + mu_ref[...] * (prev - p)


def _proj(x2d, g, w_sp, w_qkv, w_gate, mu, tm, seq):
    t = x2d.shape[0]
    const = lambda i: (0, 0)
    weight = lambda n: pl.BlockSpec((D_MODEL, n), const, pipeline_mode=pl.Buffered(1))
    return pl.pallas_call(
        functools.partial(_proj_kernel, tiles_per_seq=seq // tm),
        grid=(t // tm,),
        in_specs=[
            pl.BlockSpec((tm, D_MODEL), lambda i: (i, 0)),
            pl.BlockSpec((1, D_MODEL), const),
            weight(SHIFT_PAD), weight(QKV_B), weight(GATES),
            pl.BlockSpec((1, SHIFT_PAD), const),
        ],
        scratch_shapes=[pltpu.VMEM((8, SHIFT_PAD), F32)],
        out_specs=[
            pl.BlockSpec((tm, SHIFT_PAD), lambda i: (i, 0)),
            pl.BlockSpec((tm, QKV_B), lambda i: (i, 0)),
            pl.BlockSpec((tm, GATES), lambda i: (i, 0)),
        ],
        out_shape=[
            jax.ShapeDtypeStruct((t, SHIFT_PAD), F32),
            jax.ShapeDtypeStruct((t, QKV_B), BF16),
            jax.ShapeDtypeStruct((t, GATES), BF16),
        ],
        compiler_params=pltpu.CompilerParams(
            dimension_semantics=("arbitrary",), vmem_limit_bytes=VMEM_LIMIT),
        name="proj",
    )(x2d, g, w_sp, w_qkv, w_gate, mu)


def _split2(x):
    hi = x.astype(BF16)
    lo = (x - hi.astype(F32)).astype(BF16)
    return hi, lo


SEG_LANES = 2 * LANES


def _segsum(x, seg):
    xb = x.astype(BF16)
    return jnp.concatenate(
        [_dot(xb[:, o:o + SEG_LANES], seg) for o in range(0, x.shape[1], SEG_LANES)], axis=1)


N_STREAMS = 2


def _rwkv_kernel(sp_ref, w0_ref, wda_ref, a0_ref, wg_ref, kk_ref, ka_ref, rk_ref,
                 lnw_ref, lnb_ref, seg_ref, tri_ref, o_ref, h_ref, *, tt):
    @pl.when(pl.program_id(1) == 0)
    def _():
        h_ref[...] = jnp.zeros_like(h_ref)

    params = (w0_ref, wda_ref, a0_ref, wg_ref, kk_ref, ka_ref, rk_ref, lnw_ref, lnb_ref,
              seg_ref, tri_ref)
    live = [_rwkv_stream(k, sp_ref, o_ref, h_ref, *params, tt=tt) for k in range(N_STREAMS)]
    done = object()
    while live:
        live = [gen for gen in live if next(gen, done) is not done]


def _rwkv_stream(k_seq, sp_ref, o_ref, h_ref, w0_ref, wda_ref, a0_ref, wg_ref, kk_ref, ka_ref,
                 rk_ref, lnw_ref, lnb_ref, seg_ref, tri_ref, *, tt):
    n_chunks = tt // CHUNK
    xs = sp_ref[k_seq, 0]
    r = xs[:, 0:WIDTH]
    k = xs[:, WIDTH:2 * WIDTH]
    v = xs[:, 2 * WIDTH:3 * WIDTH]
    da = xs[:, 3 * WIDTH:3 * WIDTH + LANES]
    gd = xs[:, 3 * WIDTH + LANES:SHIFT_PAD]
    lane = lax.broadcasted_iota(jnp.int32, (1, LANES), 1)
    lo_half = lane < HEAD_DIM
    da_act = jnp.where(lo_half, jnp.tanh(da), da).astype(BF16)
    lora = _dot(da_act, wda_ref[...])
    g = _dot(jax.nn.sigmoid(gd).astype(BF16), wg_ref[...])

    logdec = -EXP_NEG_HALF * jax.nn.sigmoid(w0_ref[...] + lora[:, 0:WIDTH])
    a = jax.nn.sigmoid(a0_ref[...] + lora[:, WIDTH:2 * WIDTH])

    seg = seg_ref[...]
    kkf = k * kk_ref[...]
    kk_sq = _segsum(kkf * kkf, seg)
    yield
    kk = kkf * lax.rsqrt(jnp.maximum(kk_sq, 1e-24))
    k2 = k * (1.0 + (a - 1.0) * ka_ref[...])
    bonus = _segsum(r * k2 * rk_ref[...], seg) * v

    l1 = logdec.astype(BF16)
    r1 = logdec - l1.astype(F32)
    l2 = r1.astype(BF16)
    l3 = (r1 - l2.astype(F32)).astype(BF16)
    tri = tri_ref[...]
    cum = _dot(tri, l1) + _dot(tri, l2) + _dot(tri, l3)
    yield

    kka = kk * a
    p_inv = jnp.exp(-cum)
    a_t = -kk * jnp.exp(cum - logdec)
    r_t = r * jnp.exp(cum)
    b_t = kka * p_inv
    k_t = k2 * p_inv

    r64 = lax.broadcasted_iota(jnp.int32, (CHUNK, LANES), 0)
    c64 = lax.broadcasted_iota(jnp.int32, (CHUNK, LANES), 1) & (CHUNK - 1)
    strict = c64 < r64
    incl = c64 <= r64
    eye_side = (c64 == r64).astype(F32)
    r128 = lax.broadcasted_iota(jnp.int32, (LANES, LANES), 0)
    c128 = lax.broadcasted_iota(jnp.int32, (LANES, LANES), 1)
    eye128 = (r128 == c128).astype(F32)
    same_head = (r128 >> 6) == (c128 >> 6)
    hi_half = jnp.logical_not(lo_half)
    zeros_c = jnp.zeros((CHUNK, LANES), F32)

    def _stack2(zz):
        zb = zz.astype(BF16)
        return jnp.concatenate([jnp.where(lo_half, zb, 0), jnp.where(hi_half, zb, 0)], axis=0)

    chains = [(c, pr) for c in range(n_chunks) for pr in range(N_PAIRS)]

    def _piece(arr, c, pr):
        return arr[c * CHUNK:(c + 1) * CHUNK, pr * LANES:(pr + 1) * LANES]

    x_ab, x_ak, x_rb, x_rk = {}, {}, {}, {}
    for ch in chains:
        lq = jnp.concatenate([_piece(a_t, *ch), _piece(r_t, *ch)], axis=0).astype(BF16)
        rhs = jnp.concatenate([_stack2(_piece(b_t, *ch)), _stack2(_piece(k_t, *ch))], axis=0)
        sc = _dot_nt(lq, rhs)
        x_ab[ch] = jnp.where(strict, sc[0:CHUNK, 0:LANES], 0.0)
        x_ak[ch] = jnp.where(strict, sc[0:CHUNK, LANES:2 * LANES], 0.0)
        x_rb[ch] = jnp.where(incl, sc[CHUNK:2 * CHUNK, 0:LANES], 0.0)
        x_rk[ch] = jnp.where(incl, sc[CHUNK:2 * CHUNK, LANES:2 * LANES], 0.0)
    yield

    x_t = {ch: eye_side + x_ab[ch] for ch in chains}
    x_pw = {ch: _dot(x_ab[ch].astype(BF16), _stack2(x_ab[ch])) for ch in chains}
    av = {ch: _dot(x_ak[ch].astype(BF16), _stack2(_piece(v, *ch))) for ch in chains}
    yield
    for _ in range(4):
        for ch in chains:
            both = jnp.concatenate([x_t[ch], x_pw[ch]], axis=0).astype(BF16)
            out = _dot(both, _stack2(x_pw[ch]))
            x_t[ch] = x_t[ch] + out[0:CHUNK]
            x_pw[ch] = out[CHUNK:2 * CHUNK]
        yield
    for ch in chains:
        x_t[ch] = x_t[ch] + _dot(x_t[ch].astype(BF16), _stack2(x_pw[ch]))
    yield

    tw = {}
    for ch in chains:
        rhs = jnp.concatenate([_stack2(_piece(a_t, *ch)), _stack2(av[ch])], axis=1)
        tw[ch] = _dot(x_t[ch].astype(BF16), rhs)
    yield
    qeff, yloc, phi, psi = {}, {}, {}, {}
    for ch in chains:
        ta, w_loc, v_p = tw[ch][:, 0:LANES], tw[ch][:, LANES:2 * LANES], _piece(v, *ch)
        rhs = jnp.concatenate(
            [jnp.concatenate([_stack2(ta), _stack2(w_loc)], axis=1),
             jnp.concatenate([jnp.zeros((LANES, LANES), BF16), _stack2(v_p)], axis=1)], axis=0)
        ag = _dot(jnp.concatenate([x_rb[ch], x_rk[ch]], axis=1).astype(BF16), rhs)
        qeff[ch] = _piece(r_t, *ch) + ag[:, 0:LANES]
        yloc[ch] = ag[:, LANES:2 * LANES]
        cum_c = _piece(cum, *ch)
        cum_last = cum_c[CHUNK - 1:CHUNK, :]
        to_end = jnp.exp(cum_last - cum_c)
        kk_t = jnp.concatenate([_piece(kka, *ch) * to_end, _piece(k2, *ch) * to_end], axis=0).T
        rhs = jnp.concatenate(
            [tw[ch], jnp.concatenate([zeros_c, v_p], axis=1)], axis=0).astype(BF16)
        pp = _dot(kk_t.astype(BF16), rhs)
        phi[ch] = eye128 * jnp.exp(cum_last) + jnp.where(same_head, pp[:, 0:LANES], 0.0)
        psi[ch] = jnp.where(same_head, pp[:, LANES:2 * LANES], 0.0)
    yield

    y_rows = []
    states = [h_ref[k_seq, pr] for pr in range(N_PAIRS)]
    for c in range(n_chunks):
        y_c = []
        for pr in range(N_PAIRS):
            ch = (c, pr)
            lhs = jnp.concatenate([qeff[ch], phi[ch]], axis=0).astype(BF16)
            out = _dot(lhs, states[pr].astype(BF16))
            y_c.append(out[0:CHUNK] + yloc[ch])
            states[pr] = out[CHUNK:CHUNK + LANES] + psi[ch]
        y_rows.append(jnp.concatenate(y_c, axis=1))
        yield
    for pr in range(N_PAIRS):
        h_ref[k_seq, pr] = states[pr]
    y = jnp.concatenate(y_rows, axis=0)

    mean = _segsum(y, seg) * (1.0 / HEAD_DIM)
    yield
    d = y - mean
    var = _segsum(d * d, seg) * (1.0 / HEAD_DIM)
    yield
    yn = d * lax.rsqrt(var + GN_EPS) * lnw_ref[...] + lnb_ref[...]
    o_ref[k_seq, 0] = ((yn + bonus) * g).astype(BF16)


def _rwkv(sp, w0, wda, a0, wg, k_k, k_a, r_k, ln_w, ln_b, seg, tri, tt):
    b, s, _ = sp.shape
    assert b % N_STREAMS == 0
    rows = b // N_STREAMS
    const2 = lambda i, j: (0, 0)
    row = lambda n: pl.BlockSpec((1, n), const2)
    out = pl.pallas_call(
        functools.partial(_rwkv_kernel, tt=tt),
        grid=(rows, s // tt),
        in_specs=[
            pl.BlockSpec((N_STREAMS, 1, tt, SHIFT_PAD), lambda i, j: (0, i, j, 0)),
            row(WIDTH),
            pl.BlockSpec((LANES, 2 * WIDTH), const2),
            row(WIDTH),
            pl.BlockSpec((2 * LANES, WIDTH), const2),
            row(WIDTH), row(WIDTH), row(WIDTH), row(WIDTH), row(WIDTH),
            pl.BlockSpec((SEG_LANES, SEG_LANES), const2),
            pl.BlockSpec((tt, tt), const2),
        ],
        out_specs=pl.BlockSpec((N_STREAMS, 1, tt, WIDTH), lambda i, j: (0, i, j, 0)),
        out_shape=jax.ShapeDtypeStruct((N_STREAMS, rows, s, WIDTH), BF16),
        scratch_shapes=[
            pltpu.VMEM((N_STREAMS, N_PAIRS, LANES, LANES), F32),
        ],
        compiler_params=pltpu.CompilerParams(
            dimension_semantics=("arbitrary", "arbitrary"), vmem_limit_bytes=VMEM_LIMIT),
        name="rwkv",
    )(sp.reshape(N_STREAMS, rows, s, SHIFT_PAD), w0, wda, a0, wg, k_k, k_a, r_k, ln_w, ln_b,
      seg, tri)
    return out.reshape(b, s, WIDTH)


NEG_BIG = -(2.0 ** 30)
N_BLOCKS_MAX = 16
V_ROWS = HEAD_DIM + 16
LOG2E = 1.4426950408889634


def _aligned(i, m):
    return i * m if isinstance(i, int) else pl.multiple_of(i * m, m)


def _moba_kernel(q_ref, k_ref, v_ref, slope_ref, o_ref, kx_ref, vt_ref, qt_ref, bias_ref, *, seq):
    nb = seq // MOBA_BLOCK
    bs = MOBA_BLOCK
    scale = HEAD_DIM ** -0.5
    heads = range(N_HEADS)

    rr = lax.broadcasted_iota(jnp.int32, (bs, LANES), 0)
    ll = lax.broadcasted_iota(jnp.int32, (bs, LANES), 1)
    ones_rows = jnp.ones((V_ROWS - HEAD_DIM, bs), BF16)
    for n in range(nb):
        aug = jnp.where(ll < 4, 1.0,
              jnp.where(ll < 6, rr.astype(F32),
              jnp.where(ll < 8, float(n * bs),
              jnp.where(ll == 8 + n, 1.0, 0.0)))).astype(BF16)
        rows = slice(n * bs, (n + 1) * bs)
        for pr in range(N_PAIRS):
            ln = slice(pr * LANES, (pr + 1) * LANES)
            kx_ref[pr, n, :, 0:LANES] = k_ref[0, rows, ln]
            kx_ref[pr, n, :, LANES:2 * LANES] = aug
            v_t = v_ref[0, rows, ln].astype(F32).T.astype(BF16)
            for half in range(2):
                vt_ref[2 * pr + half, n] = jnp.concatenate(
                    [v_t[half * HEAD_DIM:(half + 1) * HEAD_DIM], ones_rows], axis=0)

    bi = lax.broadcasted_iota(jnp.int32, (N_BLOCKS_MAX, seq), 0)
    si = lax.broadcasted_iota(jnp.int32, (N_BLOCKS_MAX, seq), 1)
    avg = jnp.where((si >= bi * bs) & (si < (bi + 1) * bs), 1.0 / bs, 0.0).astype(BF16)
    kmean = _dot(avg, k_ref[0])
    km = jnp.concatenate([jnp.broadcast_to(kmean[n:n + 1, :], (N_HEADS, WIDTH))
                          for n in range(N_BLOCKS_MAX)], axis=0)
    row_head = lax.broadcasted_iota(jnp.int32, (LANES, WIDTH), 0) & (N_HEADS - 1)
    lane_head = lax.broadcasted_iota(jnp.int32, (LANES, WIDTH), 1) >> 6
    km_hi, km_lo = _split2(jnp.where(row_head == lane_head, km, 0.0))

    pr_i = lax.broadcasted_iota(jnp.int32, (LANES, LANES), 0)
    pc_i = lax.broadcasted_iota(jnp.int32, (LANES, LANES), 1)
    perm = (pc_i == ((pr_i & (N_BLOCKS_MAX - 1)) << 3) + (pr_i >> 4)).astype(BF16)
    zeros_cnt = jnp.zeros((N_HEADS, bs), F32)

    for i in range(nb):
        q_blk = q_ref[0, i * bs:(i + 1) * bs, :].astype(F32)
        q_t = [q_blk[:, pr * LANES:(pr + 1) * LANES].T for pr in range(N_PAIRS)]
        for pr in range(N_PAIRS):
            qt_ref[pr, i] = q_t[pr].astype(BF16)
        if i <= MOBA_TOP_K:
            bias_ref[i] = jnp.zeros((N_BLOCKS_MAX * N_HEADS, bs), F32)
            continue
        q_tb = jnp.concatenate(q_t, axis=0).astype(BF16)
        gate = _dot(km_hi, q_tb) + _dot(km_lo, q_tb)
        slabs = [gate[n * N_HEADS:(n + 1) * N_HEADS] for n in range(i)]
        cnt = [zeros_cnt] * i
        for n in range(i):
            for m in range(n):
                m_wins = slabs[m] >= slabs[n]
                cnt[n] = cnt[n] + jnp.where(m_wins, 1.0, 0.0)
                cnt[m] = cnt[m] + jnp.where(m_wins, 0.0, 1.0)
        bias = [jnp.where(cnt[n] < MOBA_TOP_K, 0.0, NEG_BIG) for n in range(i)]
        bias += [zeros_cnt] * (N_BLOCKS_MAX - i)
        bias_ref[i] = _dot(perm, jnp.concatenate(bias, axis=0).astype(BF16))

    r128 = lax.broadcasted_iota(jnp.int32, (LANES, 1), 0)
    row_masks = (r128 < HEAD_DIM, r128 >= HEAD_DIM)
    r8 = lax.broadcasted_iota(jnp.int32, (8, bs), 0)
    tl = lax.broadcasted_iota(jnp.int32, (8, bs), 1).astype(F32)
    key_row = lax.broadcasted_iota(jnp.int32, (bs, bs), 0)
    qry_col = lax.broadcasted_iota(jnp.int32, (bs, bs), 1)
    causal = key_row <= qry_col
    zeros_pad = jnp.zeros((LANES - 8 - N_BLOCKS_MAX, bs), F32)

    def q_block(i, carry):
        start = _aligned(i, bs)
        q_t = [qt_ref[pr, i].astype(F32) for pr in range(N_PAIRS)]
        bias = bias_ref[i]
        t0 = jnp.asarray(i * bs, F32)

        q_aug = []
        for h in heads:
            c = slope_ref[h:h + 1, :] * LOG2E
            c_hi = c.astype(BF16).astype(F32)
            ctl = c * tl
            ctl_hi = ctl.astype(BF16).astype(F32)
            ct0 = c * t0
            ct0_hi = ct0.astype(BF16).astype(F32)
            aug8 = jnp.where(r8 == 0, -ctl_hi,
                   jnp.where(r8 == 1, ctl_hi - ctl,
                   jnp.where(r8 == 2, -ct0_hi,
                   jnp.where(r8 == 3, ct0_hi - ct0,
                   jnp.where((r8 == 4) | (r8 == 6), c_hi, c - c_hi)))))
            top = jnp.where(row_masks[h % 2], q_t[h // 2], 0.0) * (scale * LOG2E)
            sel = bias[h * N_BLOCKS_MAX:(h + 1) * N_BLOCKS_MAX]
            q_aug.append(jnp.concatenate([top, aug8, sel, zeros_pad], axis=0).astype(BF16))

        def scores(j):
            kx = [kx_ref[pr, j] for pr in range(N_PAIRS)]
            return [_dot(kx[h // 2], q_aug[h]) for h in heads]

        def weighted_values(j, ps):
            return [_dot(vt_ref[h, j], ps[h]) for h in heads]

        s_own = scores(i)
        m0, p0 = [], []
        for h in heads:
            sh = jnp.where(causal, s_own[h], NEG_BIG)
            m0.append(jnp.max(sh, axis=0, keepdims=True))
            p0.append(jnp.exp2(sh - m0[h]).astype(BF16))
        pv = weighted_values(i, p0)
        acc0 = [pv[h][0:HEAD_DIM] for h in heads]
        l0 = [pv[h][HEAD_DIM:HEAD_DIM + 1] for h in heads]

        def step(js, st):
            m, l, acc = (list(x) for x in st)
            s_all = [scores(j) for j in js]
            ps, alphas = [], []
            for s in s_all:
                p_blk, a_blk = [], []
                for h in heads:
                    mh = jnp.maximum(m[h], jnp.max(s[h], axis=0, keepdims=True))
                    a_blk.append(jnp.exp2(m[h] - mh))
                    p_blk.append(jnp.exp2(s[h] - mh).astype(BF16))
                    m[h] = mh
                ps.append(p_blk)
                alphas.append(a_blk)
            pvs = [weighted_values(j, p_blk) for j, p_blk in zip(js, ps)]
            for a_blk, pv in zip(alphas, pvs):
                for h in heads:
                    acc[h] = a_blk[h] * acc[h] + pv[h][0:HEAD_DIM]
                    l[h] = a_blk[h] * l[h] + pv[h][HEAD_DIM:HEAD_DIM + 1]
            return tuple(tuple(x) for x in (m, l, acc))

        st = (tuple(m0), tuple(l0), tuple(acc0))
        st = lax.cond((i & 1) == 1, lambda st: step([i - 1], st), lambda st: st, st)
        _, l, acc = lax.fori_loop(0, i >> 1, lambda t, st: step([2 * t, 2 * t + 1], st), st)
        pairs = []
        for pr in range(N_PAIRS):
            a, b = 2 * pr, 2 * pr + 1
            out_t = jnp.concatenate([acc[a] / l[a], acc[b] / l[b]], axis=0)
            pairs.append(out_t.T)
        o_ref[0, pl.ds(start, bs), :] = jnp.concatenate(pairs, axis=1).astype(BF16)
        return carry

    lax.fori_loop(0, nb, q_block, 0)


def _moba(qkv, slopes):
    b, s, _ = qkv.shape
    assert s % MOBA_BLOCK == 0 and s // MOBA_BLOCK <= N_BLOCKS_MAX
    nb = s // MOBA_BLOCK
    col = lambda c: pl.BlockSpec((1, s, WIDTH), lambda i: (i, 0, c))
    return pl.pallas_call(
        functools.partial(_moba_kernel, seq=s),
        grid=(b,),
        in_specs=[col(0), col(1), col(2), pl.BlockSpec((N_HEADS, MOBA_BLOCK), lambda i: (0, 0))],
        out_specs=pl.BlockSpec((1, s, WIDTH), lambda i: (i, 0, 0)),
        out_shape=jax.ShapeDtypeStruct((b, s, WIDTH), BF16),
        scratch_shapes=[
            pltpu.VMEM((N_PAIRS, nb, MOBA_BLOCK, 2 * LANES), BF16),
            pltpu.VMEM((N_HEADS, nb, V_ROWS, MOBA_BLOCK), BF16),
            pltpu.VMEM((N_PAIRS, nb, LANES, MOBA_BLOCK), BF16),
            pltpu.VMEM((nb, N_BLOCKS_MAX * N_HEADS, MOBA_BLOCK), F32),
        ],
        compiler_params=pltpu.CompilerParams(
            dimension_semantics=("arbitrary",), vmem_limit_bytes=MOBA_VMEM_LIMIT),
        name="moba",
    )(qkv, qkv, qkv, slopes)


FFN_CHUNKS = ((0, 1024), (1024, 1024), (2048, 768))


def _post_kernel(x_ref, ya_ref, yb_ref, gate_ref, wua_ref, wub_ref, wo_ref, gffn_ref,
                 wfi_ref, wfo_ref, gfin_ref, o_ref):
    y_a = _dot(ya_ref[...], wua_ref[...])
    y_b = _dot(yb_ref[...], wub_ref[...])
    ga = jax.nn.sigmoid(gate_ref[:, 0:D_MODEL].astype(F32))
    gb = jax.nn.sigmoid(gate_ref[:, D_MODEL:2 * D_MODEL].astype(F32))
    mixed = (ga * y_a + gb * y_b).astype(BF16)
    x1 = x_ref[...] + _dot(mixed, wo_ref[...])
    h = _rms(x1, gffn_ref[...]).astype(BF16)
    acc = x1
    for off, n in FFN_CHUNKS:
        gg = _dot(h, wfi_ref[:, off:off + n])
        uu = _dot(h, wfi_ref[:, FFN_HIDDEN + off:FFN_HIDDEN + off + n])
        act = (gg * jax.nn.sigmoid(gg) * uu).astype(BF16)
        acc = acc + _dot(act, wfo_ref[off:off + n, :])
    o_ref[...] = _rms(acc, gfin_ref[...])


def _post(x2d, ya, yb, gates, wua, wub, wo, gffn, wfi, wfo, gfin, tm):
    t = x2d.shape[0]
    const = lambda i: (0, 0)
    tile = lambda n: pl.BlockSpec((tm, n), lambda i: (i, 0))
    weight = lambda a: pl.BlockSpec(a.shape, const, pipeline_mode=pl.Buffered(1))
    return pl.pallas_call(
        _post_kernel,
        grid=(t // tm,),
        in_specs=[tile(D_MODEL), tile(WIDTH), tile(WIDTH), tile(GATES),
                  weight(wua), weight(wub), weight(wo), weight(gffn),
                  weight(wfi), weight(wfo), weight(gfin)],
        out_specs=tile(D_MODEL),
        out_shape=jax.ShapeDtypeStruct((t, D_MODEL), F32),
        compiler_params=pltpu.CompilerParams(
            dimension_semantics=("arbitrary",), vmem_limit_bytes=VMEM_LIMIT),
        name="post",
    )(x2d, ya, yb, gates, wua, wub, wo, gffn, wfi, wfo, gfin)


def _layer(x, norm_mix_g, w_in, mu_shift, w0, w_decay_up, a0, w_iclr_up, w_gate_up,
           k_k, k_a, r_k, ln_x_w, ln_x_b, w_up_a, w_up_b, w_o, norm_ffn_g,
           w_ffn_in, w_ffn_out, out_g):
    b, s, _ = x.shape
    t = b * s
    x2d = x.reshape(t, D_MODEL)

    pad = SHIFT_PAD - SHIFT_WIDTH
    w_sp = jnp.pad(w_in[:, :SHIFT_WIDTH].astype(BF16), ((0, 0), (0, pad)))
    w_qkv = w_in[:, SHIFT_WIDTH:SHIFT_WIDTH + QKV_B].astype(BF16)
    w_gate = w_in[:, SHIFT_WIDTH + QKV_B:].astype(BF16)
    mu = jnp.pad(mu_shift, (0, pad)).reshape(1, SHIFT_PAD)
    wda = jnp.zeros((LANES, 2 * WIDTH), F32)
    wda = wda.at[:DECAY_LORA, :WIDTH].set(w_decay_up).at[DECAY_LORA:, WIDTH:].set(w_iclr_up)
    wg = jnp.pad(w_gate_up, ((0, 2 * LANES - GATE_LORA), (0, 0)))
    row = lambda a: a.reshape(1, -1)

    tt = 256
    hid = jnp.arange(SEG_LANES) // HEAD_DIM
    seg = (hid[:, None] == hid[None, :]).astype(BF16)
    ti = jnp.arange(tt)
    tri = ((ti[:, None] >= ti[None, :]) & (ti[:, None] // CHUNK == ti[None, :] // CHUNK)).astype(BF16)

    sp, qkv, gates = _proj(x2d, row(norm_mix_g), w_sp, w_qkv, w_gate, mu, tm=512, seq=s)
    ya = _rwkv(sp.reshape(b, s, SHIFT_PAD), row(w0), wda.astype(BF16), row(a0),
               wg.astype(BF16), row(k_k), row(k_a), row(r_k), row(ln_x_w), row(ln_x_b),
               seg, tri, tt)
    slopes = jnp.exp2(-8.0 * jnp.arange(1, N_HEADS + 1, dtype=F32) / N_HEADS)
    slopes = jnp.broadcast_to(slopes[:, None], (N_HEADS, MOBA_BLOCK))
    yb = _moba(qkv.reshape(b, s, QKV_B), slopes)
    out = _post(x2d, ya.reshape(t, WIDTH), yb.reshape(t, WIDTH), gates,
                w_up_a.astype(BF16), w_up_b.astype(BF16), w_o.astype(BF16), row(norm_ffn_g),
                w_ffn_in.astype(BF16), w_ffn_out.astype(BF16), row(out_g), tm=512)
    return out.reshape(b, s, D_MODEL)


def kernel(x, norm_mix_g, w_in, mu_shift, w0, w_decay_up, a0, w_iclr_up, w_gate_up, k_k, k_a, r_k, ln_x_w, ln_x_b, w_up_a, w_up_b, w_o, norm_ffn_g, w_ffn_in, w_ffn_out, norm_final_g):
    depth = w_in.shape[0]
    assert depth == 1, "the fused post kernel applies the final norm after the only layer"
    return _layer(x, norm_mix_g[0], w_in[0], mu_shift[0], w0[0], w_decay_up[0], a0[0],
                  w_iclr_up[0], w_gate_up[0], k_k[0], k_a[0], r_k[0].reshape(-1), ln_x_w[0],
                  ln_x_b[0], w_up_a[0], w_up_b[0], w_o[0], norm_ffn_g[0], w_ffn_in[0],
                  w_ffn_out[0], norm_final_g)
```

```python
import functools

import jax
import jax.numpy as jnp
from jax import lax
from jax.experimental import pallas as pl
from jax.experimental.pallas import tpu as pltpu

F32 = jnp.float32
BF16 = jnp.bfloat16

D_MODEL = 1024
HEAD_DIM = 64
N_HEADS = 8
WIDTH = N_HEADS * HEAD_DIM
N_PAIRS = N_HEADS // 2
LANES = 128
DECAY_LORA = 64
ICLR_LORA = 64
GATE_LORA = 160
GN_EPS = 64e-5
EXP_NEG_HALF = 0.6065306597126334
RMS_EPS = 1e-6
MOBA_BLOCK = 256
MOBA_TOP_K = 3
FFN_HIDDEN = 2816
SHIFT_WIDTH = 3 * WIDTH + DECAY_LORA + ICLR_LORA + GATE_LORA
SHIFT_PAD = 1920
QKV_B = 3 * WIDTH
GATES = 2 * D_MODEL
PROJ_PAD = SHIFT_PAD + QKV_B + GATES

CHUNK = 64
TOKEN_TILE = 512
TIME_TILE = 256
VMEM_LIMIT = 56 * 1024 * 1024
MOBA_VMEM_LIMIT = 60 * 1024 * 1024


def _dot(a, b):
    return jnp.dot(a, b, preferred_element_type=F32)


def _dot_nt(a, b):
    return lax.dot_general(a, b, (((1,), (1,)), ((), ())), preferred_element_type=F32)


def _rms(x, g):
    return x * lax.rsqrt(jnp.mean(x * x, axis=-1, keepdims=True) + RMS_EPS) * g


def _proj_kernel(x_ref, g_ref, wsp_ref, wqkv_ref, wgate_ref, mu_ref, sp_ref, qkv_ref, gate_ref,
                 carry_ref, *, tiles_per_seq):
    tm = x_ref.shape[0]
    step = pl.program_id(0)

    @pl.when(step == 0)
    def _():
        carry_ref[...] = jnp.zeros_like(carry_ref)

    h = _rms(x_ref[...], g_ref[...]).astype(BF16)
    qkv_ref[...] = _dot(h, wqkv_ref[...]).astype(BF16)
    gate_ref[...] = _dot(h, wgate_ref[...]).astype(BF16)

    p = _dot(h, wsp_ref[...])
    seq_start = lax.rem(step, tiles_per_seq) == 0
    carry = jnp.where(seq_start, 0.0, carry_ref[0:1, :])
    row = lax.broadcasted_iota(jnp.int32, (tm, 1), 0)
    prev = jnp.where(row == 0, carry, pltpu.roll(p, 1, 0))
    carry_ref[0:1, :] = p[tm - 1:tm, :]
    sp_ref[...] = p + mu_ref[...] * (prev - p)


def _proj(x2d, g, w_sp, w_qkv, w_gate, mu, tm, seq):
    t = x2d.shape[0]
    const = lambda i: (0, 0)
    weight = lambda n: pl.BlockSpec((D_MODEL, n), const, pipeline_mode=pl.Buffered(1))
    return pl.pallas_call(
        functools.partial(_proj_kernel, tiles_per_seq=seq // tm),
        grid=(t // tm,),
        in_specs=[
            pl.BlockSpec((tm, D_MODEL), lambda i: (i, 0)),
            pl.BlockSpec((1, D_MODEL), const),
            weight(SHIFT_PAD), weight(QKV_B), weight(GATES),
            pl.BlockSpec((1, SHIFT_PAD), const),
        ],
        scratch_shapes=[pltpu.VMEM((8, SHIFT_PAD), F32)],
        out_specs=[
            pl.BlockSpec((tm, SHIFT_PAD), lambda i: (i, 0)),
            pl.BlockSpec((tm, QKV_B), lambda i: (i, 0)),
            pl.BlockSpec((tm, GATES), lambda i: (i, 0)),
        ],
        out_shape=[
            jax.ShapeDtypeStruct((t, SHIFT_PAD), F32),
            jax.ShapeDtypeStruct((t, QKV_B), BF16),
            jax.ShapeDtypeStruct((t, GATES), BF16),
        ],
        compiler_params=pltpu.CompilerParams(
            dimension_semantics=("arbitrary",), vmem_limit_bytes=VMEM_LIMIT),
        name="proj",
    )(x2d, g, w_sp, w_qkv, w_gate, mu)


def _split2(x):
    hi = x.astype(BF16)
    lo = (x - hi.astype(F32)).astype(BF16)
    return hi, lo


SEG_LANES = 2 * LANES


def _segsum(x, seg):
    xb = x.astype(BF16)
    return jnp.concatenate(
        [_dot(xb[:, o:o + SEG_LANES], seg) for o in range(0, x.shape[1], SEG_LANES)], axis=1)


N_STREAMS = 2


def _rwkv_kernel(sp_ref, w0_ref, wda_ref, a0_ref, wg_ref, kk_ref, ka_ref, rk_ref,
                 lnw_ref, lnb_ref, seg_ref, tri_ref, o_ref, h_ref, *, tt):
    @pl.when(pl.program_id(1) == 0)
    def _():
        h_ref[...] = jnp.zeros_like(h_ref)

    params = (w0_ref, wda_ref, a0_ref, wg_ref, kk_ref, ka_ref, rk_ref, lnw_ref, lnb_ref,
              seg_ref, tri_ref)
    live = [_rwkv_stream(k, sp_ref, o_ref, h_ref, *params, tt=tt) for k in range(N_STREAMS)]
    done = object()
    while live:
        live = [gen for gen in live if next(gen, done) is not done]


def _rwkv_stream(k_seq, sp_ref, o_ref, h_ref, w0_ref, wda_ref, a0_ref, wg_ref, kk_ref, ka_ref,
                 rk_ref, lnw_ref, lnb_ref, seg_ref, tri_ref, *, tt):
    n_chunks = tt // CHUNK
    xs = sp_ref[k_seq, 0]
    r = xs[:, 0:WIDTH]
    k = xs[:, WIDTH:2 * WIDTH]
    v = xs[:, 2 * WIDTH:3 * WIDTH]
    da = xs[:, 3 * WIDTH:3 * WIDTH + LANES]
    gd = xs[:, 3 * WIDTH + LANES:SHIFT_PAD]
    lane = lax.broadcasted_iota(jnp.int32, (1, LANES), 1)
    lo_half = lane < HEAD_DIM
    da_act = jnp.where(lo_half, jnp.tanh(da), da).astype(BF16)
    lora = _dot(da_act, wda_ref[...])
    g = _dot(jax.nn.sigmoid(gd).astype(BF16), wg_ref[...])

    logdec = -EXP_NEG_HALF * jax.nn.sigmoid(w0_ref[...] + lora[:, 0:WIDTH])
    a = jax.nn.sigmoid(a0_ref[...] + lora[:, WIDTH:2 * WIDTH])

    seg = seg_ref[...]
    kkf = k * kk_ref[...]
    kk_sq = _segsum(kkf * kkf, seg)
    yield
    kk = kkf * lax.rsqrt(jnp.maximum(kk_sq, 1e-24))
    k2 = k * (1.0 + (a - 1.0) * ka_ref[...])
    bonus = _segsum(r * k2 * rk_ref[...], seg) * v

    l1 = logdec.astype(BF16)
    r1 = logdec - l1.astype(F32)
    l2 = r1.astype(BF16)
    l3 = (r1 - l2.astype(F32)).astype(BF16)
    tri = tri_ref[...]
    cum = _dot(tri, l1) + _dot(tri, l2) + _dot(tri, l3)
    yield

    kka = kk * a
    p_inv = jnp.exp(-cum)
    a_t = -kk * jnp.exp(cum - logdec)
    r_t = r * jnp.exp(cum)
    b_t = kka * p_inv
    k_t = k2 * p_inv

    r64 = lax.broadcasted_iota(jnp.int32, (CHUNK, LANES), 0)
    c64 = lax.broadcasted_iota(jnp.int32, (CHUNK, LANES), 1) & (CHUNK - 1)
    strict = c64 < r64
    incl = c64 <= r64
    eye_side = (c64 == r64).astype(F32)
    r128 = lax.broadcasted_iota(jnp.int32, (LANES, LANES), 0)
    c128 = lax.broadcasted_iota(jnp.int32, (LANES, LANES), 1)
    eye128 = (r128 == c128).astype(F32)
    same_head = (r128 >> 6) == (c128 >> 6)
    hi_half = jnp.logical_not(lo_half)
    zeros_c = jnp.zeros((CHUNK, LANES), F32)

    def _stack2(zz):
        zb = zz.astype(BF16)
        return jnp.concatenate([jnp.where(lo_half, zb, 0), jnp.where(hi_half, zb, 0)], axis=0)

    chains = [(c, pr) for c in range(n_chunks) for pr in range(N_PAIRS)]

    def _piece(arr, c, pr):
        return arr[c * CHUNK:(c + 1) * CHUNK, pr * LANES:(pr + 1) * LANES]

    x_ab, x_ak, x_rb, x_rk = {}, {}, {}, {}
    for ch in chains:
        lq = jnp.concatenate([_piece(a_t, *ch), _piece(r_t, *ch)], axis=0).astype(BF16)
        rhs = jnp.concatenate([_stack2(_piece(b_t, *ch)), _stack2(_piece(k_t, *ch))], axis=0)
        sc = _dot_nt(lq, rhs)
        x_ab[ch] = jnp.where(strict, sc[0:CHUNK, 0:LANES], 0.0)
        x_ak[ch] = jnp.where(strict, sc[0:CHUNK, LANES:2 * LANES], 0.0)
        x_rb[ch] = jnp.where(incl, sc[CHUNK:2 * CHUNK, 0:LANES], 0.0)
        x_rk[ch] = jnp.where(incl, sc[CHUNK:2 * CHUNK, LANES:2 * LANES], 0.0)
    yield

    x_t = {ch: eye_side + x_ab[ch] for ch in chains}
    x_pw = {ch: _dot(x_ab[ch].astype(BF16), _stack2(x_ab[ch])) for ch in chains}
    av = {ch: _dot(x_ak[ch].astype(BF16), _stack2(_piece(v, *ch))) for ch in chains}
    yield
    for _ in range(4):
        for ch in chains:
            both = jnp.concatenate([x_t[ch], x_pw[ch]], axis=0).astype(BF16)
            out = _dot(both, _stack2(x_pw[ch]))
            x_t[ch] = x_t[ch] + out[0:CHUNK]
            x_pw[ch] = out[CHUNK:2 * CHUNK]
        yield
    for ch in chains:
        x_t[ch] = x_t[ch] + _dot(x_t[ch].astype(BF16), _stack2(x_pw[ch]))
    yield

    tw = {}
    for ch in chains:
        rhs = jnp.concatenate([_stack2(_piece(a_t, *ch)), _stack2(av[ch])], axis=1)
        tw[ch] = _dot(x_t[ch].astype(BF16), rhs)
    yield
    qeff, yloc, phi, psi = {}, {}, {}, {}
    for ch in chains:
        ta, w_loc, v_p = tw[ch][:, 0:LANES], tw[ch][:, LANES:2 * LANES], _piece(v, *ch)
        rhs = jnp.concatenate(
            [jnp.concatenate([_stack2(ta), _stack2(w_loc)], axis=1),
             jnp.concatenate([jnp.zeros((LANES, LANES), BF16), _stack2(v_p)], axis=1)], axis=0)
        ag = _dot(jnp.concatenate([x_rb[ch], x_rk[ch]], axis=1).astype(BF16), rhs)
        qeff[ch] = _piece(r_t, *ch) + ag[:, 0:LANES]
        yloc[ch] = ag[:, LANES:2 * LANES]
        cum_c = _piece(cum, *ch)
        cum_last = cum_c[CHUNK - 1:CHUNK, :]
        to_end = jnp.exp(cum_last - cum_c)
        kk_t = jnp.concatenate([_piece(kka, *ch) * to_end, _piece(k2, *ch) * to_end], axis=0).T
        rhs = jnp.concatenate(
            [tw[ch], jnp.concatenate([zeros_c, v_p], axis=1)], axis=0).astype(BF16)
        pp = _dot(kk_t.astype(BF16), rhs)
        phi[ch] = eye128 * jnp.exp(cum_last) + jnp.where(same_head, pp[:, 0:LANES], 0.0)
        psi[ch] = jnp.where(same_head, pp[:, LANES:2 * LANES], 0.0)
    yield

    y_rows = []
    states = [h_ref[k_seq, pr] for pr in range(N_PAIRS)]
    for c in range(n_chunks):
        y_c = []
        for pr in range(N_PAIRS):
            ch = (c, pr)
            lhs = jnp.concatenate([qeff[ch], phi[ch]], axis=0).astype(BF16)
            out = _dot(lhs, states[pr].astype(BF16))
            y_c.append(out[0:CHUNK] + yloc[ch])
            states[pr] = out[CHUNK:CHUNK + LANES] + psi[ch]
        y_rows.append(jnp.concatenate(y_c, axis=1))
        yield
    for pr in range(N_PAIRS):
        h_ref[k_seq, pr] = states[pr]
    y = jnp.concatenate(y_rows, axis=0)

    mean = _segsum(y, seg) * (1.0 / HEAD_DIM)
    yield
    d = y - mean
    var = _segsum(d * d, seg) * (1.0 / HEAD_DIM)
    yield
    yn = d * lax.rsqrt(var + GN_EPS) * lnw_ref[...] + lnb_ref[...]
    o_ref[k_seq, 0] = ((yn + bonus) * g).astype(BF16)


def _rwkv(sp, w0, wda, a0, wg, k_k, k_a, r_k, ln_w, ln_b, seg, tri, tt):
    b, s, _ = sp.shape
    assert b % N_STREAMS == 0
    rows = b // N_STREAMS
    const2 = lambda i, j: (0, 0)
    row = lambda n: pl.BlockSpec((1, n), const2)
    out = pl.pallas_call(
        functools.partial(_rwkv_kernel, tt=tt),
        grid=(rows, s // tt),
        in_specs=[
            pl.BlockSpec((N_STREAMS, 1, tt, SHIFT_PAD), lambda i, j: (0, i, j, 0)),
            row(WIDTH),
            pl.BlockSpec((LANES, 2 * WIDTH), const2),
            row(WIDTH),
            pl.BlockSpec((2 * LANES, WIDTH), const2),
            row(WIDTH), row(WIDTH), row(WIDTH), row(WIDTH), row(WIDTH),
            pl.BlockSpec((SEG_LANES, SEG_LANES), const2),
            pl.BlockSpec((tt, tt), const2),
        ],
        out_specs=pl.BlockSpec((N_STREAMS, 1, tt, WIDTH), lambda i, j: (0, i, j, 0)),
        out_shape=jax.ShapeDtypeStruct((N_STREAMS, rows, s, WIDTH), BF16),
        scratch_shapes=[
            pltpu.VMEM((N_STREAMS, N_PAIRS, LANES, LANES), F32),
        ],
        compiler_params=pltpu.CompilerParams(
            dimension_semantics=("arbitrary", "arbitrary"), vmem_limit_bytes=VMEM_LIMIT),
        name="rwkv",
    )(sp.reshape(N_STREAMS, rows, s, SHIFT_PAD), w0, wda, a0, wg, k_k, k_a, r_k, ln_w, ln_b,
      seg, tri)
    return out.reshape(b, s, WIDTH)


NEG_BIG = -(2.0 ** 30)
N_BLOCKS_MAX = 16
V_ROWS = HEAD_DIM + 16
LOG2E = 1.4426950408889634


def _aligned(i, m):
    return i * m if isinstance(i, int) else pl.multiple_of(i * m, m)


def _moba_kernel(q_ref, k_ref, v_ref, slope_ref, o_ref, kx_ref, vt_ref, qt_ref, bias_ref, *, seq):
    nb = seq // MOBA_BLOCK
    bs = MOBA_BLOCK
    scale = HEAD_DIM ** -0.5
    heads = range(N_HEADS)

    rr = lax.broadcasted_iota(jnp.int32, (bs, LANES), 0)
    ll = lax.broadcasted_iota(jnp.int32, (bs, LANES), 1)
    ones_rows = jnp.ones((V_ROWS - HEAD_DIM, bs), BF16)
    for n in range(nb):
        aug = jnp.where(ll < 4, 1.0,
              jnp.where(ll < 6, rr.astype(F32),
              jnp.where(ll < 8, float(n * bs),
              jnp.where(ll == 8 + n, 1.0, 0.0)))).astype(BF16)
        rows = slice(n * bs, (n + 1) * bs)
        for pr in range(N_PAIRS):
            ln = slice(pr * LANES, (pr + 1) * LANES)
            kx_ref[pr, n, :, 0:LANES] = k_ref[0, rows, ln]
            kx_ref[pr, n, :, LANES:2 * LANES] = aug
            v_t = v_ref[0, rows, ln].astype(F32).T.astype(BF16)
            for half in range(2):
                vt_ref[2 * pr + half, n] = jnp.concatenate(
                    [v_t[half * HEAD_DIM:(half + 1) * HEAD_DIM], ones_rows], axis=0)

    bi = lax.broadcasted_iota(jnp.int32, (N_BLOCKS_MAX, seq), 0)
    si = lax.broadcasted_iota(jnp.int32, (N_BLOCKS_MAX, seq), 1)
    avg = jnp.where((si >= bi * bs) & (si < (bi + 1) * bs), 1.0 / bs, 0.0).astype(BF16)
    kmean = _dot(avg, k_ref[0])
    km = jnp.concatenate([jnp.broadcast_to(kmean[n:n + 1, :], (N_HEADS, WIDTH))
                          for n in range(N_BLOCKS_MAX)], axis=0)
    row_head = lax.broadcasted_iota(jnp.int32, (LANES, WIDTH), 0) & (N_HEADS - 1)
    lane_head = lax.broadcasted_iota(jnp.int32, (LANES, WIDTH), 1) >> 6
    km_hi, km_lo = _split2(jnp.where(row_head == lane_head, km, 0.0))

    pr_i = lax.broadcasted_iota(jnp.int32, (LANES, LANES), 0)
    pc_i = lax.broadcasted_iota(jnp.int32, (LANES, LANES), 1)
    perm = (pc_i == ((pr_i & (N_BLOCKS_MAX - 1)) << 3) + (pr_i >> 4)).astype(BF16)
    zeros_cnt = jnp.zeros((N_HEADS, bs), F32)

    for i in range(nb):
        q_blk = q_ref[0, i * bs:(i + 1) * bs, :].astype(F32)
        q_t = [q_blk[:, pr * LANES:(pr + 1) * LANES].T for pr in range(N_PAIRS)]
        for pr in range(N_PAIRS):
            qt_ref[pr, i] = q_t[pr].astype(BF16)
        if i <= MOBA_TOP_K:
            bias_ref[i] = jnp.zeros((N_BLOCKS_MAX * N_HEADS, bs), F32)
            continue
        q_tb = jnp.concatenate(q_t, axis=0).astype(BF16)
        gate = _dot(km_hi, q_tb) + _dot(km_lo, q_tb)
        slabs = [gate[n * N_HEADS:(n + 1) * N_HEADS] for n in range(i)]
        cnt = [zeros_cnt] * i
        for n in range(i):
            for m in range(n):
                m_wins = slabs[m] >= slabs[n]
                cnt[n] = cnt[n] + jnp.where(m_wins, 1.0, 0.0)
                cnt[m] = cnt[m] + jnp.where(m_wins, 0.0, 1.0)
        bias = [jnp.where(cnt[n] < MOBA_TOP_K, 0.0, NEG_BIG) for n in range(i)]
        bias += [zeros_cnt] * (N_BLOCKS_MAX - i)
        bias_ref[i] = _dot(perm, jnp.concatenate(bias, axis=0).astype(BF16))

    r128 = lax.broadcasted_iota(jnp.int32, (LANES, 1), 0)
    row_masks = (r128 < HEAD_DIM, r128 >= HEAD_DIM)
    r8 = lax.broadcasted_iota(jnp.int32, (8, bs), 0)
    tl = lax.broadcasted_iota(jnp.int32, (8, bs), 1).astype(F32)
    key_row = lax.broadcasted_iota(jnp.int32, (bs, bs), 0)
    qry_col = lax.broadcasted_iota(jnp.int32, (bs, bs), 1)
    causal = key_row <= qry_col
    zeros_pad = jnp.zeros((LANES - 8 - N_BLOCKS_MAX, bs), F32)

    def q_block(i, odd):
        start = _aligned(i, bs)
        q_t = [qt_ref[pr, i].astype(F32) for pr in range(N_PAIRS)]
        bias = bias_ref[i]
        t0 = jnp.asarray(i * bs, F32)

        q_aug = []
        for h in heads:
            c = slope_ref[h:h + 1, :] * LOG2E
            c_hi = c.astype(BF16).astype(F32)
            ctl = c * tl
            ctl_hi = ctl.astype(BF16).astype(F32)
            ct0 = c * t0
            ct0_hi = ct0.astype(BF16).astype(F32)
            aug8 = jnp.where(r8 == 0, -ctl_hi,
                   jnp.where(r8 == 1, ctl_hi - ctl,
                   jnp.where(r8 == 2, -ct0_hi,
                   jnp.where(r8 == 3, ct0_hi - ct0,
                   jnp.where((r8 == 4) | (r8 == 6), c_hi, c - c_hi)))))
            top = jnp.where(row_masks[h % 2], q_t[h // 2], 0.0) * (scale * LOG2E)
            sel = bias[h * N_BLOCKS_MAX:(h + 1) * N_BLOCKS_MAX]
            q_aug.append(jnp.concatenate([top, aug8, sel, zeros_pad], axis=0).astype(BF16))

        def scores(j):
            kx = [kx_ref[pr, j] for pr in range(N_PAIRS)]
            return [_dot(kx[h // 2], q_aug[h]) for h in heads]

        def weighted_values(j, ps):
            return [_dot(vt_ref[h, j], ps[h]) for h in heads]

        s_own = scores(i)
        yield
        m0, p0 = [], []
        for h in heads:
            sh = jnp.where(causal, s_own[h], NEG_BIG)
            m0.append(jnp.max(sh, axis=0, keepdims=True))
            p0.append(jnp.exp2(sh - m0[h]).astype(BF16))
        pv = weighted_values(i, p0)
        yield
        acc0 = [pv[h][0:HEAD_DIM] for h in heads]
        l0 = [pv[h][HEAD_DIM:HEAD_DIM + 1] for h in heads]

        def step(js, st):
            m, l, acc = (list(x) for x in st)
            s_all = [scores(j) for j in js]
            ps, alphas = [], []
            for s in s_all:
                p_blk, a_blk = [], []
                for h in heads:
                    mh = jnp.maximum(m[h], jnp.max(s[h], axis=0, keepdims=True))
                    a_blk.append(jnp.exp2(m[h] - mh))
                    p_blk.append(jnp.exp2(s[h] - mh).astype(BF16))
                    m[h] = mh
                ps.append(p_blk)
                alphas.append(a_blk)
            pvs = [weighted_values(j, p_blk) for j, p_blk in zip(js, ps)]
            for a_blk, pv in zip(alphas, pvs):
                for h in heads:
                    acc[h] = a_blk[h] * acc[h] + pv[h][0:HEAD_DIM]
                    l[h] = a_blk[h] * l[h] + pv[h][HEAD_DIM:HEAD_DIM + 1]
            return tuple(tuple(x) for x in (m, l, acc))

        st = (tuple(m0), tuple(l0), tuple(acc0))
        if odd:
            st = step([i - 1], st)
        _, l, acc = lax.fori_loop(0, i >> 1, lambda t, st: step([2 * t, 2 * t + 1], st), st)
        pairs = []
        for pr in range(N_PAIRS):
            a, b = 2 * pr, 2 * pr + 1
            out_t = jnp.concatenate([acc[a] / l[a], acc[b] / l[b]], axis=0)
            pairs.append(out_t.T)
        o_ref[0, pl.ds(start, bs), :] = jnp.concatenate(pairs, axis=1).astype(BF16)

    def q_block_pair(u, carry):
        live = [q_block(2 * u, False), q_block(2 * u + 1, True)]
        done = object()
        while live:
            live = [gen for gen in live if next(gen, done) is not done]
        return carry

    assert nb % 2 == 0
    lax.fori_loop(0, nb // 2, q_block_pair, 0)


def _moba(qkv, slopes):
    b, s, _ = qkv.shape
    assert s % MOBA_BLOCK == 0 and s // MOBA_BLOCK <= N_BLOCKS_MAX
    nb = s // MOBA_BLOCK
    col = lambda c: pl.BlockSpec((1, s, WIDTH), lambda i: (i, 0, c))
    return pl.pallas_call(
        functools.partial(_moba_kernel, seq=s),
        grid=(b,),
        in_specs=[col(0), col(1), col(2), pl.BlockSpec((N_HEADS, MOBA_BLOCK), lambda i: (0, 0))],
        out_specs=pl.BlockSpec((1, s, WIDTH), lambda i: (i, 0, 0)),
        out_shape=jax.ShapeDtypeStruct((b, s, WIDTH), BF16),
        scratch_shapes=[
            pltpu.VMEM((N_PAIRS, nb, MOBA_BLOCK, 2 * LANES), BF16),
            pltpu.VMEM((N_HEADS, nb, V_ROWS, MOBA_BLOCK), BF16),
            pltpu.VMEM((N_PAIRS, nb, LANES, MOBA_BLOCK), BF16),
            pltpu.VMEM((nb, N_BLOCKS_MAX * N_HEADS, MOBA_BLOCK), F32),
        ],
        compiler_params=pltpu.CompilerParams(
            dimension_semantics=("arbitrary",), vmem_limit_bytes=MOBA_VMEM_LIMIT),
        name="moba",
    )(qkv, qkv, qkv, slopes)


FFN_CHUNKS = ((0, 1024), (1024, 1024), (2048, 768))


def _post_kernel(x_ref, ya_ref, yb_ref, gate_ref, wua_ref, wub_ref, wo_ref, gffn_ref,
                 wfi_ref, wfo_ref, gfin_ref, o_ref):
    y_a = _dot(ya_ref[...], wua_ref[...])
    y_b = _dot(yb_ref[...], wub_ref[...])
    ga = jax.nn.sigmoid(gate_ref[:, 0:D_MODEL].astype(F32))
    gb = jax.nn.sigmoid(gate_ref[:, D_MODEL:2 * D_MODEL].astype(F32))
    mixed = (ga * y_a + gb * y_b).astype(BF16)
    x1 = x_ref[...] + _dot(mixed, wo_ref[...])
    h = _rms(x1, gffn_ref[...]).astype(BF16)
    acc = x1
    for off, n in FFN_CHUNKS:
        gg = _dot(h, wfi_ref[:, off:off + n])
        uu = _dot(h, wfi_ref[:, FFN_HIDDEN + off:FFN_HIDDEN + off + n])
        act = (gg * jax.nn.sigmoid(gg) * uu).astype(BF16)
        acc = acc + _dot(act, wfo_ref[off:off + n, :])
    o_ref[...] = _rms(acc, gfin_ref[...])


def _post(x2d, ya, yb, gates, wua, wub, wo, gffn, wfi, wfo, gfin, tm):
    t = x2d.shape[0]
    const = lambda i: (0, 0)
    tile = lambda n: pl.BlockSpec((tm, n), lambda i: (i, 0))
    weight = lambda a: pl.BlockSpec(a.shape, const, pipeline_mode=pl.Buffered(1))
    return pl.pallas_call(
        _post_kernel,
        grid=(t // tm,),
        in_specs=[tile(D_MODEL), tile(WIDTH), tile(WIDTH), tile(GATES),
                  weight(wua), weight(wub), weight(wo), weight(gffn),
                  weight(wfi), weight(wfo), weight(gfin)],
        out_specs=tile(D_MODEL),
        out_shape=jax.ShapeDtypeStruct((t, D_MODEL), F32),
        compiler_params=pltpu.CompilerParams(
            dimension_semantics=("arbitrary",), vmem_limit_bytes=VMEM_LIMIT),
        name="post",
    )(x2d, ya, yb, gates, wua, wub, wo, gffn, wfi, wfo, gfin)


def _layer(x, norm_mix_g, w_in, mu_shift, w0, w_decay_up, a0, w_iclr_up, w_gate_up,
           k_k, k_a, r_k, ln_x_w, ln_x_b, w_up_a, w_up_b, w_o, norm_ffn_g,
           w_ffn_in, w_ffn_out, out_g):
    b, s, _ = x.shape
    t = b * s
    x2d = x.reshape(t, D_MODEL)

    pad = SHIFT_PAD - SHIFT_WIDTH
    w_sp = jnp.pad(w_in[:, :SHIFT_WIDTH].astype(BF16), ((0, 0), (0, pad)))
    w_qkv = w_in[:, SHIFT_WIDTH:SHIFT_WIDTH + QKV_B].astype(BF16)
    w_gate = w_in[:, SHIFT_WIDTH + QKV_B:].astype(BF16)
    mu = jnp.pad(mu_shift, (0, pad)).reshape(1, SHIFT_PAD)
    wda = jnp.zeros((LANES, 2 * WIDTH), F32)
    wda = wda.at[:DECAY_LORA, :WIDTH].set(w_decay_up).at[DECAY_LORA:, WIDTH:].set(w_iclr_up)
    wg = jnp.pad(w_gate_up, ((0, 2 * LANES - GATE_LORA), (0, 0)))
    row = lambda a: a.reshape(1, -1)

    tt = TIME_TILE
    hid = jnp.arange(SEG_LANES) // HEAD_DIM
    seg = (hid[:, None] == hid[None, :]).astype(BF16)
    ti = jnp.arange(tt)
    tri = ((ti[:, None] >= ti[None, :]) & (ti[:, None] // CHUNK == ti[None, :] // CHUNK)).astype(BF16)

    sp, qkv, gates = _proj(x2d, row(norm_mix_g), w_sp, w_qkv, w_gate, mu, tm=TOKEN_TILE,
                            seq=s)
    ya = _rwkv(sp.reshape(b, s, SHIFT_PAD), row(w0), wda.astype(BF16), row(a0),
               wg.astype(BF16), row(k_k), row(k_a), row(r_k), row(ln_x_w), row(ln_x_b),
               seg, tri, tt)
    slopes = jnp.exp2(-8.0 * jnp.arange(1, N_HEADS + 1, dtype=F32) / N_HEADS)
    slopes = jnp.broadcast_to(slopes[:, None], (N_HEADS, MOBA_BLOCK))
    yb = _moba(qkv.reshape(b, s, QKV_B), slopes)
    out = _post(x2d, ya.reshape(t, WIDTH), yb.reshape(t, WIDTH), gates,
                w_up_a.astype(BF16), w_up_b.astype(BF16), w_o.astype(BF16), row(norm_ffn_g),
                w_ffn_in.astype(BF16), w_ffn_out.astype(BF16), row(out_g), tm=TOKEN_TILE)
    return out.reshape(b, s, D_MODEL)


def kernel(x, norm_mix_g, w_in, mu_shift, w0, w_decay_up, a0, w_iclr_up, w_gate_up, k_k, k_a, r_k, ln_x_w, ln_x_b, w_up_a, w_up_b, w_o, norm_ffn_g, w_ffn_in, w_ffn_out, norm_final_g):
    depth = w_in.shape[0]
    assert depth == 1, "the fused post kernel applies the final norm after the only layer"
    return _layer(x, norm_mix_g[0], w_in[0], mu_shift[0], w0[0], w_decay_up[0], a0[0],
                  w_iclr_up[0], w_gate_up[0], k_k[0], k_a[0], r_k[0].reshape(-1), ln_x_w[0],
                  ln_x_b[0], w_up_a[0], w_up_b[0], w_o[0], norm_ffn_g[0], w_ffn_in[0],
                  w_ffn_out[0], norm_final_g)
```

```python
import functools

import jax
import jax.numpy as jnp
from jax import lax
from jax.experimental import pallas as pl
from jax.experimental.pallas import tpu as pltpu

F32 = jnp.float32
BF16 = jnp.bfloat16

D_MODEL = 1024
HEAD_DIM = 64
N_HEADS = 8
WIDTH = N_HEADS * HEAD_DIM
N_PAIRS = N_HEADS // 2
LANES = 128
DECAY_LORA = 64
ICLR_LORA = 64
GATE_LORA = 160
GN_EPS = 64e-5
EXP_NEG_HALF = 0.6065306597126334
RMS_EPS = 1e-6
MOBA_BLOCK = 256
MOBA_TOP_K = 3
FFN_HIDDEN = 2816
SHIFT_WIDTH = 3 * WIDTH + DECAY_LORA + ICLR_LORA + GATE_LORA
SHIFT_PAD = 1920
QKV_B = 3 * WIDTH
GATES = 2 * D_MODEL
PROJ_PAD = SHIFT_PAD + QKV_B + GATES

CHUNK = 64
TOKEN_TILE = 512
TIME_TILE = 256
VMEM_LIMIT = 56 * 1024 * 1024
MOBA_VMEM_LIMIT = 60 * 1024 * 1024


def _dot(a, b):
    return jnp.dot(a, b, preferred_element_type=F32)


def _dot_nt(a, b):
    return lax.dot_general(a, b, (((1,), (1,)), ((), ())), preferred_element_type=F32)


def _rms(x, g):
    return x * lax.rsqrt(jnp.mean(x * x, axis=-1, keepdims=True) + RMS_EPS) * g


def _proj_kernel(x_ref, g_ref, wsp_ref, wqkv_ref, wgate_ref, mu_ref, sp_ref, qkv_ref, gate_ref,
                 carry_ref, *, tiles_per_seq):
    tm = x_ref.shape[0]
    step = pl.program_id(0)

    @pl.when(step == 0)
    def _():
        carry_ref[...] = jnp.zeros_like(carry_ref)

    h = _rms(x_ref[...], g_ref[...]).astype(BF16)
    qkv_ref[...] = _dot(h, wqkv_ref[...]).astype(BF16)
    gate_ref[...] = _dot(h, wgate_ref[...]).astype(BF16)

    p = _dot(h, wsp_ref[...])
    seq_start = lax.rem(step, tiles_per_seq) == 0
    carry = jnp.where(seq_start, 0.0, carry_ref[0:1, :])
    row = lax.broadcasted_iota(jnp.int32, (tm, 1), 0)
    prev = jnp.where(row == 0, carry, pltpu.roll(p, 1, 0))
    carry_ref[0:1, :] = p[tm - 1:tm, :]
    sp_ref[...] = p + mu_ref[...] * (prev - p)


def _proj(x2d, g, w_sp, w_qkv, w_gate, mu, tm, seq):
    t = x2d.shape[0]
    const = lambda i: (0, 0)
    weight = lambda n: pl.BlockSpec((D_MODEL, n), const, pipeline_mode=pl.Buffered(1))
    return pl.pallas_call(
        functools.partial(_proj_kernel, tiles_per_seq=seq // tm),
        grid=(t // tm,),
        in_specs=[
            pl.BlockSpec((tm, D_MODEL), lambda i: (i, 0)),
            pl.BlockSpec((1, D_MODEL), const),
            weight(SHIFT_PAD), weight(QKV_B), weight(GATES),
            pl.BlockSpec((1, SHIFT_PAD), const),
        ],
        scratch_shapes=[pltpu.VMEM((8, SHIFT_PAD), F32)],
        out_specs=[
            pl.BlockSpec((tm, SHIFT_PAD), lambda i: (i, 0)),
            pl.BlockSpec((tm, QKV_B), lambda i: (i, 0)),
            pl.BlockSpec((tm, GATES), lambda i: (i, 0)),
        ],
        out_shape=[
            jax.ShapeDtypeStruct((t, SHIFT_PAD), F32),
            jax.ShapeDtypeStruct((t, QKV_B), BF16),
            jax.ShapeDtypeStruct((t, GATES), BF16),
        ],
        compiler_params=pltpu.CompilerParams(
            dimension_semantics=("arbitrary",), vmem_limit_bytes=VMEM_LIMIT),
        name="proj",
    )(x2d, g, w_sp, w_qkv, w_gate, mu)


def _split2(x):
    hi = x.astype(BF16)
    lo = (x - hi.astype(F32)).astype(BF16)
    return hi, lo


SEG_LANES = 2 * LANES


def _segsum(x, seg):
    xb = x.astype(BF16)
    return jnp.concatenate(
        [_dot(xb[:, o:o + SEG_LANES], seg) for o in range(0, x.shape[1], SEG_LANES)], axis=1)


N_STREAMS = 2


def _rwkv_kernel(sp_ref, w0_ref, wda_ref, a0_ref, wg_ref, kk_ref, ka_ref, rk_ref,
                 lnw_ref, lnb_ref, seg_ref, tri_ref, o_ref, h_ref, *, tt):
    @pl.when(pl.program_id(1) == 0)
    def _():
        h_ref[...] = jnp.zeros_like(h_ref)

    params = (w0_ref, wda_ref, a0_ref, wg_ref, kk_ref, ka_ref, rk_ref, lnw_ref, lnb_ref,
              seg_ref, tri_ref)
    live = [_rwkv_stream(k, sp_ref, o_ref, h_ref, *params, tt=tt) for k in range(N_STREAMS)]
    done = object()
    while live:
        live = [gen for gen in live if next(gen, done) is not done]


def _rwkv_stream(k_seq, sp_ref, o_ref, h_ref, w0_ref, wda_ref, a0_ref, wg_ref, kk_ref, ka_ref,
                 rk_ref, lnw_ref, lnb_ref, seg_ref, tri_ref, *, tt):
    n_chunks = tt // CHUNK
    xs = sp_ref[k_seq, 0]
    r = xs[:, 0:WIDTH]
    k = xs[:, WIDTH:2 * WIDTH]
    v = xs[:, 2 * WIDTH:3 * WIDTH]
    da = xs[:, 3 * WIDTH:3 * WIDTH + LANES]
    gd = xs[:, 3 * WIDTH + LANES:SHIFT_PAD]
    lane = lax.broadcasted_iota(jnp.int32, (1, LANES), 1)
    lo_half = lane < HEAD_DIM
    da_act = jnp.where(lo_half, jnp.tanh(da), da).astype(BF16)
    lora = _dot(da_act, wda_ref[...])
    g = _dot(jax.nn.sigmoid(gd).astype(BF16), wg_ref[...])

    logdec = -EXP_NEG_HALF * jax.nn.sigmoid(w0_ref[...] + lora[:, 0:WIDTH])
    a = jax.nn.sigmoid(a0_ref[...] + lora[:, WIDTH:2 * WIDTH])

    seg = seg_ref[...]
    kkf = k * kk_ref[...]
    kk_sq = _segsum(kkf * kkf, seg)
    yield
    kk = kkf * lax.rsqrt(jnp.maximum(kk_sq, 1e-24))
    k2 = k * (1.0 + (a - 1.0) * ka_ref[...])
    bonus = _segsum(r * k2 * rk_ref[...], seg) * v

    l1 = logdec.astype(BF16)
    r1 = logdec - l1.astype(F32)
    l2 = r1.astype(BF16)
    l3 = (r1 - l2.astype(F32)).astype(BF16)
    tri = tri_ref[...]
    cum = _dot(tri, l1) + _dot(tri, l2) + _dot(tri, l3)
    yield

    kka = kk * a
    p_inv = jnp.exp(-cum)
    a_t = -kk * jnp.exp(cum - logdec)
    r_t = r * jnp.exp(cum)
    b_t = kka * p_inv
    k_t = k2 * p_inv

    r64 = lax.broadcasted_iota(jnp.int32, (CHUNK, LANES), 0)
    c64 = lax.broadcasted_iota(jnp.int32, (CHUNK, LANES), 1) & (CHUNK - 1)
    strict = c64 < r64
    incl = c64 <= r64
    eye_side = (c64 == r64).astype(F32)
    r128 = lax.broadcasted_iota(jnp.int32, (LANES, LANES), 0)
    c128 = lax.broadcasted_iota(jnp.int32, (LANES, LANES), 1)
    eye128 = (r128 == c128).astype(F32)
    same_head = (r128 >> 6) == (c128 >> 6)
    hi_half = jnp.logical_not(lo_half)
    zeros_c = jnp.zeros((CHUNK, LANES), F32)

    def _stack2(zz):
        zb = zz.astype(BF16)
        return jnp.concatenate([jnp.where(lo_half, zb, 0), jnp.where(hi_half, zb, 0)], axis=0)

    chains = [(c, pr) for c in range(n_chunks) for pr in range(N_PAIRS)]

    def _piece(arr, c, pr):
        return arr[c * CHUNK:(c + 1) * CHUNK, pr * LANES:(pr + 1) * LANES]

    x_ab, x_ak, x_rb, x_rk = {}, {}, {}, {}
    for ch in chains:
        lq = jnp.concatenate([_piece(a_t, *ch), _piece(r_t, *ch)], axis=0).astype(BF16)
        rhs = jnp.concatenate([_stack2(_piece(b_t, *ch)), _stack2(_piece(k_t, *ch))], axis=0)
        sc = _dot_nt(lq, rhs)
        x_ab[ch] = jnp.where(strict, sc[0:CHUNK, 0:LANES], 0.0)
        x_ak[ch] = jnp.where(strict, sc[0:CHUNK, LANES:2 * LANES], 0.0)
        x_rb[ch] = jnp.where(incl, sc[CHUNK:2 * CHUNK, 0:LANES], 0.0)
        x_rk[ch] = jnp.where(incl, sc[CHUNK:2 * CHUNK, LANES:2 * LANES], 0.0)
    yield

    x_t = {ch: eye_side + x_ab[ch] for ch in chains}
    x_pw = {ch: _dot(x_ab[ch].astype(BF16), _stack2(x_ab[ch])) for ch in chains}
    av = {ch: _dot(x_ak[ch].astype(BF16), _stack2(_piece(v, *ch))) for ch in chains}
    yield
    for _ in range(4):
        for ch in chains:
            both = jnp.concatenate([x_t[ch], x_pw[ch]], axis=0).astype(BF16)
            out = _dot(both, _stack2(x_pw[ch]))
            x_t[ch] = x_t[ch] + out[0:CHUNK]
            x_pw[ch] = out[CHUNK:2 * CHUNK]
        yield
    for ch in chains:
        x_t[ch] = x_t[ch] + _dot(x_t[ch].astype(BF16), _stack2(x_pw[ch]))
    yield

    tw = {}
    for ch in chains:
        rhs = jnp.concatenate([_stack2(_piece(a_t, *ch)), _stack2(av[ch])], axis=1)
        tw[ch] = _dot(x_t[ch].astype(BF16), rhs)
    yield
    qeff, yloc, phi, psi = {}, {}, {}, {}
    for ch in chains:
        ta, w_loc, v_p = tw[ch][:, 0:LANES], tw[ch][:, LANES:2 * LANES], _piece(v, *ch)
        rhs = jnp.concatenate(
            [jnp.concatenate([_stack2(ta), _stack2(w_loc)], axis=1),
             jnp.concatenate([jnp.zeros((LANES, LANES), BF16), _stack2(v_p)], axis=1)], axis=0)
        ag = _dot(jnp.concatenate([x_rb[ch], x_rk[ch]], axis=1).astype(BF16), rhs)
        qeff[ch] = _piece(r_t, *ch) + ag[:, 0:LANES]
        yloc[ch] = ag[:, LANES:2 * LANES]
        cum_c = _piece(cum, *ch)
        cum_last = cum_c[CHUNK - 1:CHUNK, :]
        to_end = jnp.exp(cum_last - cum_c)
        kk_t = jnp.concatenate([_piece(kka, *ch) * to_end, _piece(k2, *ch) * to_end], axis=0).T
        rhs = jnp.concatenate(
            [tw[ch], jnp.concatenate([zeros_c, v_p], axis=1)], axis=0).astype(BF16)
        pp = _dot(kk_t.astype(BF16), rhs)
        phi[ch] = eye128 * jnp.exp(cum_last) + jnp.where(same_head, pp[:, 0:LANES], 0.0)
        psi[ch] = jnp.where(same_head, pp[:, LANES:2 * LANES], 0.0)
    yield

    y_rows = []
    states = [h_ref[k_seq, pr] for pr in range(N_PAIRS)]
    for c in range(n_chunks):
        y_c = []
        for pr in range(N_PAIRS):
            ch = (c, pr)
            lhs = jnp.concatenate([qeff[ch], phi[ch]], axis=0).astype(BF16)
            out = _dot(lhs, states[pr].astype(BF16))
            y_c.append(out[0:CHUNK] + yloc[ch])
            states[pr] = out[CHUNK:CHUNK + LANES] + psi[ch]
        y_rows.append(jnp.concatenate(y_c, axis=1))
        yield
    for pr in range(N_PAIRS):
        h_ref[k_seq, pr] = states[pr]
    y = jnp.concatenate(y_rows, axis=0)

    mean = _segsum(y, seg) * (1.0 / HEAD_DIM)
    yield
    d = y - mean
    var = _segsum(d * d, seg) * (1.0 / HEAD_DIM)
    yield
    yn = d * lax.rsqrt(var + GN_EPS) * lnw_ref[...] + lnb_ref[...]
    o_ref[k_seq, 0] = ((yn + bonus) * g).astype(BF16)


def _rwkv(sp, w0, wda, a0, wg, k_k, k_a, r_k, ln_w, ln_b, seg, tri, tt):
    b, s, _ = sp.shape
    assert b % N_STREAMS == 0
    rows = b // N_STREAMS
    const2 = lambda i, j: (0, 0)
    row = lambda n: pl.BlockSpec((1, n), const2)
    out = pl.pallas_call(
        functools.partial(_rwkv_kernel, tt=tt),
        grid=(rows, s // tt),
        in_specs=[
            pl.BlockSpec((N_STREAMS, 1, tt, SHIFT_PAD), lambda i, j: (0, i, j, 0)),
            row(WIDTH),
            pl.BlockSpec((LANES, 2 * WIDTH), const2),
            row(WIDTH),
            pl.BlockSpec((2 * LANES, WIDTH), const2),
            row(WIDTH), row(WIDTH), row(WIDTH), row(WIDTH), row(WIDTH),
            pl.BlockSpec((SEG_LANES, SEG_LANES), const2),
            pl.BlockSpec((tt, tt), const2),
        ],
        out_specs=pl.BlockSpec((N_STREAMS, 1, tt, WIDTH), lambda i, j: (0, i, j, 0)),
        out_shape=jax.ShapeDtypeStruct((N_STREAMS, rows, s, WIDTH), BF16),
        scratch_shapes=[
            pltpu.VMEM((N_STREAMS, N_PAIRS, LANES, LANES), F32),
        ],
        compiler_params=pltpu.CompilerParams(
            dimension_semantics=("arbitrary", "arbitrary"), vmem_limit_bytes=VMEM_LIMIT),
        name="rwkv",
    )(sp.reshape(N_STREAMS, rows, s, SHIFT_PAD), w0, wda, a0, wg, k_k, k_a, r_k, ln_w, ln_b,
      seg, tri)
    return out.reshape(b, s, WIDTH)


NEG_BIG = -(2.0 ** 30)
N_BLOCKS_MAX = 16
V_ROWS = HEAD_DIM + 16
LOG2E = 1.4426950408889634


def _aligned(i, m):
    return i * m if isinstance(i, int) else pl.multiple_of(i * m, m)


def _moba_kernel(q_ref, k_ref, v_ref, slope_ref, o_ref, kx_ref, vt_ref, qt_ref, bias_ref, *, seq):
    nb = seq // MOBA_BLOCK
    bs = MOBA_BLOCK
    scale = HEAD_DIM ** -0.5
    heads = range(N_HEADS)

    rr = lax.broadcasted_iota(jnp.int32, (bs, LANES), 0)
    ll = lax.broadcasted_iota(jnp.int32, (bs, LANES), 1)
    ones_rows = jnp.ones((V_ROWS - HEAD_DIM, bs), BF16)
    for n in range(nb):
        aug = jnp.where(ll < 4, 1.0,
              jnp.where(ll < 6, rr.astype(F32),
              jnp.where(ll < 8, float(n * bs),
              jnp.where(ll == 8 + n, 1.0, 0.0)))).astype(BF16)
        rows = slice(n * bs, (n + 1) * bs)
        for pr in range(N_PAIRS):
            ln = slice(pr * LANES, (pr + 1) * LANES)
            kx_ref[pr, n, :, 0:LANES] = k_ref[0, rows, ln]
            kx_ref[pr, n, :, LANES:2 * LANES] = aug
            v_t = v_ref[0, rows, ln].astype(F32).T.astype(BF16)
            for half in range(2):
                vt_ref[2 * pr + half, n] = jnp.concatenate(
                    [v_t[half * HEAD_DIM:(half + 1) * HEAD_DIM], ones_rows], axis=0)

    bi = lax.broadcasted_iota(jnp.int32, (N_BLOCKS_MAX, seq), 0)
    si = lax.broadcasted_iota(jnp.int32, (N_BLOCKS_MAX, seq), 1)
    avg = jnp.where((si >= bi * bs) & (si < (bi + 1) * bs), 1.0 / bs, 0.0).astype(BF16)
    kmean = _dot(avg, k_ref[0])
    km = jnp.concatenate([jnp.broadcast_to(kmean[n:n + 1, :], (N_HEADS, WIDTH))
                          for n in range(N_BLOCKS_MAX)], axis=0)
    row_head = lax.broadcasted_iota(jnp.int32, (LANES, WIDTH), 0) & (N_HEADS - 1)
    lane_head = lax.broadcasted_iota(jnp.int32, (LANES, WIDTH), 1) >> 6
    km_hi, km_lo = _split2(jnp.where(row_head == lane_head, km, 0.0))

    pr_i = lax.broadcasted_iota(jnp.int32, (LANES, LANES), 0)
    pc_i = lax.broadcasted_iota(jnp.int32, (LANES, LANES), 1)
    perm = (pc_i == ((pr_i & (N_BLOCKS_MAX - 1)) << 3) + (pr_i >> 4)).astype(BF16)
    zeros_cnt = jnp.zeros((N_HEADS, bs), F32)

    for i in range(nb):
        q_blk = q_ref[0, i * bs:(i + 1) * bs, :].astype(F32)
        q_t = [q_blk[:, pr * LANES:(pr + 1) * LANES].T for pr in range(N_PAIRS)]
        for pr in range(N_PAIRS):
            qt_ref[pr, i] = q_t[pr].astype(BF16)
        if i <= MOBA_TOP_K:
            bias_ref[i] = jnp.zeros((N_BLOCKS_MAX * N_HEADS, bs), F32)
            continue
        q_tb = jnp.concatenate(q_t, axis=0).astype(BF16)
        gate = _dot(km_hi, q_tb) + _dot(km_lo, q_tb)
        slabs = [gate[n * N_HEADS:(n + 1) * N_HEADS] for n in range(i)]
        cnt = [zeros_cnt] * i
        for n in range(i):
            for m in range(n):
                m_wins = slabs[m] >= slabs[n]
                cnt[n] = cnt[n] + jnp.where(m_wins, 1.0, 0.0)
                cnt[m] = cnt[m] + jnp.where(m_wins, 0.0, 1.0)
        bias = [jnp.where(cnt[n] < MOBA_TOP_K, 0.0, NEG_BIG) for n in range(i)]
        bias += [zeros_cnt] * (N_BLOCKS_MAX - i)
        bias_ref[i] = _dot(perm, jnp.concatenate(bias, axis=0).astype(BF16))

    r128 = lax.broadcasted_iota(jnp.int32, (LANES, 1), 0)
    row_masks = (r128 < HEAD_DIM, r128 >= HEAD_DIM)
    r8 = lax.broadcasted_iota(jnp.int32, (8, bs), 0)
    tl = lax.broadcasted_iota(jnp.int32, (8, bs), 1).astype(F32)
    key_row = lax.broadcasted_iota(jnp.int32, (bs, bs), 0)
    qry_col = lax.broadcasted_iota(jnp.int32, (bs, bs), 1)
    causal = key_row <= qry_col
    zeros_pad = jnp.zeros((LANES - 8 - N_BLOCKS_MAX, bs), F32)

    def q_block(i, odd):
        start = _aligned(i, bs)
        q_t = [qt_ref[pr, i].astype(F32) for pr in range(N_PAIRS)]
        bias = bias_ref[i]
        t0 = jnp.asarray(i * bs, F32)

        q_aug = []
        for h in heads:
            c = slope_ref[h:h + 1, :] * LOG2E
            c_hi = c.astype(BF16).astype(F32)
            ctl = c * tl
            ctl_hi = ctl.astype(BF16).astype(F32)
            ct0 = c * t0
            ct0_hi = ct0.astype(BF16).astype(F32)
            aug8 = jnp.where(r8 == 0, -ctl_hi,
                   jnp.where(r8 == 1, ctl_hi - ctl,
                   jnp.where(r8 == 2, -ct0_hi,
                   jnp.where(r8 == 3, ct0_hi - ct0,
                   jnp.where((r8 == 4) | (r8 == 6), c_hi, c - c_hi)))))
            top = jnp.where(row_masks[h % 2], q_t[h // 2], 0.0) * (scale * LOG2E)
            sel = bias[h * N_BLOCKS_MAX:(h + 1) * N_BLOCKS_MAX]
            q_aug.append(jnp.concatenate([top, aug8, sel, zeros_pad], axis=0).astype(BF16))

        def scores(j):
            kx = [kx_ref[pr, j] for pr in range(N_PAIRS)]
            return [_dot(kx[h // 2], q_aug[h]) for h in heads]

        def weighted_values(j, ps):
            return [_dot(vt_ref[h, j], ps[h]) for h in heads]

        s_own = scores(i)
        yield
        m0, p0 = [], []
        for h in heads:
            sh = jnp.where(causal, s_own[h], NEG_BIG)
            m0.append(jnp.max(sh, axis=0, keepdims=True))
            p0.append(jnp.exp2(sh - m0[h]).astype(BF16))
        pv = weighted_values(i, p0)
        yield
        acc0 = [pv[h][0:HEAD_DIM] for h in heads]
        l0 = [pv[h][HEAD_DIM:HEAD_DIM + 1] for h in heads]

        def step(js, st):
            m, l, acc = (list(x) for x in st)
            s_all = [scores(j) for j in js]
            ps, alphas = [], []
            for s in s_all:
                p_blk, a_blk = [], []
                for h in heads:
                    mh = jnp.maximum(m[h], jnp.max(s[h], axis=0, keepdims=True))
                    a_blk.append(jnp.exp2(m[h] - mh))
                    p_blk.append(jnp.exp2(s[h] - mh).astype(BF16))
                    m[h] = mh
                ps.append(p_blk)
                alphas.append(a_blk)
            pvs = [weighted_values(j, p_blk) for j, p_blk in zip(js, ps)]
            for a_blk, pv in zip(alphas, pvs):
                for h in heads:
                    acc[h] = a_blk[h] * acc[h] + pv[h][0:HEAD_DIM]
                    l[h] = a_blk[h] * l[h] + pv[h][HEAD_DIM:HEAD_DIM + 1]
            return tuple(tuple(x) for x in (m, l, acc))

        st = (tuple(m0), tuple(l0), tuple(acc0))
        if odd:
            st = step([i - 1], st)
        _, l, acc = lax.fori_loop(0, i >> 1, lambda t, st: step([2 * t, 2 * t + 1], st), st)
        pairs = []
        for pr in range(N_PAIRS):
            a, b = 2 * pr, 2 * pr + 1
            out_t = jnp.concatenate([acc[a] / l[a], acc[b] / l[b]], axis=0)
            pairs.append(out_t.T)
        o_ref[0, pl.ds(start, bs), :] = jnp.concatenate(pairs, axis=1).astype(BF16)

    def q_block_pair(u, carry):
        live = [q_block(2 * u, False), q_block(2 * u + 1, True)]
        done = object()
        while live:
            live = [gen for gen in live if next(gen, done) is not done]
        return carry

    assert nb % 2 == 0
    lax.fori_loop(0, nb // 2, q_block_pair, 0)


def _moba(qkv, slopes):
    b, s, _ = qkv.shape
    assert s % MOBA_BLOCK == 0 and s // MOBA_BLOCK <= N_BLOCKS_MAX
    nb = s // MOBA_BLOCK
    col = lambda c: pl.BlockSpec((1, s, WIDTH), lambda i: (i, 0, c))
    return pl.pallas_call(
        functools.partial(_moba_kernel, seq=s),
        grid=(b,),
        in_specs=[col(0), col(1), col(2), pl.BlockSpec((N_HEADS, MOBA_BLOCK), lambda i: (0, 0))],
        out_specs=pl.BlockSpec((1, s, WIDTH), lambda i: (i, 0, 0)),
        out_shape=jax.ShapeDtypeStruct((b, s, WIDTH), BF16),
        scratch_shapes=[
            pltpu.VMEM((N_PAIRS, nb, MOBA_BLOCK, 2 * LANES), BF16),
            pltpu.VMEM((N_HEADS, nb, V_ROWS, MOBA_BLOCK), BF16),
            pltpu.VMEM((N_PAIRS, nb, LANES, MOBA_BLOCK), BF16),
            pltpu.VMEM((nb, N_BLOCKS_MAX * N_HEADS, MOBA_BLOCK), F32),
        ],
        compiler_params=pltpu.CompilerParams(
            dimension_semantics=("arbitrary",), vmem_limit_bytes=MOBA_VMEM_LIMIT),
        name="moba",
    )(qkv, qkv, qkv, slopes)


FFN_CHUNKS = ((0, 1024), (1024, 1024), (2048, 768))


POST_STREAMS = 2


def _post_rows(rows, x_ref, ya_ref, yb_ref, gate_ref, wua_ref, wub_ref, wo_ref, gffn_ref,
               wfi_ref, wfo_ref, gfin_ref, o_ref):
    y_a = _dot(ya_ref[rows, :], wua_ref[...])
    y_b = _dot(yb_ref[rows, :], wub_ref[...])
    yield
    ga = jax.nn.sigmoid(gate_ref[rows, 0:D_MODEL].astype(F32))
    gb = jax.nn.sigmoid(gate_ref[rows, D_MODEL:2 * D_MODEL].astype(F32))
    mixed = (ga * y_a + gb * y_b).astype(BF16)
    x1 = x_ref[rows, :] + _dot(mixed, wo_ref[...])
    yield
    h = _rms(x1, gffn_ref[...]).astype(BF16)
    acc = x1
    for off, n in FFN_CHUNKS:
        gg = _dot(h, wfi_ref[:, off:off + n])
        uu = _dot(h, wfi_ref[:, FFN_HIDDEN + off:FFN_HIDDEN + off + n])
        yield
        act = (gg * jax.nn.sigmoid(gg) * uu).astype(BF16)
        acc = acc + _dot(act, wfo_ref[off:off + n, :])
        yield
    o_ref[rows, :] = _rms(acc, gfin_ref[...])


def _post_kernel(*refs):
    tm = refs[0].shape[0]
    part = tm // POST_STREAMS
    live = [_post_rows(slice(k * part, (k + 1) * part), *refs) for k in range(POST_STREAMS)]
    done = object()
    while live:
        live = [gen for gen in live if next(gen, done) is not done]


def _post(x2d, ya, yb, gates, wua, wub, wo, gffn, wfi, wfo, gfin, tm):
    t = x2d.shape[0]
    const = lambda i: (0, 0)
    tile = lambda n: pl.BlockSpec((tm, n), lambda i: (i, 0))
    weight = lambda a: pl.BlockSpec(a.shape, const, pipeline_mode=pl.Buffered(1))
    return pl.pallas_call(
        _post_kernel,
        grid=(t // tm,),
        in_specs=[tile(D_MODEL), tile(WIDTH), tile(WIDTH), tile(GATES),
                  weight(wua), weight(wub), weight(wo), weight(gffn),
                  weight(wfi), weight(wfo), weight(gfin)],
        out_specs=tile(D_MODEL),
        out_shape=jax.ShapeDtypeStruct((t, D_MODEL), F32),
        compiler_params=pltpu.CompilerParams(
            dimension_semantics=("arbitrary",), vmem_limit_bytes=VMEM_LIMIT),
        name="post",
    )(x2d, ya, yb, gates, wua, wub, wo, gffn, wfi, wfo, gfin)


def _layer(x, norm_mix_g, w_in, mu_shift, w0, w_decay_up, a0, w_iclr_up, w_gate_up,
           k_k, k_a, r_k, ln_x_w, ln_x_b, w_up_a, w_up_b, w_o, norm_ffn_g,
           w_ffn_in, w_ffn_out, out_g):
    b, s, _ = x.shape
    t = b * s
    x2d = x.reshape(t, D_MODEL)

    pad = SHIFT_PAD - SHIFT_WIDTH
    w_sp = jnp.pad(w_in[:, :SHIFT_WIDTH].astype(BF16), ((0, 0), (0, pad)))
    w_qkv = w_in[:, SHIFT_WIDTH:SHIFT_WIDTH + QKV_B].astype(BF16)
    w_gate = w_in[:, SHIFT_WIDTH + QKV_B:].astype(BF16)
    mu = jnp.pad(mu_shift, (0, pad)).reshape(1, SHIFT_PAD)
    wda = jnp.zeros((LANES, 2 * WIDTH), F32)
    wda = wda.at[:DECAY_LORA, :WIDTH].set(w_decay_up).at[DECAY_LORA:, WIDTH:].set(w_iclr_up)
    wg = jnp.pad(w_gate_up, ((0, 2 * LANES - GATE_LORA), (0, 0)))
    row = lambda a: a.reshape(1, -1)

    tt = TIME_TILE
    hid = jnp.arange(SEG_LANES) // HEAD_DIM
    seg = (hid[:, None] == hid[None, :]).astype(BF16)
    ti = jnp.arange(tt)
    tri = ((ti[:, None] >= ti[None, :]) & (ti[:, None] // CHUNK == ti[None, :] // CHUNK)).astype(BF16)

    sp, qkv, gates = _proj(x2d, row(norm_mix_g), w_sp, w_qkv, w_gate, mu, tm=TOKEN_TILE,
                            seq=s)
    ya = _rwkv(sp.reshape(b, s, SHIFT_PAD), row(w0), wda.astype(BF16), row(a0),
               wg.astype(BF16), row(k_k), row(k_a), row(r_k), row(ln_x_w), row(ln_x_b),
               seg, tri, tt)
    slopes = jnp.exp2(-8.0 * jnp.arange(1, N_HEADS + 1, dtype=F32) / N_HEADS)
    slopes = jnp.broadcast_to(slopes[:, None], (N_HEADS, MOBA_BLOCK))
    yb = _moba(qkv.reshape(b, s, QKV_B), slopes)
    out = _post(x2d, ya.reshape(t, WIDTH), yb.reshape(t, WIDTH), gates,
                w_up_a.astype(BF16), w_up_b.astype(BF16), w_o.astype(BF16), row(norm_ffn_g),
                w_ffn_in.astype(BF16), w_ffn_out.astype(BF16), row(out_g), tm=TOKEN_TILE)
    return out.reshape(b, s, D_MODEL)


def kernel(x, norm_mix_g, w_in, mu_shift, w0, w_decay_up, a0, w_iclr_up, w_gate_up, k_k, k_a, r_k, ln_x_w, ln_x_b, w_up_a, w_up_b, w_o, norm_ffn_g, w_ffn_in, w_ffn_out, norm_final_g):
    depth = w_in.shape[0]
    assert depth == 1, "the fused post kernel applies the final norm after the only layer"
    return _layer(x, norm_mix_g[0], w_in[0], mu_shift[0], w0[0], w_decay_up[0], a0[0],
                  w_iclr_up[0], w_gate_up[0], k_k[0], k_a[0], r_k[0].reshape(-1), ln_x_w[0],
                  ln_x_b[0], w_up_a[0], w_up_b[0], w_o[0], norm_ffn_g[0], w_ffn_in[0],
                  w_ffn_out[0], norm_final_g)
```

```python
import functools

import jax
import jax.numpy as jnp
from jax import lax
from jax.experimental import pallas as pl
from jax.experimental.pallas import tpu as pltpu

F32 = jnp.float32
BF16 = jnp.bfloat16

D_MODEL = 1024
HEAD_DIM = 64
N_HEADS = 8
WIDTH = N_HEADS * HEAD_DIM
N_PAIRS = N_HEADS // 2
LANES = 128
DECAY_LORA = 64
ICLR_LORA = 64
GATE_LORA = 160
GN_EPS = 64e-5
EXP_NEG_HALF = 0.6065306597126334
RMS_EPS = 1e-6
MOBA_BLOCK = 256
MOBA_TOP_K = 3
FFN_HIDDEN = 2816
SHIFT_WIDTH = 3 * WIDTH + DECAY_LORA + ICLR_LORA + GATE_LORA
SHIFT_PAD = 1920
QKV_B = 3 * WIDTH
GATES = 2 * D_MODEL
PROJ_PAD = SHIFT_PAD + QKV_B + GATES

CHUNK = 64
TOKEN_TILE = 512
TIME_TILE = 256
VMEM_LIMIT = 56 * 1024 * 1024
MOBA_VMEM_LIMIT = 60 * 1024 * 1024


def _dot(a, b):
    return jnp.dot(a, b, preferred_element_type=F32)


def _dot_nt(a, b):
    return lax.dot_general(a, b, (((1,), (1,)), ((), ())), preferred_element_type=F32)


def _rms(x, g):
    return x * lax.rsqrt(jnp.mean(x * x, axis=-1, keepdims=True) + RMS_EPS) * g


def _proj_kernel(x_ref, g_ref, wsp_ref, wqkv_ref, wgate_ref, mu_ref, sp_ref, qkv_ref, gate_ref,
                 carry_ref, *, tiles_per_seq):
    tm = x_ref.shape[0]
    step = pl.program_id(0)

    @pl.when(step == 0)
    def _():
        carry_ref[...] = jnp.zeros_like(carry_ref)

    h = _rms(x_ref[...], g_ref[...]).astype(BF16)
    qkv_ref[...] = _dot(h, wqkv_ref[...]).astype(BF16)
    gate_ref[...] = _dot(h, wgate_ref[...]).astype(BF16)

    p = _dot(h, wsp_ref[...])
    seq_start = lax.rem(step, tiles_per_seq) == 0
    carry = jnp.where(seq_start, 0.0, carry_ref[0:1, :])
    row = lax.broadcasted_iota(jnp.int32, (tm, 1), 0)
    prev = jnp.where(row == 0, carry, pltpu.roll(p, 1, 0))
    carry_ref[0:1, :] = p[tm - 1:tm, :]
    sp_ref[...] = p + mu_ref[...] * (prev - p)


def _proj(x2d, g, w_sp, w_qkv, w_gate, mu, tm, seq):
    t = x2d.shape[0]
    const = lambda i: (0, 0)
    weight = lambda n: pl.BlockSpec((D_MODEL, n), const, pipeline_mode=pl.Buffered(1))
    return pl.pallas_call(
        functools.partial(_proj_kernel, tiles_per_seq=seq // tm),
        grid=(t // tm,),
        in_specs=[
            pl.BlockSpec((tm, D_MODEL), lambda i: (i, 0)),
            pl.BlockSpec((1, D_MODEL), const),
            weight(SHIFT_PAD), weight(QKV_B), weight(GATES),
            pl.BlockSpec((1, SHIFT_PAD), const),
        ],
        scratch_shapes=[pltpu.VMEM((8, SHIFT_PAD), F32)],
        out_specs=[
            pl.BlockSpec((tm, SHIFT_PAD), lambda i: (i, 0)),
            pl.BlockSpec((tm, QKV_B), lambda i: (i, 0)),
            pl.BlockSpec((tm, GATES), lambda i: (i, 0)),
        ],
        out_shape=[
            jax.ShapeDtypeStruct((t, SHIFT_PAD), F32),
            jax.ShapeDtypeStruct((t, QKV_B), BF16),
            jax.ShapeDtypeStruct((t, GATES), BF16),
        ],
        compiler_params=pltpu.CompilerParams(
            dimension_semantics=("arbitrary",), vmem_limit_bytes=VMEM_LIMIT),
        name="proj",
    )(x2d, g, w_sp, w_qkv, w_gate, mu)


def _split2(x):
    hi = x.astype(BF16)
    lo = (x - hi.astype(F32)).astype(BF16)
    return hi, lo


SEG_LANES = 2 * LANES


def _segsum(x, seg):
    xb = x.astype(BF16)
    return jnp.concatenate(
        [_dot(xb[:, o:o + SEG_LANES], seg) for o in range(0, x.shape[1], SEG_LANES)], axis=1)


N_STREAMS = 2


def _rwkv_kernel(sp_ref, w0_ref, wda_ref, a0_ref, wg_ref, kk_ref, ka_ref, rk_ref,
                 lnw_ref, lnb_ref, seg_ref, tri_ref, o_ref, h_ref, *, tt):
    @pl.when(pl.program_id(1) == 0)
    def _():
        h_ref[...] = jnp.zeros_like(h_ref)

    params = (w0_ref, wda_ref, a0_ref, wg_ref, kk_ref, ka_ref, rk_ref, lnw_ref, lnb_ref,
              seg_ref, tri_ref)
    live = [_rwkv_stream(k, sp_ref, o_ref, h_ref, *params, tt=tt) for k in range(N_STREAMS)]
    done = object()
    while live:
        live = [gen for gen in live if next(gen, done) is not done]


def _rwkv_stream(k_seq, sp_ref, o_ref, h_ref, w0_ref, wda_ref, a0_ref, wg_ref, kk_ref, ka_ref,
                 rk_ref, lnw_ref, lnb_ref, seg_ref, tri_ref, *, tt):
    n_chunks = tt // CHUNK
    xs = sp_ref[k_seq, 0]
    r = xs[:, 0:WIDTH]
    k = xs[:, WIDTH:2 * WIDTH]
    v = xs[:, 2 * WIDTH:3 * WIDTH]
    da = xs[:, 3 * WIDTH:3 * WIDTH + LANES]
    gd = xs[:, 3 * WIDTH + LANES:SHIFT_PAD]
    lane = lax.broadcasted_iota(jnp.int32, (1, LANES), 1)
    lo_half = lane < HEAD_DIM
    da_act = jnp.where(lo_half, jnp.tanh(da), da).astype(BF16)
    lora = _dot(da_act, wda_ref[...])
    g = _dot(jax.nn.sigmoid(gd).astype(BF16), wg_ref[...])

    logdec = -EXP_NEG_HALF * jax.nn.sigmoid(w0_ref[...] + lora[:, 0:WIDTH])
    a = jax.nn.sigmoid(a0_ref[...] + lora[:, WIDTH:2 * WIDTH])

    seg = seg_ref[...]
    kkf = k * kk_ref[...]
    kk_sq = _segsum(kkf * kkf, seg)
    yield
    kk = kkf * lax.rsqrt(jnp.maximum(kk_sq, 1e-24))
    k2 = k * (1.0 + (a - 1.0) * ka_ref[...])
    bonus = _segsum(r * k2 * rk_ref[...], seg) * v

    l1, l2 = _split2(logdec)
    tri = tri_ref[...]
    cum = _dot(tri, l1) + _dot(tri, l2)
    yield

    kka = kk * a
    p_inv = jnp.exp(-cum)
    a_t = -kk * jnp.exp(cum - logdec)
    r_t = r * jnp.exp(cum)
    b_t = kka * p_inv
    k_t = k2 * p_inv

    r64 = lax.broadcasted_iota(jnp.int32, (CHUNK, LANES), 0)
    c64 = lax.broadcasted_iota(jnp.int32, (CHUNK, LANES), 1) & (CHUNK - 1)
    strict = c64 < r64
    incl = c64 <= r64
    eye_side = (c64 == r64).astype(F32)
    r128 = lax.broadcasted_iota(jnp.int32, (LANES, LANES), 0)
    c128 = lax.broadcasted_iota(jnp.int32, (LANES, LANES), 1)
    eye128 = (r128 == c128).astype(F32)
    same_head = (r128 >> 6) == (c128 >> 6)
    hi_half = jnp.logical_not(lo_half)
    zeros_c = jnp.zeros((CHUNK, LANES), F32)

    def _stack2(zz):
        zb = zz.astype(BF16)
        return jnp.concatenate([jnp.where(lo_half, zb, 0), jnp.where(hi_half, zb, 0)], axis=0)

    chains = [(c, pr) for c in range(n_chunks) for pr in range(N_PAIRS)]

    def _piece(arr, c, pr):
        return arr[c * CHUNK:(c + 1) * CHUNK, pr * LANES:(pr + 1) * LANES]

    x_ab, x_ak, x_rb, x_rk = {}, {}, {}, {}
    for ch in chains:
        lq = jnp.concatenate([_piece(a_t, *ch), _piece(r_t, *ch)], axis=0).astype(BF16)
        rhs = jnp.concatenate([_stack2(_piece(b_t, *ch)), _stack2(_piece(k_t, *ch))], axis=0)
        sc = _dot_nt(lq, rhs)
        x_ab[ch] = jnp.where(strict, sc[0:CHUNK, 0:LANES], 0.0)
        x_ak[ch] = jnp.where(strict, sc[0:CHUNK, LANES:2 * LANES], 0.0)
        x_rb[ch] = jnp.where(incl, sc[CHUNK:2 * CHUNK, 0:LANES], 0.0)
        x_rk[ch] = jnp.where(incl, sc[CHUNK:2 * CHUNK, LANES:2 * LANES], 0.0)
    yield

    x_t = {ch: eye_side + x_ab[ch] for ch in chains}
    x_pw = {ch: _dot(x_ab[ch].astype(BF16), _stack2(x_ab[ch])) for ch in chains}
    av = {ch: _dot(x_ak[ch].astype(BF16), _stack2(_piece(v, *ch))) for ch in chains}
    yield
    for _ in range(4):
        for ch in chains:
            both = jnp.concatenate([x_t[ch], x_pw[ch]], axis=0).astype(BF16)
            out = _dot(both, _stack2(x_pw[ch]))
            x_t[ch] = x_t[ch] + out[0:CHUNK]
            x_pw[ch] = out[CHUNK:2 * CHUNK]
        yield
    for ch in chains:
        x_t[ch] = x_t[ch] + _dot(x_t[ch].astype(BF16), _stack2(x_pw[ch]))
    yield

    tw = {}
    for ch in chains:
        rhs = jnp.concatenate([_stack2(_piece(a_t, *ch)), _stack2(av[ch])], axis=1)
        tw[ch] = _dot(x_t[ch].astype(BF16), rhs)
    yield
    qeff, yloc, phi, psi = {}, {}, {}, {}
    for ch in chains:
        ta, w_loc, v_p = tw[ch][:, 0:LANES], tw[ch][:, LANES:2 * LANES], _piece(v, *ch)
        rhs = jnp.concatenate(
            [jnp.concatenate([_stack2(ta), _stack2(w_loc)], axis=1),
             jnp.concatenate([jnp.zeros((LANES, LANES), BF16), _stack2(v_p)], axis=1)], axis=0)
        ag = _dot(jnp.concatenate([x_rb[ch], x_rk[ch]], axis=1).astype(BF16), rhs)
        qeff[ch] = _piece(r_t, *ch) + ag[:, 0:LANES]
        yloc[ch] = ag[:, LANES:2 * LANES]
        cum_last = _piece(cum, *ch)[CHUNK - 1:CHUNK, :]
        p_end = jnp.exp(cum_last)
        kk_t = jnp.concatenate([_piece(b_t, *ch) * p_end, _piece(k_t, *ch) * p_end], axis=0).T
        rhs = jnp.concatenate(
            [tw[ch], jnp.concatenate([zeros_c, v_p], axis=1)], axis=0).astype(BF16)
        pp = _dot(kk_t.astype(BF16), rhs)
        phi[ch] = eye128 * p_end + jnp.where(same_head, pp[:, 0:LANES], 0.0)
        psi[ch] = jnp.where(same_head, pp[:, LANES:2 * LANES], 0.0)
    yield

    y_rows = []
    states = [h_ref[k_seq, pr] for pr in range(N_PAIRS)]
    for c in range(n_chunks):
        y_c = []
        for pr in range(N_PAIRS):
            ch = (c, pr)
            lhs = jnp.concatenate([qeff[ch], phi[ch]], axis=0).astype(BF16)
            out = _dot(lhs, states[pr].astype(BF16))
            y_c.append(out[0:CHUNK] + yloc[ch])
            states[pr] = out[CHUNK:CHUNK + LANES] + psi[ch]
        y_rows.append(jnp.concatenate(y_c, axis=1))
        yield
    for pr in range(N_PAIRS):
        h_ref[k_seq, pr] = states[pr]
    y = jnp.concatenate(y_rows, axis=0)

    mean = _segsum(y, seg) * (1.0 / HEAD_DIM)
    yield
    d = y - mean
    var = _segsum(d * d, seg) * (1.0 / HEAD_DIM)
    yield
    yn = d * lax.rsqrt(var + GN_EPS) * lnw_ref[...] + lnb_ref[...]
    o_ref[k_seq, 0] = ((yn + bonus) * g).astype(BF16)


def _rwkv(sp, w0, wda, a0, wg, k_k, k_a, r_k, ln_w, ln_b, seg, tri, tt):
    b, s, _ = sp.shape
    assert b % N_STREAMS == 0
    rows = b // N_STREAMS
    const2 = lambda i, j: (0, 0)
    row = lambda n: pl.BlockSpec((1, n), const2)
    out = pl.pallas_call(
        functools.partial(_rwkv_kernel, tt=tt),
        grid=(rows, s // tt),
        in_specs=[
            pl.BlockSpec((N_STREAMS, 1, tt, SHIFT_PAD), lambda i, j: (0, i, j, 0)),
            row(WIDTH),
            pl.BlockSpec((LANES, 2 * WIDTH), const2),
            row(WIDTH),
            pl.BlockSpec((2 * LANES, WIDTH), const2),
            row(WIDTH), row(WIDTH), row(WIDTH), row(WIDTH), row(WIDTH),
            pl.BlockSpec((SEG_LANES, SEG_LANES), const2),
            pl.BlockSpec((tt, tt), const2),
        ],
        out_specs=pl.BlockSpec((N_STREAMS, 1, tt, WIDTH), lambda i, j: (0, i, j, 0)),
        out_shape=jax.ShapeDtypeStruct((N_STREAMS, rows, s, WIDTH), BF16),
        scratch_shapes=[
            pltpu.VMEM((N_STREAMS, N_PAIRS, LANES, LANES), F32),
        ],
        compiler_params=pltpu.CompilerParams(
            dimension_semantics=("arbitrary", "arbitrary"), vmem_limit_bytes=VMEM_LIMIT),
        name="rwkv",
    )(sp.reshape(N_STREAMS, rows, s, SHIFT_PAD), w0, wda, a0, wg, k_k, k_a, r_k, ln_w, ln_b,
      seg, tri)
    return out.reshape(b, s, WIDTH)


NEG_BIG = -(2.0 ** 30)
N_BLOCKS_MAX = 16
V_ROWS = HEAD_DIM + 16
LOG2E = 1.4426950408889634
Q_GROUP = 2


def _aligned(i, m):
    return i * m if isinstance(i, int) else pl.multiple_of(i * m, m)


def _moba_kernel(q_ref, k_ref, v_ref, slope_ref, o_ref, kx_ref, vt_ref, qt_ref, bias_ref, *, seq):
    nb = seq // MOBA_BLOCK
    bs = MOBA_BLOCK
    scale = HEAD_DIM ** -0.5
    heads = range(N_HEADS)

    rr = lax.broadcasted_iota(jnp.int32, (bs, LANES), 0)
    ll = lax.broadcasted_iota(jnp.int32, (bs, LANES), 1)
    ones_rows = jnp.ones((V_ROWS - HEAD_DIM, bs), BF16)
    for n in range(nb):
        aug = jnp.where(ll < 4, 1.0,
              jnp.where(ll < 6, rr.astype(F32),
              jnp.where(ll < 8, float(n * bs),
              jnp.where(ll == 8 + n, 1.0, 0.0)))).astype(BF16)
        rows = slice(n * bs, (n + 1) * bs)
        for pr in range(N_PAIRS):
            ln = slice(pr * LANES, (pr + 1) * LANES)
            kx_ref[pr, n, :, 0:LANES] = k_ref[0, rows, ln]
            kx_ref[pr, n, :, LANES:2 * LANES] = aug
            v_t = v_ref[0, rows, ln].astype(F32).T.astype(BF16)
            for half in range(2):
                vt_ref[2 * pr + half, n] = jnp.concatenate(
                    [v_t[half * HEAD_DIM:(half + 1) * HEAD_DIM], ones_rows], axis=0)

    bi = lax.broadcasted_iota(jnp.int32, (N_BLOCKS_MAX, seq), 0)
    si = lax.broadcasted_iota(jnp.int32, (N_BLOCKS_MAX, seq), 1)
    avg = jnp.where((si >= bi * bs) & (si < (bi + 1) * bs), 1.0 / bs, 0.0).astype(BF16)
    kmean = _dot(avg, k_ref[0])
    km = jnp.concatenate([jnp.broadcast_to(kmean[n:n + 1, :], (N_HEADS, WIDTH))
                          for n in range(N_BLOCKS_MAX)], axis=0)
    row_head = lax.broadcasted_iota(jnp.int32, (LANES, WIDTH), 0) & (N_HEADS - 1)
    lane_head = lax.broadcasted_iota(jnp.int32, (LANES, WIDTH), 1) >> 6
    km_hi, km_lo = _split2(jnp.where(row_head == lane_head, km, 0.0))

    pr_i = lax.broadcasted_iota(jnp.int32, (LANES, LANES), 0)
    pc_i = lax.broadcasted_iota(jnp.int32, (LANES, LANES), 1)
    perm = (pc_i == ((pr_i & (N_BLOCKS_MAX - 1)) << 3) + (pr_i >> 4)).astype(BF16)
    zeros_cnt = jnp.zeros((N_HEADS, bs), F32)

    for i in range(nb):
        q_blk = q_ref[0, i * bs:(i + 1) * bs, :].astype(F32)
        q_t = [q_blk[:, pr * LANES:(pr + 1) * LANES].T for pr in range(N_PAIRS)]
        for pr in range(N_PAIRS):
            qt_ref[pr, i] = q_t[pr].astype(BF16)
        if i <= MOBA_TOP_K:
            bias_ref[i] = jnp.zeros((N_BLOCKS_MAX * N_HEADS, bs), F32)
            continue
        q_tb = jnp.concatenate(q_t, axis=0).astype(BF16)
        gate = _dot(km_hi, q_tb) + _dot(km_lo, q_tb)
        slabs = [gate[n * N_HEADS:(n + 1) * N_HEADS] for n in range(i)]
        cnt = [zeros_cnt] * i
        for n in range(i):
            for m in range(n):
                m_wins = slabs[m] >= slabs[n]
                cnt[n] = cnt[n] + jnp.where(m_wins, 1.0, 0.0)
                cnt[m] = cnt[m] + jnp.where(m_wins, 0.0, 1.0)
        bias = [jnp.where(cnt[n] < MOBA_TOP_K, 0.0, NEG_BIG) for n in range(i)]
        bias += [zeros_cnt] * (N_BLOCKS_MAX - i)
        bias_ref[i] = _dot(perm, jnp.concatenate(bias, axis=0).astype(BF16))

    r128 = lax.broadcasted_iota(jnp.int32, (LANES, 1), 0)
    row_masks = (r128 < HEAD_DIM, r128 >= HEAD_DIM)
    r8 = lax.broadcasted_iota(jnp.int32, (8, bs), 0)
    tl = lax.broadcasted_iota(jnp.int32, (8, bs), 1).astype(F32)
    key_row = lax.broadcasted_iota(jnp.int32, (bs, bs), 0)
    qry_col = lax.broadcasted_iota(jnp.int32, (bs, bs), 1)
    causal = key_row <= qry_col
    zeros_pad = jnp.zeros((LANES - 8 - N_BLOCKS_MAX, bs), F32)

    def q_block(i, odd):
        start = _aligned(i, bs)
        q_t = [qt_ref[pr, i].astype(F32) for pr in range(N_PAIRS)]
        bias = bias_ref[i]
        t0 = jnp.asarray(i * bs, F32)

        q_aug = []
        for h in heads:
            c = slope_ref[h:h + 1, :] * LOG2E
            c_hi = c.astype(BF16).astype(F32)
            ctl = c * tl
            ctl_hi = ctl.astype(BF16).astype(F32)
            ct0 = c * t0
            ct0_hi = ct0.astype(BF16).astype(F32)
            aug8 = jnp.where(r8 == 0, -ctl_hi,
                   jnp.where(r8 == 1, ctl_hi - ctl,
                   jnp.where(r8 == 2, -ct0_hi,
                   jnp.where(r8 == 3, ct0_hi - ct0,
                   jnp.where((r8 == 4) | (r8 == 6), c_hi, c - c_hi)))))
            top = jnp.where(row_masks[h % 2], q_t[h // 2], 0.0) * (scale * LOG2E)
            sel = bias[h * N_BLOCKS_MAX:(h + 1) * N_BLOCKS_MAX]
            q_aug.append(jnp.concatenate([top, aug8, sel, zeros_pad], axis=0).astype(BF16))

        def scores(j):
            kx = [kx_ref[pr, j] for pr in range(N_PAIRS)]
            return [_dot(kx[h // 2], q_aug[h]) for h in heads]

        def weighted_values(j, ps):
            return [_dot(vt_ref[h, j], ps[h]) for h in heads]

        s_own = scores(i)
        yield
        m0, p0 = [], []
        for h in heads:
            sh = jnp.where(causal, s_own[h], NEG_BIG)
            m0.append(jnp.max(sh, axis=0, keepdims=True))
            p0.append(jnp.exp2(sh - m0[h]).astype(BF16))
        pv = weighted_values(i, p0)
        yield
        acc0 = [pv[h][0:HEAD_DIM] for h in heads]
        l0 = [pv[h][HEAD_DIM:HEAD_DIM + 1] for h in heads]

        def step(js, st):
            m, l, acc = (list(x) for x in st)
            s_all = [scores(j) for j in js]
            ps, alphas = [], []
            for s in s_all:
                p_blk, a_blk = [], []
                for h in heads:
                    mh = jnp.maximum(m[h], jnp.max(s[h], axis=0, keepdims=True))
                    a_blk.append(jnp.exp2(m[h] - mh))
                    p_blk.append(jnp.exp2(s[h] - mh).astype(BF16))
                    m[h] = mh
                ps.append(p_blk)
                alphas.append(a_blk)
            pvs = [weighted_values(j, p_blk) for j, p_blk in zip(js, ps)]
            for a_blk, pv in zip(alphas, pvs):
                for h in heads:
                    acc[h] = a_blk[h] * acc[h] + pv[h][0:HEAD_DIM]
                    l[h] = a_blk[h] * l[h] + pv[h][HEAD_DIM:HEAD_DIM + 1]
            return tuple(tuple(x) for x in (m, l, acc))

        st = (tuple(m0), tuple(l0), tuple(acc0))
        if odd:
            st = step([i - 1], st)
        _, l, acc = lax.fori_loop(0, i >> 1, lambda t, st: step([2 * t, 2 * t + 1], st), st)
        pairs = []
        for pr in range(N_PAIRS):
            a, b = 2 * pr, 2 * pr + 1
            out_t = jnp.concatenate([acc[a] / l[a], acc[b] / l[b]], axis=0)
            pairs.append(out_t.T)
        o_ref[0, pl.ds(start, bs), :] = jnp.concatenate(pairs, axis=1).astype(BF16)

    def q_block_group(u, carry):
        live = [q_block(Q_GROUP * u + g, g % 2 == 1) for g in range(Q_GROUP)]
        done = object()
        while live:
            live = [gen for gen in live if next(gen, done) is not done]
        return carry

    assert nb % Q_GROUP == 0 and Q_GROUP % 2 == 0
    lax.fori_loop(0, nb // Q_GROUP, q_block_group, 0)


def _moba(qkv, slopes):
    b, s, _ = qkv.shape
    assert s % MOBA_BLOCK == 0 and s // MOBA_BLOCK <= N_BLOCKS_MAX
    nb = s // MOBA_BLOCK
    col = lambda c: pl.BlockSpec((1, s, WIDTH), lambda i: (i, 0, c))
    return pl.pallas_call(
        functools.partial(_moba_kernel, seq=s),
        grid=(b,),
        in_specs=[col(0), col(1), col(2), pl.BlockSpec((N_HEADS, MOBA_BLOCK), lambda i: (0, 0))],
        out_specs=pl.BlockSpec((1, s, WIDTH), lambda i: (i, 0, 0)),
        out_shape=jax.ShapeDtypeStruct((b, s, WIDTH), BF16),
        scratch_shapes=[
            pltpu.VMEM((N_PAIRS, nb, MOBA_BLOCK, 2 * LANES), BF16),
            pltpu.VMEM((N_HEADS, nb, V_ROWS, MOBA_BLOCK), BF16),
            pltpu.VMEM((N_PAIRS, nb, LANES, MOBA_BLOCK), BF16),
            pltpu.VMEM((nb, N_BLOCKS_MAX * N_HEADS, MOBA_BLOCK), F32),
        ],
        compiler_params=pltpu.CompilerParams(
            dimension_semantics=("arbitrary",), vmem_limit_bytes=MOBA_VMEM_LIMIT),
        name="moba",
    )(qkv, qkv, qkv, slopes)


FFN_CHUNKS = ((0, 1024), (1024, 1024), (2048, 768))


POST_STREAMS = 2


def _post_rows(rows, x_ref, ya_ref, yb_ref, gate_ref, wua_ref, wub_ref, wo_ref, gffn_ref,
               wfi_ref, wfo_ref, gfin_ref, o_ref):
    y_a = _dot(ya_ref[rows, :], wua_ref[...])
    y_b = _dot(yb_ref[rows, :], wub_ref[...])
    yield
    ga = jax.nn.sigmoid(gate_ref[rows, 0:D_MODEL].astype(F32))
    gb = jax.nn.sigmoid(gate_ref[rows, D_MODEL:2 * D_MODEL].astype(F32))
    mixed = (ga * y_a + gb * y_b).astype(BF16)
    x1 = x_ref[rows, :] + _dot(mixed, wo_ref[...])
    yield
    h = _rms(x1, gffn_ref[...]).astype(BF16)
    acc = x1
    for off, n in FFN_CHUNKS:
        gg = _dot(h, wfi_ref[:, off:off + n])
        uu = _dot(h, wfi_ref[:, FFN_HIDDEN + off:FFN_HIDDEN + off + n])
        yield
        act = (gg * jax.nn.sigmoid(gg) * uu).astype(BF16)
        acc = acc + _dot(act, wfo_ref[off:off + n, :])
        yield
    o_ref[rows, :] = _rms(acc, gfin_ref[...])


def _post_kernel(*refs):
    tm = refs[0].shape[0]
    part = tm // POST_STREAMS
    live = [_post_rows(slice(k * part, (k + 1) * part), *refs) for k in range(POST_STREAMS)]
    done = object()
    while live:
        live = [gen for gen in live if next(gen, done) is not done]


def _post(x2d, ya, yb, gates, wua, wub, wo, gffn, wfi, wfo, gfin, tm):
    t = x2d.shape[0]
    const = lambda i: (0, 0)
    tile = lambda n: pl.BlockSpec((tm, n), lambda i: (i, 0))
    weight = lambda a: pl.BlockSpec(a.shape, const, pipeline_mode=pl.Buffered(1))
    return pl.pallas_call(
        _post_kernel,
        grid=(t // tm,),
        in_specs=[tile(D_MODEL), tile(WIDTH), tile(WIDTH), tile(GATES),
                  weight(wua), weight(wub), weight(wo), weight(gffn),
                  weight(wfi), weight(wfo), weight(gfin)],
        out_specs=tile(D_MODEL),
        out_shape=jax.ShapeDtypeStruct((t, D_MODEL), F32),
        compiler_params=pltpu.CompilerParams(
            dimension_semantics=("arbitrary",), vmem_limit_bytes=VMEM_LIMIT),
        name="post",
    )(x2d, ya, yb, gates, wua, wub, wo, gffn, wfi, wfo, gfin)


def _layer(x, norm_mix_g, w_in, mu_shift, w0, w_decay_up, a0, w_iclr_up, w_gate_up,
           k_k, k_a, r_k, ln_x_w, ln_x_b, w_up_a, w_up_b, w_o, norm_ffn_g,
           w_ffn_in, w_ffn_out, out_g):
    b, s, _ = x.shape
    t = b * s
    x2d = x.reshape(t, D_MODEL)

    pad = SHIFT_PAD - SHIFT_WIDTH
    w_sp = jnp.pad(w_in[:, :SHIFT_WIDTH].astype(BF16), ((0, 0), (0, pad)))
    w_qkv = w_in[:, SHIFT_WIDTH:SHIFT_WIDTH + QKV_B].astype(BF16)
    w_gate = w_in[:, SHIFT_WIDTH + QKV_B:].astype(BF16)
    mu = jnp.pad(mu_shift, (0, pad)).reshape(1, SHIFT_PAD)
    wda = jnp.zeros((LANES, 2 * WIDTH), F32)
    wda = wda.at[:DECAY_LORA, :WIDTH].set(w_decay_up).at[DECAY_LORA:, WIDTH:].set(w_iclr_up)
    wg = jnp.pad(w_gate_up, ((0, 2 * LANES - GATE_LORA), (0, 0)))
    row = lambda a: a.reshape(1, -1)

    tt = TIME_TILE
    hid = jnp.arange(SEG_LANES) // HEAD_DIM
    seg = (hid[:, None] == hid[None, :]).astype(BF16)
    ti = jnp.arange(tt)
    tri = ((ti[:, None] >= ti[None, :]) & (ti[:, None] // CHUNK == ti[None, :] // CHUNK)).astype(BF16)

    sp, qkv, gates = _proj(x2d, row(norm_mix_g), w_sp, w_qkv, w_gate, mu, tm=TOKEN_TILE,
                            seq=s)
    ya = _rwkv(sp.reshape(b, s, SHIFT_PAD), row(w0), wda.astype(BF16), row(a0),
               wg.astype(BF16), row(k_k), row(k_a), row(r_k), row(ln_x_w), row(ln_x_b),
               seg, tri, tt)
    slopes = jnp.exp2(-8.0 * jnp.arange(1, N_HEADS + 1, dtype=F32) / N_HEADS)
    slopes = jnp.broadcast_to(slopes[:, None], (N_HEADS, MOBA_BLOCK))
    yb = _moba(qkv.reshape(b, s, QKV_B), slopes)
    out = _post(x2d, ya.reshape(t, WIDTH), yb.reshape(t, WIDTH), gates,
                w_up_a.astype(BF16), w_up_b.astype(BF16), w_o.astype(BF16), row(norm_ffn_g),
                w_ffn_in.astype(BF16), w_ffn_out.astype(BF16), row(out_g), tm=TOKEN_TILE)
    return out.reshape(b, s, D_MODEL)


def kernel(x, norm_mix_g, w_in, mu_shift, w0, w_decay_up, a0, w_iclr_up, w_gate_up, k_k, k_a, r_k, ln_x_w, ln_x_b, w_up_a, w_up_b, w_o, norm_ffn_g, w_ffn_in, w_ffn_out, norm_final_g):
    depth = w_in.shape[0]
    assert depth == 1, "the fused post kernel applies the final norm after the only layer"
    return _layer(x, norm_mix_g[0], w_in[0], mu_shift[0], w0[0], w_decay_up[0], a0[0],
                  w_iclr_up[0], w_gate_up[0], k_k[0], k_a[0], r_k[0].reshape(-1), ln_x_w[0],
                  ln_x_b[0], w_up_a[0], w_up_b[0], w_o[0], norm_ffn_g[0], w_ffn_in[0],
                  w_ffn_out[0], norm_final_g)
```

```python
import functools

import jax
import jax.numpy as jnp
from jax import lax
from jax.experimental import pallas as pl
from jax.experimental.pallas import tpu as pltpu

F32 = jnp.float32
BF16 = jnp.bfloat16

D_MODEL = 1024
HEAD_DIM = 64
N_HEADS = 8
WIDTH = N_HEADS * HEAD_DIM
N_PAIRS = N_HEADS // 2
LANES = 128
DECAY_LORA = 64
ICLR_LORA = 64
GATE_LORA = 160
GN_EPS = 64e-5
EXP_NEG_HALF = 0.6065306597126334
RMS_EPS = 1e-6
MOBA_BLOCK = 256
MOBA_TOP_K = 3
FFN_HIDDEN = 2816
SHIFT_WIDTH = 3 * WIDTH + DECAY_LORA + ICLR_LORA + GATE_LORA
SHIFT_PAD = 1920
QKV_B = 3 * WIDTH
GATES = 2 * D_MODEL
PROJ_PAD = SHIFT_PAD + QKV_B + GATES

CHUNK = 64
TOKEN_TILE = 512
TIME_TILE = 256
VMEM_LIMIT = 56 * 1024 * 1024
MOBA_VMEM_LIMIT = 60 * 1024 * 1024


def _dot(a, b):
    return jnp.dot(a, b, preferred_element_type=F32)


def _dot_nt(a, b):
    return lax.dot_general(a, b, (((1,), (1,)), ((), ())), preferred_element_type=F32)


def _rms(x, g):
    return x * lax.rsqrt(jnp.mean(x * x, axis=-1, keepdims=True) + RMS_EPS) * g


def _proj_kernel(x_ref, g_ref, wsp_ref, wqkv_ref, wgate_ref, mu_ref, sp_ref, qkv_ref, gate_ref,
                 carry_ref, *, tiles_per_seq):
    tm = x_ref.shape[0]
    step = pl.program_id(0)

    @pl.when(step == 0)
    def _():
        carry_ref[...] = jnp.zeros_like(carry_ref)

    h = _rms(x_ref[...], g_ref[...]).astype(BF16)
    qkv_ref[...] = _dot(h, wqkv_ref[...]).astype(BF16)
    gate_ref[...] = _dot(h, wgate_ref[...]).astype(BF16)

    p = _dot(h, wsp_ref[...])
    seq_start = lax.rem(step, tiles_per_seq) == 0
    carry = jnp.where(seq_start, 0.0, carry_ref[0:1, :])
    row = lax.broadcasted_iota(jnp.int32, (tm, 1), 0)
    prev = jnp.where(row == 0, carry, pltpu.roll(p, 1, 0))
    carry_ref[0:1, :] = p[tm - 1:tm, :]
    sp_ref[...] = p + mu_ref[...] * (prev - p)


def _proj(x2d, g, w_sp, w_qkv, w_gate, mu, tm, seq):
    t = x2d.shape[0]
    const = lambda i: (0, 0)
    weight = lambda n: pl.BlockSpec((D_MODEL, n), const, pipeline_mode=pl.Buffered(1))
    return pl.pallas_call(
        functools.partial(_proj_kernel, tiles_per_seq=seq // tm),
        grid=(t // tm,),
        in_specs=[
            pl.BlockSpec((tm, D_MODEL), lambda i: (i, 0)),
            pl.BlockSpec((1, D_MODEL), const),
            weight(SHIFT_PAD), weight(QKV_B), weight(GATES),
            pl.BlockSpec((1, SHIFT_PAD), const),
        ],
        scratch_shapes=[pltpu.VMEM((8, SHIFT_PAD), F32)],
        out_specs=[
            pl.BlockSpec((tm, SHIFT_PAD), lambda i: (i, 0)),
            pl.BlockSpec((tm, QKV_B), lambda i: (i, 0)),
            pl.BlockSpec((tm, GATES), lambda i: (i, 0)),
        ],
        out_shape=[
            jax.ShapeDtypeStruct((t, SHIFT_PAD), F32),
            jax.ShapeDtypeStruct((t, QKV_B), BF16),
            jax.ShapeDtypeStruct((t, GATES), BF16),
        ],
        compiler_params=pltpu.CompilerParams(
            dimension_semantics=("arbitrary",), vmem_limit_bytes=VMEM_LIMIT),
        name="proj",
    )(x2d, g, w_sp, w_qkv, w_gate, mu)


def _split2(x):
    hi = x.astype(BF16)
    lo = (x - hi.astype(F32)).astype(BF16)
    return hi, lo


SEG_LANES = 2 * LANES


def _segsum(x, seg):
    xb = x.astype(BF16)
    return jnp.concatenate(
        [_dot(xb[:, o:o + SEG_LANES], seg) for o in range(0, x.shape[1], SEG_LANES)], axis=1)


N_STREAMS = 2


def _rwkv_kernel(sp_ref, w0_ref, wda_ref, a0_ref, wg_ref, kk_ref, ka_ref, rk_ref,
                 lnw_ref, lnb_ref, seg_ref, tri_ref, o_ref, h_ref, *, tt):
    @pl.when(pl.program_id(1) == 0)
    def _():
        h_ref[...] = jnp.zeros_like(h_ref)

    params = (w0_ref, wda_ref, a0_ref, wg_ref, kk_ref, ka_ref, rk_ref, lnw_ref, lnb_ref,
              seg_ref, tri_ref)
    live = [_rwkv_stream(k, sp_ref, o_ref, h_ref, *params, tt=tt) for k in range(N_STREAMS)]
    done = object()
    while live:
        live = [gen for gen in live if next(gen, done) is not done]


def _rwkv_stream(k_seq, sp_ref, o_ref, h_ref, w0_ref, wda_ref, a0_ref, wg_ref, kk_ref, ka_ref,
                 rk_ref, lnw_ref, lnb_ref, seg_ref, tri_ref, *, tt):
    n_chunks = tt // CHUNK
    xs = sp_ref[k_seq, 0]
    r = xs[:, 0:WIDTH]
    k = xs[:, WIDTH:2 * WIDTH]
    v = xs[:, 2 * WIDTH:3 * WIDTH]
    da = xs[:, 3 * WIDTH:3 * WIDTH + LANES]
    gd = xs[:, 3 * WIDTH + LANES:SHIFT_PAD]
    lane = lax.broadcasted_iota(jnp.int32, (1, LANES), 1)
    lo_half = lane < HEAD_DIM
    da_act = jnp.where(lo_half, jnp.tanh(da), da).astype(BF16)
    lora = _dot(da_act, wda_ref[...])
    g = _dot(jax.nn.sigmoid(gd).astype(BF16), wg_ref[...])

    logdec = -EXP_NEG_HALF * jax.nn.sigmoid(w0_ref[...] + lora[:, 0:WIDTH])
    a = jax.nn.sigmoid(a0_ref[...] + lora[:, WIDTH:2 * WIDTH])

    seg = seg_ref[...]
    kkf = k * kk_ref[...]
    kk_sq = _segsum(kkf * kkf, seg)
    yield
    kk = kkf * lax.rsqrt(jnp.maximum(kk_sq, 1e-24))
    k2 = k * (1.0 + (a - 1.0) * ka_ref[...])
    bonus = _segsum(r * k2 * rk_ref[...], seg) * v

    l1, l2 = _split2(logdec)
    tri = tri_ref[...]
    cum = _dot(tri, l1) + _dot(tri, l2)
    yield

    kka = kk * a
    p_inv = jnp.exp(-cum)
    a_t = -kk * jnp.exp(cum - logdec)
    r_t = r * jnp.exp(cum)
    b_t = kka * p_inv
    k_t = k2 * p_inv

    r64 = lax.broadcasted_iota(jnp.int32, (CHUNK, LANES), 0)
    c64 = lax.broadcasted_iota(jnp.int32, (CHUNK, LANES), 1) & (CHUNK - 1)
    strict = c64 < r64
    incl = c64 <= r64
    eye_side = (c64 == r64).astype(F32)
    r128 = lax.broadcasted_iota(jnp.int32, (LANES, LANES), 0)
    c128 = lax.broadcasted_iota(jnp.int32, (LANES, LANES), 1)
    eye128 = (r128 == c128).astype(F32)
    same_head = (r128 >> 6) == (c128 >> 6)
    hi_half = jnp.logical_not(lo_half)
    zeros_c = jnp.zeros((CHUNK, LANES), F32)

    def _stack2(zz):
        zb = zz.astype(BF16)
        return jnp.concatenate([jnp.where(lo_half, zb, 0), jnp.where(hi_half, zb, 0)], axis=0)

    chains = [(c, pr) for c in range(n_chunks) for pr in range(N_PAIRS)]

    def _piece(arr, c, pr):
        return arr[c * CHUNK:(c + 1) * CHUNK, pr * LANES:(pr + 1) * LANES]

    x_ab, x_ak, x_rb, x_rk = {}, {}, {}, {}
    for ch in chains:
        lq = jnp.concatenate([_piece(a_t, *ch), _piece(r_t, *ch)], axis=0).astype(BF16)
        rhs = jnp.concatenate([_stack2(_piece(b_t, *ch)), _stack2(_piece(k_t, *ch))], axis=0)
        sc = _dot_nt(lq, rhs)
        x_ab[ch] = jnp.where(strict, sc[0:CHUNK, 0:LANES], 0.0)
        x_ak[ch] = jnp.where(strict, sc[0:CHUNK, LANES:2 * LANES], 0.0)
        x_rb[ch] = jnp.where(incl, sc[CHUNK:2 * CHUNK, 0:LANES], 0.0)
        x_rk[ch] = jnp.where(incl, sc[CHUNK:2 * CHUNK, LANES:2 * LANES], 0.0)
    yield

    x_t = {ch: eye_side + x_ab[ch] for ch in chains}
    x_pw = {ch: _dot(x_ab[ch].astype(BF16), _stack2(x_ab[ch])) for ch in chains}
    av = {ch: _dot(x_ak[ch].astype(BF16), _stack2(_piece(v, *ch))) for ch in chains}
    yield
    for _ in range(4):
        for ch in chains:
            both = jnp.concatenate([x_t[ch], x_pw[ch]], axis=0).astype(BF16)
            out = _dot(both, _stack2(x_pw[ch]))
            x_t[ch] = x_t[ch] + out[0:CHUNK]
            x_pw[ch] = out[CHUNK:2 * CHUNK]
        yield
    for ch in chains:
        x_t[ch] = x_t[ch] + _dot(x_t[ch].astype(BF16), _stack2(x_pw[ch]))
    yield

    tw = {}
    for ch in chains:
        rhs = jnp.concatenate([_stack2(_piece(a_t, *ch)), _stack2(av[ch])], axis=1)
        tw[ch] = _dot(x_t[ch].astype(BF16), rhs)
    yield
    qeff, yloc, phi, psi = {}, {}, {}, {}
    for ch in chains:
        ta, w_loc, v_p = tw[ch][:, 0:LANES], tw[ch][:, LANES:2 * LANES], _piece(v, *ch)
        rhs = jnp.concatenate(
            [jnp.concatenate([_stack2(ta), _stack2(w_loc)], axis=1),
             jnp.concatenate([jnp.zeros((LANES, LANES), BF16), _stack2(v_p)], axis=1)], axis=0)
        ag = _dot(jnp.concatenate([x_rb[ch], x_rk[ch]], axis=1).astype(BF16), rhs)
        qeff[ch] = _piece(r_t, *ch) + ag[:, 0:LANES]
        yloc[ch] = ag[:, LANES:2 * LANES]
        cum_last = _piece(cum, *ch)[CHUNK - 1:CHUNK, :]
        p_end = jnp.exp(cum_last)
        kk_t = jnp.concatenate([_piece(b_t, *ch) * p_end, _piece(k_t, *ch) * p_end],
                               axis=0).astype(BF16).T
        rhs = jnp.concatenate(
            [tw[ch], jnp.concatenate([zeros_c, v_p], axis=1)], axis=0).astype(BF16)
        pp = _dot(kk_t, rhs)
        phi[ch] = eye128 * p_end + jnp.where(same_head, pp[:, 0:LANES], 0.0)
        psi[ch] = jnp.where(same_head, pp[:, LANES:2 * LANES], 0.0)
    yield

    y_rows = []
    states = [h_ref[k_seq, pr] for pr in range(N_PAIRS)]
    for c in range(n_chunks):
        y_c = []
        for pr in range(N_PAIRS):
            ch = (c, pr)
            lhs = jnp.concatenate([qeff[ch], phi[ch]], axis=0).astype(BF16)
            out = _dot(lhs, states[pr].astype(BF16))
            y_c.append(out[0:CHUNK] + yloc[ch])
            states[pr] = out[CHUNK:CHUNK + LANES] + psi[ch]
        y_rows.append(jnp.concatenate(y_c, axis=1))
        yield
    for pr in range(N_PAIRS):
        h_ref[k_seq, pr] = states[pr]
    y = jnp.concatenate(y_rows, axis=0)

    mean = _segsum(y, seg) * (1.0 / HEAD_DIM)
    yield
    d = y - mean
    var = _segsum(d * d, seg) * (1.0 / HEAD_DIM)
    yield
    yn = d * lax.rsqrt(var + GN_EPS) * lnw_ref[...] + lnb_ref[...]
    o_ref[k_seq, 0] = ((yn + bonus) * g).astype(BF16)


def _rwkv(sp, w0, wda, a0, wg, k_k, k_a, r_k, ln_w, ln_b, seg, tri, tt):
    b, s, _ = sp.shape
    assert b % N_STREAMS == 0
    rows = b // N_STREAMS
    const2 = lambda i, j: (0, 0)
    row = lambda n: pl.BlockSpec((1, n), const2)
    out = pl.pallas_call(
        functools.partial(_rwkv_kernel, tt=tt),
        grid=(rows, s // tt),
        in_specs=[
            pl.BlockSpec((N_STREAMS, 1, tt, SHIFT_PAD), lambda i, j: (0, i, j, 0)),
            row(WIDTH),
            pl.BlockSpec((LANES, 2 * WIDTH), const2),
            row(WIDTH),
            pl.BlockSpec((2 * LANES, WIDTH), const2),
            row(WIDTH), row(WIDTH), row(WIDTH), row(WIDTH), row(WIDTH),
            pl.BlockSpec((SEG_LANES, SEG_LANES), const2),
            pl.BlockSpec((tt, tt), const2),
        ],
        out_specs=pl.BlockSpec((N_STREAMS, 1, tt, WIDTH), lambda i, j: (0, i, j, 0)),
        out_shape=jax.ShapeDtypeStruct((N_STREAMS, rows, s, WIDTH), BF16),
        scratch_shapes=[
            pltpu.VMEM((N_STREAMS, N_PAIRS, LANES, LANES), F32),
        ],
        compiler_params=pltpu.CompilerParams(
            dimension_semantics=("arbitrary", "arbitrary"), vmem_limit_bytes=VMEM_LIMIT),
        name="rwkv",
    )(sp.reshape(N_STREAMS, rows, s, SHIFT_PAD), w0, wda, a0, wg, k_k, k_a, r_k, ln_w, ln_b,
      seg, tri)
    return out.reshape(b, s, WIDTH)


NEG_BIG = -(2.0 ** 30)
N_BLOCKS_MAX = 16
V_ROWS = HEAD_DIM + 16
LOG2E = 1.4426950408889634
Q_GROUP = 2


def _aligned(i, m):
    return i * m if isinstance(i, int) else pl.multiple_of(i * m, m)


def _moba_kernel(q_ref, k_ref, v_ref, slope_ref, o_ref, kx_ref, vt_ref, qt_ref, bias_ref, *, seq):
    nb = seq // MOBA_BLOCK
    bs = MOBA_BLOCK
    scale = HEAD_DIM ** -0.5
    heads = range(N_HEADS)

    rr = lax.broadcasted_iota(jnp.int32, (bs, LANES), 0)
    ll = lax.broadcasted_iota(jnp.int32, (bs, LANES), 1)
    ones_rows = jnp.ones((V_ROWS - HEAD_DIM, bs), BF16)
    for n in range(nb):
        aug = jnp.where(ll < 4, 1.0,
              jnp.where(ll < 6, rr.astype(F32),
              jnp.where(ll < 8, float(n * bs),
              jnp.where(ll == 8 + n, 1.0, 0.0)))).astype(BF16)
        rows = slice(n * bs, (n + 1) * bs)
        for pr in range(N_PAIRS):
            ln = slice(pr * LANES, (pr + 1) * LANES)
            kx_ref[pr, n, :, 0:LANES] = k_ref[0, rows, ln]
            kx_ref[pr, n, :, LANES:2 * LANES] = aug
            v_t = v_ref[0, rows, ln].T
            for half in range(2):
                vt_ref[2 * pr + half, n] = jnp.concatenate(
                    [v_t[half * HEAD_DIM:(half + 1) * HEAD_DIM], ones_rows], axis=0)

    bi = lax.broadcasted_iota(jnp.int32, (N_BLOCKS_MAX, seq), 0)
    si = lax.broadcasted_iota(jnp.int32, (N_BLOCKS_MAX, seq), 1)
    avg = jnp.where((si >= bi * bs) & (si < (bi + 1) * bs), 1.0 / bs, 0.0).astype(BF16)
    kmean = _dot(avg, k_ref[0])
    km = jnp.concatenate([jnp.broadcast_to(kmean[n:n + 1, :], (N_HEADS, WIDTH))
                          for n in range(N_BLOCKS_MAX)], axis=0)
    row_head = lax.broadcasted_iota(jnp.int32, (LANES, WIDTH), 0) & (N_HEADS - 1)
    lane_head = lax.broadcasted_iota(jnp.int32, (LANES, WIDTH), 1) >> 6
    km_hi, km_lo = _split2(jnp.where(row_head == lane_head, km, 0.0))

    pr_i = lax.broadcasted_iota(jnp.int32, (LANES, LANES), 0)
    pc_i = lax.broadcasted_iota(jnp.int32, (LANES, LANES), 1)
    perm = (pc_i == ((pr_i & (N_BLOCKS_MAX - 1)) << 3) + (pr_i >> 4)).astype(BF16)
    zeros_cnt = jnp.zeros((N_HEADS, bs), F32)

    gates = {}
    for i in range(nb):
        q_blk = q_ref[0, i * bs:(i + 1) * bs, :]
        q_t = [q_blk[:, pr * LANES:(pr + 1) * LANES].T for pr in range(N_PAIRS)]
        for pr in range(N_PAIRS):
            qt_ref[pr, i] = q_t[pr]
        if i <= MOBA_TOP_K:
            bias_ref[i] = jnp.zeros((N_BLOCKS_MAX * N_HEADS, bs), F32)
            continue
        q_tb = jnp.concatenate(q_t, axis=0)
        gates[i] = _dot(km_hi, q_tb) + _dot(km_lo, q_tb)
    biases = {}
    for i, gate in gates.items():
        slabs = [gate[n * N_HEADS:(n + 1) * N_HEADS] for n in range(i)]
        cnt = [zeros_cnt] * i
        for n in range(i):
            for m in range(n):
                m_wins = slabs[m] >= slabs[n]
                cnt[n] = cnt[n] + jnp.where(m_wins, 1.0, 0.0)
                cnt[m] = cnt[m] + jnp.where(m_wins, 0.0, 1.0)
        bias = [jnp.where(cnt[n] < MOBA_TOP_K, 0.0, NEG_BIG) for n in range(i)]
        bias += [zeros_cnt] * (N_BLOCKS_MAX - i)
        biases[i] = jnp.concatenate(bias, axis=0).astype(BF16)
    for i, bias in biases.items():
        bias_ref[i] = _dot(perm, bias)

    r128 = lax.broadcasted_iota(jnp.int32, (LANES, 1), 0)
    row_masks = (r128 < HEAD_DIM, r128 >= HEAD_DIM)
    r8 = lax.broadcasted_iota(jnp.int32, (8, bs), 0)
    tl = lax.broadcasted_iota(jnp.int32, (8, bs), 1).astype(F32)
    key_row = lax.broadcasted_iota(jnp.int32, (bs, bs), 0)
    qry_col = lax.broadcasted_iota(jnp.int32, (bs, bs), 1)
    causal = key_row <= qry_col
    zeros_pad = jnp.zeros((LANES - 8 - N_BLOCKS_MAX, bs), F32)

    def q_block(i, odd):
        start = _aligned(i, bs)
        q_t = [qt_ref[pr, i].astype(F32) for pr in range(N_PAIRS)]
        bias = bias_ref[i]
        t0 = jnp.asarray(i * bs, F32)

        q_aug = []
        for h in heads:
            c = slope_ref[h:h + 1, :] * LOG2E
            c_hi = c.astype(BF16).astype(F32)
            ctl = c * tl
            ctl_hi = ctl.astype(BF16).astype(F32)
            ct0 = c * t0
            ct0_hi = ct0.astype(BF16).astype(F32)
            aug8 = jnp.where(r8 == 0, -ctl_hi,
                   jnp.where(r8 == 1, ctl_hi - ctl,
                   jnp.where(r8 == 2, -ct0_hi,
                   jnp.where(r8 == 3, ct0_hi - ct0,
                   jnp.where((r8 == 4) | (r8 == 6), c_hi, c - c_hi)))))
            top = jnp.where(row_masks[h % 2], q_t[h // 2], 0.0) * (scale * LOG2E)
            sel = bias[h * N_BLOCKS_MAX:(h + 1) * N_BLOCKS_MAX]
            q_aug.append(jnp.concatenate([top, aug8, sel, zeros_pad], axis=0).astype(BF16))

        def scores(j):
            kx = [kx_ref[pr, j] for pr in range(N_PAIRS)]
            return [_dot(kx[h // 2], q_aug[h]) for h in heads]

        def weighted_values(j, ps):
            return [_dot(vt_ref[h, j], ps[h]) for h in heads]

        s_own = scores(i)
        yield
        m0, p0 = [], []
        for h in heads:
            sh = jnp.where(causal, s_own[h], NEG_BIG)
            m0.append(jnp.max(sh, axis=0, keepdims=True))
            p0.append(jnp.exp2(sh - m0[h]).astype(BF16))
        pv = weighted_values(i, p0)
        yield
        acc0 = [pv[h][0:HEAD_DIM] for h in heads]
        l0 = [pv[h][HEAD_DIM:HEAD_DIM + 1] for h in heads]

        def step(js, st):
            m, l, acc = (list(x) for x in st)
            s_all = [scores(j) for j in js]
            ps, alphas = [], []
            for s in s_all:
                p_blk, a_blk = [], []
                for h in heads:
                    mh = jnp.maximum(m[h], jnp.max(s[h], axis=0, keepdims=True))
                    a_blk.append(jnp.exp2(m[h] - mh))
                    p_blk.append(jnp.exp2(s[h] - mh).astype(BF16))
                    m[h] = mh
                ps.append(p_blk)
                alphas.append(a_blk)
            pvs = [weighted_values(j, p_blk) for j, p_blk in zip(js, ps)]
            for a_blk, pv in zip(alphas, pvs):
                for h in heads:
                    acc[h] = a_blk[h] * acc[h] + pv[h][0:HEAD_DIM]
                    l[h] = a_blk[h] * l[h] + pv[h][HEAD_DIM:HEAD_DIM + 1]
            return tuple(tuple(x) for x in (m, l, acc))

        st = (tuple(m0), tuple(l0), tuple(acc0))
        if odd:
            st = step([i - 1], st)
        _, l, acc = lax.fori_loop(0, i >> 1, lambda t, st: step([2 * t, 2 * t + 1], st), st)
        pairs = []
        for pr in range(N_PAIRS):
            a, b = 2 * pr, 2 * pr + 1
            out_t = jnp.concatenate([acc[a] / l[a], acc[b] / l[b]], axis=0)
            pairs.append(out_t.astype(BF16).T)
        o_ref[0, pl.ds(start, bs), :] = jnp.concatenate(pairs, axis=1)

    def q_block_group(u, carry):
        live = [q_block(Q_GROUP * u + g, g % 2 == 1) for g in range(Q_GROUP)]
        done = object()
        while live:
            live = [gen for gen in live if next(gen, done) is not done]
        return carry

    assert nb % Q_GROUP == 0 and Q_GROUP % 2 == 0
    lax.fori_loop(0, nb // Q_GROUP, q_block_group, 0)


def _moba(qkv, slopes):
    b, s, _ = qkv.shape
    assert s % MOBA_BLOCK == 0 and s // MOBA_BLOCK <= N_BLOCKS_MAX
    nb = s // MOBA_BLOCK
    col = lambda c: pl.BlockSpec((1, s, WIDTH), lambda i: (i, 0, c))
    return pl.pallas_call(
        functools.partial(_moba_kernel, seq=s),
        grid=(b,),
        in_specs=[col(0), col(1), col(2), pl.BlockSpec((N_HEADS, MOBA_BLOCK), lambda i: (0, 0))],
        out_specs=pl.BlockSpec((1, s, WIDTH), lambda i: (i, 0, 0)),
        out_shape=jax.ShapeDtypeStruct((b, s, WIDTH), BF16),
        scratch_shapes=[
            pltpu.VMEM((N_PAIRS, nb, MOBA_BLOCK, 2 * LANES), BF16),
            pltpu.VMEM((N_HEADS, nb, V_ROWS, MOBA_BLOCK), BF16),
            pltpu.VMEM((N_PAIRS, nb, LANES, MOBA_BLOCK), BF16),
            pltpu.VMEM((nb, N_BLOCKS_MAX * N_HEADS, MOBA_BLOCK), F32),
        ],
        compiler_params=pltpu.CompilerParams(
            dimension_semantics=("arbitrary",), vmem_limit_bytes=MOBA_VMEM_LIMIT),
        name="moba",
    )(qkv, qkv, qkv, slopes)


FFN_CHUNKS = ((0, 1024), (1024, 1024), (2048, 768))


POST_STREAMS = 2


def _post_rows(rows, x_ref, ya_ref, yb_ref, gate_ref, wua_ref, wub_ref, wo_ref, gffn_ref,
               wfi_ref, wfo_ref, gfin_ref, o_ref):
    y_a = _dot(ya_ref[rows, :], wua_ref[...])
    y_b = _dot(yb_ref[rows, :], wub_ref[...])
    yield
    ga = jax.nn.sigmoid(gate_ref[rows, 0:D_MODEL].astype(F32))
    gb = jax.nn.sigmoid(gate_ref[rows, D_MODEL:2 * D_MODEL].astype(F32))
    mixed = (ga * y_a + gb * y_b).astype(BF16)
    x1 = x_ref[rows, :] + _dot(mixed, wo_ref[...])
    yield
    h = _rms(x1, gffn_ref[...]).astype(BF16)
    acc = x1
    for off, n in FFN_CHUNKS:
        gg = _dot(h, wfi_ref[:, off:off + n])
        uu = _dot(h, wfi_ref[:, FFN_HIDDEN + off:FFN_HIDDEN + off + n])
        yield
        act = (gg * jax.nn.sigmoid(gg) * uu).astype(BF16)
        acc = acc + _dot(act, wfo_ref[off:off + n, :])
        yield
    o_ref[rows, :] = _rms(acc, gfin_ref[...])


def _post_kernel(*refs):
    tm = refs[0].shape[0]
    part = tm // POST_STREAMS
    live = [_post_rows(slice(k * part, (k + 1) * part), *refs) for k in range(POST_STREAMS)]
    done = object()
    while live:
        live = [gen for gen in live if next(gen, done) is not done]


def _post(x2d, ya, yb, gates, wua, wub, wo, gffn, wfi, wfo, gfin, tm):
    t = x2d.shape[0]
    const = lambda i: (0, 0)
    tile = lambda n: pl.BlockSpec((tm, n), lambda i: (i, 0))
    weight = lambda a: pl.BlockSpec(a.shape, const, pipeline_mode=pl.Buffered(1))
    return pl.pallas_call(
        _post_kernel,
        grid=(t // tm,),
        in_specs=[tile(D_MODEL), tile(WIDTH), tile(WIDTH), tile(GATES),
                  weight(wua), weight(wub), weight(wo), weight(gffn),
                  weight(wfi), weight(wfo), weight(gfin)],
        out_specs=tile(D_MODEL),
        out_shape=jax.ShapeDtypeStruct((t, D_MODEL), F32),
        compiler_params=pltpu.CompilerParams(
            dimension_semantics=("arbitrary",), vmem_limit_bytes=VMEM_LIMIT),
        name="post",
    )(x2d, ya, yb, gates, wua, wub, wo, gffn, wfi, wfo, gfin)


def _layer(x, norm_mix_g, w_in, mu_shift, w0, w_decay_up, a0, w_iclr_up, w_gate_up,
           k_k, k_a, r_k, ln_x_w, ln_x_b, w_up_a, w_up_b, w_o, norm_ffn_g,
           w_ffn_in, w_ffn_out, out_g):
    b, s, _ = x.shape
    t = b * s
    x2d = x.reshape(t, D_MODEL)

    pad = SHIFT_PAD - SHIFT_WIDTH
    w_sp = jnp.pad(w_in[:, :SHIFT_WIDTH].astype(BF16), ((0, 0), (0, pad)))
    w_qkv = w_in[:, SHIFT_WIDTH:SHIFT_WIDTH + QKV_B].astype(BF16)
    w_gate = w_in[:, SHIFT_WIDTH + QKV_B:].astype(BF16)
    mu = jnp.pad(mu_shift, (0, pad)).reshape(1, SHIFT_PAD)
    wda = jnp.zeros((LANES, 2 * WIDTH), F32)
    wda = wda.at[:DECAY_LORA, :WIDTH].set(w_decay_up).at[DECAY_LORA:, WIDTH:].set(w_iclr_up)
    wg = jnp.pad(w_gate_up, ((0, 2 * LANES - GATE_LORA), (0, 0)))
    row = lambda a: a.reshape(1, -1)

    tt = TIME_TILE
    hid = jnp.arange(SEG_LANES) // HEAD_DIM
    seg = (hid[:, None] == hid[None, :]).astype(BF16)
    ti = jnp.arange(tt)
    tri = ((ti[:, None] >= ti[None, :]) & (ti[:, None] // CHUNK == ti[None, :] // CHUNK)).astype(BF16)

    sp, qkv, gates = _proj(x2d, row(norm_mix_g), w_sp, w_qkv, w_gate, mu, tm=TOKEN_TILE,
                            seq=s)
    ya = _rwkv(sp.reshape(b, s, SHIFT_PAD), row(w0), wda.astype(BF16), row(a0),
               wg.astype(BF16), row(k_k), row(k_a), row(r_k), row(ln_x_w), row(ln_x_b),
               seg, tri, tt)
    slopes = jnp.exp2(-8.0 * jnp.arange(1, N_HEADS + 1, dtype=F32) / N_HEADS)
    slopes = jnp.broadcast_to(slopes[:, None], (N_HEADS, MOBA_BLOCK))
    yb = _moba(qkv.reshape(b, s, QKV_B), slopes)
    out = _post(x2d, ya.reshape(t, WIDTH), yb.reshape(t, WIDTH), gates,
                w_up_a.astype(BF16), w_up_b.astype(BF16), w_o.astype(BF16), row(norm_ffn_g),
                w_ffn_in.astype(BF16), w_ffn_out.astype(BF16), row(out_g), tm=TOKEN_TILE)
    return out.reshape(b, s, D_MODEL)


def kernel(x, norm_mix_g, w_in, mu_shift, w0, w_decay_up, a0, w_iclr_up, w_gate_up, k_k, k_a, r_k, ln_x_w, ln_x_b, w_up_a, w_up_b, w_o, norm_ffn_g, w_ffn_in, w_ffn_out, norm_final_g):
    depth = w_in.shape[0]
    assert depth == 1, "the fused post kernel applies the final norm after the only layer"
    return _layer(x, norm_mix_g[0], w_in[0], mu_shift[0], w0[0], w_decay_up[0], a0[0],
                  w_iclr_up[0], w_gate_up[0], k_k[0], k_a[0], r_k[0].reshape(-1), ln_x_w[0],
                  ln_x_b[0], w_up_a[0], w_up_b[0], w_o[0], norm_ffn_g[0], w_ffn_in[0],
                  w_ffn_out[0], norm_final_g)
```

```python
import functools

import jax
import jax.numpy as jnp
from jax import lax
from jax.experimental import pallas as pl
from jax.experimental.pallas import tpu as pltpu

F32 = jnp.float32
BF16 = jnp.bfloat16

D_MODEL = 1024
HEAD_DIM = 64
N_HEADS = 8
WIDTH = N_HEADS * HEAD_DIM
N_PAIRS = N_HEADS // 2
LANES = 128
DECAY_LORA = 64
ICLR_LORA = 64
GATE_LORA = 160
GN_EPS = 64e-5
EXP_NEG_HALF = 0.6065306597126334
RMS_EPS = 1e-6
MOBA_BLOCK = 256
MOBA_TOP_K = 3
FFN_HIDDEN = 2816
SHIFT_WIDTH = 3 * WIDTH + DECAY_LORA + ICLR_LORA + GATE_LORA
SHIFT_PAD = 1920
QKV_B = 3 * WIDTH
GATES = 2 * D_MODEL
PROJ_PAD = SHIFT_PAD + QKV_B + GATES

CHUNK = 64
TOKEN_TILE = 512
TIME_TILE = 256
VMEM_LIMIT = 56 * 1024 * 1024
MOBA_VMEM_LIMIT = 60 * 1024 * 1024


def _dot(a, b):
    return jnp.dot(a, b, preferred_element_type=F32)


def _dot_nt(a, b):
    return lax.dot_general(a, b, (((1,), (1,)), ((), ())), preferred_element_type=F32)


def _sigmoid(x):
    return 0.5 * jnp.tanh(0.5 * x) + 0.5


def _rms(x, g):
    return x * lax.rsqrt(jnp.mean(x * x, axis=-1, keepdims=True) + RMS_EPS) * g


def _proj_kernel(x_ref, g_ref, wsp_ref, wqkv_ref, wgate_ref, mu_ref, sp_ref, qkv_ref, gate_ref,
                 carry_ref, *, tiles_per_seq):
    tm = x_ref.shape[0]
    step = pl.program_id(0)

    @pl.when(step == 0)
    def _():
        carry_ref[...] = jnp.zeros_like(carry_ref)

    h = _rms(x_ref[...], g_ref[...]).astype(BF16)
    qkv_ref[...] = _dot(h, wqkv_ref[...]).astype(BF16)
    gate_ref[...] = _dot(h, wgate_ref[...]).astype(BF16)

    p = _dot(h, wsp_ref[...])
    seq_start = lax.rem(step, tiles_per_seq) == 0
    carry = jnp.where(seq_start, 0.0, carry_ref[0:1, :])
    row = lax.broadcasted_iota(jnp.int32, (tm, 1), 0)
    prev = jnp.where(row == 0, carry, pltpu.roll(p, 1, 0))
    carry_ref[0:1, :] = p[tm - 1:tm, :]
    sp_ref[...] = p + mu_ref[...] * (prev - p)


def _proj(x2d, g, w_sp, w_qkv, w_gate, mu, tm, seq):
    t = x2d.shape[0]
    const = lambda i: (0, 0)
    weight = lambda n: pl.BlockSpec((D_MODEL, n), const, pipeline_mode=pl.Buffered(1))
    return pl.pallas_call(
        functools.partial(_proj_kernel, tiles_per_seq=seq // tm),
        grid=(t // tm,),
        in_specs=[
            pl.BlockSpec((tm, D_MODEL), lambda i: (i, 0)),
            pl.BlockSpec((1, D_MODEL), const),
            weight(SHIFT_PAD), weight(QKV_B), weight(GATES),
            pl.BlockSpec((1, SHIFT_PAD), const),
        ],
        scratch_shapes=[pltpu.VMEM((8, SHIFT_PAD), F32)],
        out_specs=[
            pl.BlockSpec((tm, SHIFT_PAD), lambda i: (i, 0)),
            pl.BlockSpec((tm, QKV_B), lambda i: (i, 0)),
            pl.BlockSpec((tm, GATES), lambda i: (i, 0)),
        ],
        out_shape=[
            jax.ShapeDtypeStruct((t, SHIFT_PAD), F32),
            jax.ShapeDtypeStruct((t, QKV_B), BF16),
            jax.ShapeDtypeStruct((t, GATES), BF16),
        ],
        compiler_params=pltpu.CompilerParams(
            dimension_semantics=("arbitrary",), vmem_limit_bytes=VMEM_LIMIT),
        name="proj",
    )(x2d, g, w_sp, w_qkv, w_gate, mu)


def _split2(x):
    hi = x.astype(BF16)
    lo = (x - hi.astype(F32)).astype(BF16)
    return hi, lo


SEG_LANES = 2 * LANES


def _segsum(x, seg):
    xb = x.astype(BF16)
    return jnp.concatenate(
        [_dot(xb[:, o:o + SEG_LANES], seg) for o in range(0, x.shape[1], SEG_LANES)], axis=1)


N_STREAMS = 2


def _rwkv_kernel(sp_ref, w0_ref, wda_ref, a0_ref, wg_ref, kk_ref, ka_ref, rk_ref,
                 lnw_ref, lnb_ref, seg_ref, tri_ref, o_ref, h_ref, *, tt):
    @pl.when(pl.program_id(1) == 0)
    def _():
        h_ref[...] = jnp.zeros_like(h_ref)

    params = (w0_ref, wda_ref, a0_ref, wg_ref, kk_ref, ka_ref, rk_ref, lnw_ref, lnb_ref,
              seg_ref, tri_ref)
    live = [_rwkv_stream(k, sp_ref, o_ref, h_ref, *params, tt=tt) for k in range(N_STREAMS)]
    done = object()
    while live:
        live = [gen for gen in live if next(gen, done) is not done]


def _rwkv_stream(k_seq, sp_ref, o_ref, h_ref, w0_ref, wda_ref, a0_ref, wg_ref, kk_ref, ka_ref,
                 rk_ref, lnw_ref, lnb_ref, seg_ref, tri_ref, *, tt):
    n_chunks = tt // CHUNK
    xs = sp_ref[k_seq, 0]
    r = xs[:, 0:WIDTH]
    k = xs[:, WIDTH:2 * WIDTH]
    v = xs[:, 2 * WIDTH:3 * WIDTH]
    da = xs[:, 3 * WIDTH:3 * WIDTH + LANES]
    gd = xs[:, 3 * WIDTH + LANES:SHIFT_PAD]
    lane = lax.broadcasted_iota(jnp.int32, (1, LANES), 1)
    lo_half = lane < HEAD_DIM
    da_act = jnp.where(lo_half, jnp.tanh(da), da).astype(BF16)
    lora = _dot(da_act, wda_ref[...])
    g = _dot(_sigmoid(gd).astype(BF16), wg_ref[...])

    logdec = -EXP_NEG_HALF * _sigmoid(w0_ref[...] + lora[:, 0:WIDTH])
    a = _sigmoid(a0_ref[...] + lora[:, WIDTH:2 * WIDTH])

    seg = seg_ref[...]
    kkf = k * kk_ref[...]
    kk_sq = _segsum(kkf * kkf, seg)
    yield
    kk = kkf * lax.rsqrt(jnp.maximum(kk_sq, 1e-24))
    k2 = k * (1.0 + (a - 1.0) * ka_ref[...])
    bonus = _segsum(r * k2 * rk_ref[...], seg) * v

    l1, l2 = _split2(logdec)
    tri = tri_ref[...]
    cum = _dot(tri, l1) + _dot(tri, l2)
    yield

    kka = kk * a
    p_inv = jnp.exp(-cum)
    a_t = -kk * jnp.exp(cum - logdec)
    r_t = r * jnp.exp(cum)
    b_t = kka * p_inv
    k_t = k2 * p_inv

    r64 = lax.broadcasted_iota(jnp.int32, (CHUNK, LANES), 0)
    c64 = lax.broadcasted_iota(jnp.int32, (CHUNK, LANES), 1) & (CHUNK - 1)
    strict = c64 < r64
    incl = c64 <= r64
    eye_side = (c64 == r64).astype(F32)
    r128 = lax.broadcasted_iota(jnp.int32, (LANES, LANES), 0)
    c128 = lax.broadcasted_iota(jnp.int32, (LANES, LANES), 1)
    eye128 = (r128 == c128).astype(F32)
    same_head = (r128 >> 6) == (c128 >> 6)
    hi_half = jnp.logical_not(lo_half)
    zeros_c = jnp.zeros((CHUNK, LANES), F32)

    def _stack2(zz):
        zb = zz.astype(BF16)
        return jnp.concatenate([jnp.where(lo_half, zb, 0), jnp.where(hi_half, zb, 0)], axis=0)

    chains = [(c, pr) for c in range(n_chunks) for pr in range(N_PAIRS)]

    def _piece(arr, c, pr):
        return arr[c * CHUNK:(c + 1) * CHUNK, pr * LANES:(pr + 1) * LANES]

    x_ab, x_ak, x_rb, x_rk = {}, {}, {}, {}
    for ch in chains:
        lq = jnp.concatenate([_piece(a_t, *ch), _piece(r_t, *ch)], axis=0).astype(BF16)
        rhs = jnp.concatenate([_stack2(_piece(b_t, *ch)), _stack2(_piece(k_t, *ch))], axis=0)
        sc = _dot_nt(lq, rhs)
        x_ab[ch] = jnp.where(strict, sc[0:CHUNK, 0:LANES], 0.0)
        x_ak[ch] = jnp.where(strict, sc[0:CHUNK, LANES:2 * LANES], 0.0)
        x_rb[ch] = jnp.where(incl, sc[CHUNK:2 * CHUNK, 0:LANES], 0.0)
        x_rk[ch] = jnp.where(incl, sc[CHUNK:2 * CHUNK, LANES:2 * LANES], 0.0)
    yield

    x_t = {ch: eye_side + x_ab[ch] for ch in chains}
    x_pw = {ch: _dot(x_ab[ch].astype(BF16), _stack2(x_ab[ch])) for ch in chains}
    av = {ch: _dot(x_ak[ch].astype(BF16), _stack2(_piece(v, *ch))) for ch in chains}
    yield
    for _ in range(4):
        for ch in chains:
            both = jnp.concatenate([x_t[ch], x_pw[ch]], axis=0).astype(BF16)
            out = _dot(both, _stack2(x_pw[ch]))
            x_t[ch] = x_t[ch] + out[0:CHUNK]
            x_pw[ch] = out[CHUNK:2 * CHUNK]
        yield
    for ch in chains:
        x_t[ch] = x_t[ch] + _dot(x_t[ch].astype(BF16), _stack2(x_pw[ch]))
    yield

    tw = {}
    for ch in chains:
        rhs = jnp.concatenate([_stack2(_piece(a_t, *ch)), _stack2(av[ch])], axis=1)
        tw[ch] = _dot(x_t[ch].astype(BF16), rhs)
    yield
    qeff, yloc, phi, psi = {}, {}, {}, {}
    for ch in chains:
        ta, w_loc, v_p = tw[ch][:, 0:LANES], tw[ch][:, LANES:2 * LANES], _piece(v, *ch)
        rhs = jnp.concatenate(
            [jnp.concatenate([_stack2(ta), _stack2(w_loc)], axis=1),
             jnp.concatenate([jnp.zeros((LANES, LANES), BF16), _stack2(v_p)], axis=1)], axis=0)
        ag = _dot(jnp.concatenate([x_rb[ch], x_rk[ch]], axis=1).astype(BF16), rhs)
        qeff[ch] = _piece(r_t, *ch) + ag[:, 0:LANES]
        yloc[ch] = ag[:, LANES:2 * LANES]
        cum_last = _piece(cum, *ch)[CHUNK - 1:CHUNK, :]
        p_end = jnp.exp(cum_last)
        kk_t = jnp.concatenate([_piece(b_t, *ch) * p_end, _piece(k_t, *ch) * p_end],
                               axis=0).astype(BF16).T
        rhs = jnp.concatenate(
            [tw[ch], jnp.concatenate([zeros_c, v_p], axis=1)], axis=0).astype(BF16)
        pp = _dot(kk_t, rhs)
        phi[ch] = eye128 * p_end + jnp.where(same_head, pp[:, 0:LANES], 0.0)
        psi[ch] = jnp.where(same_head, pp[:, LANES:2 * LANES], 0.0)
    yield

    y_rows = []
    states = [h_ref[k_seq, pr] for pr in range(N_PAIRS)]
    for c in range(n_chunks):
        y_c = []
        for pr in range(N_PAIRS):
            ch = (c, pr)
            lhs = jnp.concatenate([qeff[ch], phi[ch]], axis=0).astype(BF16)
            out = _dot(lhs, states[pr].astype(BF16))
            y_c.append(out[0:CHUNK] + yloc[ch])
            states[pr] = out[CHUNK:CHUNK + LANES] + psi[ch]
        y_rows.append(jnp.concatenate(y_c, axis=1))
        yield
    for pr in range(N_PAIRS):
        h_ref[k_seq, pr] = states[pr]
    y = jnp.concatenate(y_rows, axis=0)

    mean = _segsum(y, seg) * (1.0 / HEAD_DIM)
    yield
    d = y - mean
    var = _segsum(d * d, seg) * (1.0 / HEAD_DIM)
    yield
    yn = d * lax.rsqrt(var + GN_EPS) * lnw_ref[...] + lnb_ref[...]
    o_ref[k_seq, 0] = ((yn + bonus) * g).astype(BF16)


def _rwkv(sp, w0, wda, a0, wg, k_k, k_a, r_k, ln_w, ln_b, seg, tri, tt):
    b, s, _ = sp.shape
    assert b % N_STREAMS == 0
    rows = b // N_STREAMS
    const2 = lambda i, j: (0, 0)
    row = lambda n: pl.BlockSpec((1, n), const2)
    out = pl.pallas_call(
        functools.partial(_rwkv_kernel, tt=tt),
        grid=(rows, s // tt),
        in_specs=[
            pl.BlockSpec((N_STREAMS, 1, tt, SHIFT_PAD), lambda i, j: (0, i, j, 0)),
            row(WIDTH),
            pl.BlockSpec((LANES, 2 * WIDTH), const2),
            row(WIDTH),
            pl.BlockSpec((2 * LANES, WIDTH), const2),
            row(WIDTH), row(WIDTH), row(WIDTH), row(WIDTH), row(WIDTH),
            pl.BlockSpec((SEG_LANES, SEG_LANES), const2),
            pl.BlockSpec((tt, tt), const2),
        ],
        out_specs=pl.BlockSpec((N_STREAMS, 1, tt, WIDTH), lambda i, j: (0, i, j, 0)),
        out_shape=jax.ShapeDtypeStruct((N_STREAMS, rows, s, WIDTH), BF16),
        scratch_shapes=[
            pltpu.VMEM((N_STREAMS, N_PAIRS, LANES, LANES), F32),
        ],
        compiler_params=pltpu.CompilerParams(
            dimension_semantics=("arbitrary", "arbitrary"), vmem_limit_bytes=VMEM_LIMIT),
        name="rwkv",
    )(sp.reshape(N_STREAMS, rows, s, SHIFT_PAD), w0, wda, a0, wg, k_k, k_a, r_k, ln_w, ln_b,
      seg, tri)
    return out.reshape(b, s, WIDTH)


NEG_BIG = -(2.0 ** 30)
N_BLOCKS_MAX = 16
V_ROWS = HEAD_DIM + 16
LOG2E = 1.4426950408889634
Q_GROUP = 2


def _aligned(i, m):
    return i * m if isinstance(i, int) else pl.multiple_of(i * m, m)


def _moba_kernel(q_ref, k_ref, v_ref, slope_ref, o_ref, kx_ref, vt_ref, qt_ref, bias_ref, *, seq):
    nb = seq // MOBA_BLOCK
    bs = MOBA_BLOCK
    scale = HEAD_DIM ** -0.5
    heads = range(N_HEADS)

    rr = lax.broadcasted_iota(jnp.int32, (bs, LANES), 0)
    ll = lax.broadcasted_iota(jnp.int32, (bs, LANES), 1)
    ones_rows = jnp.ones((V_ROWS - HEAD_DIM, bs), BF16)
    for n in range(nb):
        aug = jnp.where(ll < 4, 1.0,
              jnp.where(ll < 6, rr.astype(F32),
              jnp.where(ll < 8, float(n * bs),
              jnp.where(ll == 8 + n, 1.0, 0.0)))).astype(BF16)
        rows = slice(n * bs, (n + 1) * bs)
        for pr in range(N_PAIRS):
            ln = slice(pr * LANES, (pr + 1) * LANES)
            kx_ref[pr, n, :, 0:LANES] = k_ref[0, rows, ln]
            kx_ref[pr, n, :, LANES:2 * LANES] = aug
            v_t = v_ref[0, rows, ln].T
            for half in range(2):
                vt_ref[2 * pr + half, n] = jnp.concatenate(
                    [v_t[half * HEAD_DIM:(half + 1) * HEAD_DIM], ones_rows], axis=0)

    bi = lax.broadcasted_iota(jnp.int32, (N_BLOCKS_MAX, seq), 0)
    si = lax.broadcasted_iota(jnp.int32, (N_BLOCKS_MAX, seq), 1)
    avg = jnp.where((si >= bi * bs) & (si < (bi + 1) * bs), 1.0 / bs, 0.0).astype(BF16)
    kmean = _dot(avg, k_ref[0])
    km = jnp.concatenate([jnp.broadcast_to(kmean[n:n + 1, :], (N_HEADS, WIDTH))
                          for n in range(N_BLOCKS_MAX)], axis=0)
    row_head = lax.broadcasted_iota(jnp.int32, (LANES, WIDTH), 0) & (N_HEADS - 1)
    lane_head = lax.broadcasted_iota(jnp.int32, (LANES, WIDTH), 1) >> 6
    km_hi, km_lo = _split2(jnp.where(row_head == lane_head, km, 0.0))

    pr_i = lax.broadcasted_iota(jnp.int32, (LANES, LANES), 0)
    pc_i = lax.broadcasted_iota(jnp.int32, (LANES, LANES), 1)
    perm = (pc_i == ((pr_i & (N_BLOCKS_MAX - 1)) << 3) + (pr_i >> 4)).astype(BF16)
    zeros_cnt = jnp.zeros((N_HEADS, bs), F32)

    gates = {}
    for i in range(nb):
        q_blk = q_ref[0, i * bs:(i + 1) * bs, :]
        q_t = [q_blk[:, pr * LANES:(pr + 1) * LANES].T for pr in range(N_PAIRS)]
        for pr in range(N_PAIRS):
            qt_ref[pr, i] = q_t[pr]
        if i <= MOBA_TOP_K:
            bias_ref[i] = jnp.zeros((N_BLOCKS_MAX * N_HEADS, bs), BF16)
            continue
        q_tb = jnp.concatenate(q_t, axis=0)
        gates[i] = _dot(km_hi, q_tb) + _dot(km_lo, q_tb)
    biases = {}
    for i, gate in gates.items():
        slabs = [gate[n * N_HEADS:(n + 1) * N_HEADS] for n in range(i)]
        cnt = [zeros_cnt] * i
        for n in range(i):
            for m in range(n):
                m_wins = slabs[m] >= slabs[n]
                cnt[n] = cnt[n] + jnp.where(m_wins, 1.0, 0.0)
                cnt[m] = cnt[m] + jnp.where(m_wins, 0.0, 1.0)
        bias = [jnp.where(cnt[n] < MOBA_TOP_K, 0.0, NEG_BIG) for n in range(i)]
        bias += [zeros_cnt] * (N_BLOCKS_MAX - i)
        biases[i] = jnp.concatenate(bias, axis=0).astype(BF16)
    for i, bias in biases.items():
        bias_ref[i] = _dot(perm, bias).astype(BF16)

    r128 = lax.broadcasted_iota(jnp.int32, (LANES, 1), 0)
    row_masks = (r128 < HEAD_DIM, r128 >= HEAD_DIM)
    r8 = lax.broadcasted_iota(jnp.int32, (8, bs), 0)
    tl = lax.broadcasted_iota(jnp.int32, (8, bs), 1).astype(F32)
    key_row = lax.broadcasted_iota(jnp.int32, (bs, bs), 0)
    qry_col = lax.broadcasted_iota(jnp.int32, (bs, bs), 1)
    causal = key_row <= qry_col
    zeros_pad = jnp.zeros((LANES - 8 - N_BLOCKS_MAX, bs), F32)

    def q_block(i, odd):
        start = _aligned(i, bs)
        q_t = [qt_ref[pr, i].astype(F32) for pr in range(N_PAIRS)]
        bias = bias_ref[i].astype(F32)
        t0 = jnp.asarray(i * bs, F32)

        q_aug = []
        for h in heads:
            c = slope_ref[h:h + 1, :] * LOG2E
            c_hi = c.astype(BF16).astype(F32)
            ctl = c * tl
            ctl_hi = ctl.astype(BF16).astype(F32)
            ct0 = c * t0
            ct0_hi = ct0.astype(BF16).astype(F32)
            aug8 = jnp.where(r8 == 0, -ctl_hi,
                   jnp.where(r8 == 1, ctl_hi - ctl,
                   jnp.where(r8 == 2, -ct0_hi,
                   jnp.where(r8 == 3, ct0_hi - ct0,
                   jnp.where((r8 == 4) | (r8 == 6), c_hi, c - c_hi)))))
            top = jnp.where(row_masks[h % 2], q_t[h // 2], 0.0) * (scale * LOG2E)
            sel = bias[h * N_BLOCKS_MAX:(h + 1) * N_BLOCKS_MAX]
            q_aug.append(jnp.concatenate([top, aug8, sel, zeros_pad], axis=0).astype(BF16))

        def scores(j):
            kx = [kx_ref[pr, j] for pr in range(N_PAIRS)]
            return [_dot(kx[h // 2], q_aug[h]) for h in heads]

        def weighted_values(j, ps):
            return [_dot(vt_ref[h, j], ps[h]) for h in heads]

        s_own = scores(i)
        s_left = scores(i - 1) if odd else None
        yield
        m0, p0 = [], []
        for h in heads:
            sh = jnp.where(causal, s_own[h], NEG_BIG)
            m0.append(jnp.max(sh, axis=0, keepdims=True))
            p0.append(jnp.exp2(sh - m0[h]).astype(BF16))
        pv = weighted_values(i, p0)
        if odd:
            alpha1, p1 = [], []
            for h in heads:
                mh = jnp.maximum(m0[h], jnp.max(s_left[h], axis=0, keepdims=True))
                alpha1.append(jnp.exp2(m0[h] - mh))
                p1.append(jnp.exp2(s_left[h] - mh).astype(BF16))
                m0[h] = mh
            pv1 = weighted_values(i - 1, p1)
        yield
        acc0 = [pv[h][0:HEAD_DIM] for h in heads]
        l0 = [pv[h][HEAD_DIM:HEAD_DIM + 1] for h in heads]
        if odd:
            acc0 = [alpha1[h] * acc0[h] + pv1[h][0:HEAD_DIM] for h in heads]
            l0 = [alpha1[h] * l0[h] + pv1[h][HEAD_DIM:HEAD_DIM + 1] for h in heads]

        def step(js, st):
            m, l, acc = (list(x) for x in st)
            s_all = [scores(j) for j in js]
            ps, alphas = [], []
            for s in s_all:
                p_blk, a_blk = [], []
                for h in heads:
                    mh = jnp.maximum(m[h], jnp.max(s[h], axis=0, keepdims=True))
                    a_blk.append(jnp.exp2(m[h] - mh))
                    p_blk.append(jnp.exp2(s[h] - mh).astype(BF16))
                    m[h] = mh
                ps.append(p_blk)
                alphas.append(a_blk)
            pvs = [weighted_values(j, p_blk) for j, p_blk in zip(js, ps)]
            for a_blk, pv in zip(alphas, pvs):
                for h in heads:
                    acc[h] = a_blk[h] * acc[h] + pv[h][0:HEAD_DIM]
                    l[h] = a_blk[h] * l[h] + pv[h][HEAD_DIM:HEAD_DIM + 1]
            return tuple(tuple(x) for x in (m, l, acc))

        st = (tuple(m0), tuple(l0), tuple(acc0))
        _, l, acc = lax.fori_loop(0, i >> 1, lambda t, st: step([2 * t, 2 * t + 1], st), st)
        pairs = []
        for pr in range(N_PAIRS):
            a, b = 2 * pr, 2 * pr + 1
            out_t = jnp.concatenate([acc[a] / l[a], acc[b] / l[b]], axis=0)
            pairs.append(out_t.astype(BF16).T)
        o_ref[0, pl.ds(start, bs), :] = jnp.concatenate(pairs, axis=1)

    def q_block_group(u, carry):
        live = [q_block(Q_GROUP * u + g, g % 2 == 1) for g in range(Q_GROUP)]
        done = object()
        while live:
            live = [gen for gen in live if next(gen, done) is not done]
        return carry

    assert nb % Q_GROUP == 0 and Q_GROUP % 2 == 0
    lax.fori_loop(0, nb // Q_GROUP, q_block_group, 0)


def _moba(qkv, slopes):
    b, s, _ = qkv.shape
    assert s % MOBA_BLOCK == 0 and s // MOBA_BLOCK <= N_BLOCKS_MAX
    nb = s // MOBA_BLOCK
    col = lambda c: pl.BlockSpec((1, s, WIDTH), lambda i: (i, 0, c))
    return pl.pallas_call(
        functools.partial(_moba_kernel, seq=s),
        grid=(b,),
        in_specs=[col(0), col(1), col(2), pl.BlockSpec((N_HEADS, MOBA_BLOCK), lambda i: (0, 0))],
        out_specs=pl.BlockSpec((1, s, WIDTH), lambda i: (i, 0, 0)),
        out_shape=jax.ShapeDtypeStruct((b, s, WIDTH), BF16),
        scratch_shapes=[
            pltpu.VMEM((N_PAIRS, nb, MOBA_BLOCK, 2 * LANES), BF16),
            pltpu.VMEM((N_HEADS, nb, V_ROWS, MOBA_BLOCK), BF16),
            pltpu.VMEM((N_PAIRS, nb, LANES, MOBA_BLOCK), BF16),
            pltpu.VMEM((nb, N_BLOCKS_MAX * N_HEADS, MOBA_BLOCK), BF16),
        ],
        compiler_params=pltpu.CompilerParams(
            dimension_semantics=("arbitrary",), vmem_limit_bytes=MOBA_VMEM_LIMIT),
        name="moba",
    )(qkv, qkv, qkv, slopes)


FFN_CHUNKS = ((0, 1024), (1024, 1024), (2048, 768))


POST_STREAMS = 2


def _post_rows(rows, x_ref, ya_ref, yb_ref, gate_ref, wua_ref, wub_ref, wo_ref, gffn_ref,
               wfi_ref, wfo_ref, gfin_ref, o_ref):
    y_a = _dot(ya_ref[rows, :], wua_ref[...])
    y_b = _dot(yb_ref[rows, :], wub_ref[...])
    yield
    ga = _sigmoid(gate_ref[rows, 0:D_MODEL].astype(F32))
    gb = _sigmoid(gate_ref[rows, D_MODEL:2 * D_MODEL].astype(F32))
    mixed = (ga * y_a + gb * y_b).astype(BF16)
    x1 = x_ref[rows, :] + _dot(mixed, wo_ref[...])
    yield
    h = _rms(x1, gffn_ref[...]).astype(BF16)
    acc = x1
    for off, n in FFN_CHUNKS:
        gg = _dot(h, wfi_ref[:, off:off + n])
        uu = _dot(h, wfi_ref[:, FFN_HIDDEN + off:FFN_HIDDEN + off + n])
        yield
        act = (gg * _sigmoid(gg) * uu).astype(BF16)
        acc = acc + _dot(act, wfo_ref[off:off + n, :])
        yield
    o_ref[rows, :] = _rms(acc, gfin_ref[...])


def _post_kernel(*refs):
    tm = refs[0].shape[0]
    part = tm // POST_STREAMS
    live = [_post_rows(slice(k * part, (k + 1) * part), *refs) for k in range(POST_STREAMS)]
    done = object()
    while live:
        live = [gen for gen in live if next(gen, done) is not done]


def _post(x2d, ya, yb, gates, wua, wub, wo, gffn, wfi, wfo, gfin, tm):
    t = x2d.shape[0]
    const = lambda i: (0, 0)
    tile = lambda n: pl.BlockSpec((tm, n), lambda i: (i, 0))
    weight = lambda a: pl.BlockSpec(a.shape, const, pipeline_mode=pl.Buffered(1))
    return pl.pallas_call(
        _post_kernel,
        grid=(t // tm,),
        in_specs=[tile(D_MODEL), tile(WIDTH), tile(WIDTH), tile(GATES),
                  weight(wua), weight(wub), weight(wo), weight(gffn),
                  weight(wfi), weight(wfo), weight(gfin)],
        out_specs=tile(D_MODEL),
        out_shape=jax.ShapeDtypeStruct((t, D_MODEL), F32),
        compiler_params=pltpu.CompilerParams(
            dimension_semantics=("arbitrary",), vmem_limit_bytes=VMEM_LIMIT),
        name="post",
    )(x2d, ya, yb, gates, wua, wub, wo, gffn, wfi, wfo, gfin)


def _layer(x, norm_mix_g, w_in, mu_shift, w0, w_decay_up, a0, w_iclr_up, w_gate_up,
           k_k, k_a, r_k, ln_x_w, ln_x_b, w_up_a, w_up_b, w_o, norm_ffn_g,
           w_ffn_in, w_ffn_out, out_g):
    b, s, _ = x.shape
    t = b * s
    x2d = x.reshape(t, D_MODEL)

    pad = SHIFT_PAD - SHIFT_WIDTH
    w_sp = jnp.pad(w_in[:, :SHIFT_WIDTH].astype(BF16), ((0, 0), (0, pad)))
    w_qkv = w_in[:, SHIFT_WIDTH:SHIFT_WIDTH + QKV_B].astype(BF16)
    w_gate = w_in[:, SHIFT_WIDTH + QKV_B:].astype(BF16)
    mu = jnp.pad(mu_shift, (0, pad)).reshape(1, SHIFT_PAD)
    wda = jnp.zeros((LANES, 2 * WIDTH), F32)
    wda = wda.at[:DECAY_LORA, :WIDTH].set(w_decay_up).at[DECAY_LORA:, WIDTH:].set(w_iclr_up)
    wg = jnp.pad(w_gate_up, ((0, 2 * LANES - GATE_LORA), (0, 0)))
    row = lambda a: a.reshape(1, -1)

    tt = TIME_TILE
    hid = jnp.arange(SEG_LANES) // HEAD_DIM
    seg = (hid[:, None] == hid[None, :]).astype(BF16)
    ti = jnp.arange(tt)
    tri = ((ti[:, None] >= ti[None, :]) & (ti[:, None] // CHUNK == ti[None, :] // CHUNK)).astype(BF16)

    sp, qkv, gates = _proj(x2d, row(norm_mix_g), w_sp, w_qkv, w_gate, mu, tm=TOKEN_TILE,
                            seq=s)
    ya = _rwkv(sp.reshape(b, s, SHIFT_PAD), row(w0), wda.astype(BF16), row(a0),
               wg.astype(BF16), row(k_k), row(k_a), row(r_k), row(ln_x_w), row(ln_x_b),
               seg, tri, tt)
    slopes = jnp.exp2(-8.0 * jnp.arange(1, N_HEADS + 1, dtype=F32) / N_HEADS)
    slopes = jnp.broadcast_to(slopes[:, None], (N_HEADS, MOBA_BLOCK))
    yb = _moba(qkv.reshape(b, s, QKV_B), slopes)
    out = _post(x2d, ya.reshape(t, WIDTH), yb.reshape(t, WIDTH), gates,
                w_up_a.astype(BF16), w_up_b.astype(BF16), w_o.astype(BF16), row(norm_ffn_g),
                w_ffn_in.astype(BF16), w_ffn_out.astype(BF16), row(out_g), tm=TOKEN_TILE)
    return out.reshape(b, s, D_MODEL)


def kernel(x, norm_mix_g, w_in, mu_shift, w0, w_decay_up, a0, w_iclr_up, w_gate_up, k_k, k_a, r_k, ln_x_w, ln_x_b, w_up_a, w_up_b, w_o, norm_ffn_g, w_ffn_in, w_ffn_out, norm_final_g):
    depth = w_in.shape[0]
    assert depth == 1, "the fused post kernel applies the final norm after the only layer"
    return _layer(x, norm_mix_g[0], w_in[0], mu_shift[0], w0[0], w_decay_up[0], a0[0],
                  w_iclr_up[0], w_gate_up[0], k_k[0], k_a[0], r_k[0].reshape(-1), ln_x_w[0],
                  ln_x_b[0], w_up_a[0], w_up_b[0], w_o[0], norm_ffn_g[0], w_ffn_in[0],
                  w_ffn_out[0], norm_final_g)
```

```python
import functools

import jax
import jax.numpy as jnp
from jax import lax
from jax.experimental import pallas as pl
from jax.experimental.pallas import tpu as pltpu

F32 = jnp.float32
BF16 = jnp.bfloat16

D_MODEL = 1024
HEAD_DIM = 64
N_HEADS = 8
WIDTH = N_HEADS * HEAD_DIM
N_PAIRS = N_HEADS // 2
LANES = 128
DECAY_LORA = 64
ICLR_LORA = 64
GATE_LORA = 160
GN_EPS = 64e-5
EXP_NEG_HALF = 0.6065306597126334
RMS_EPS = 1e-6
MOBA_BLOCK = 256
MOBA_TOP_K = 3
FFN_HIDDEN = 2816
SHIFT_WIDTH = 3 * WIDTH + DECAY_LORA + ICLR_LORA + GATE_LORA
SHIFT_PAD = 1920
QKV_B = 3 * WIDTH
GATES = 2 * D_MODEL
PROJ_PAD = SHIFT_PAD + QKV_B + GATES

CHUNK = 64
PROJ_TILE = 1024
TOKEN_TILE = 512
TIME_TILE = 256
VMEM_LIMIT = 56 * 1024 * 1024
MOBA_VMEM_LIMIT = 60 * 1024 * 1024


def _dot(a, b):
    return jnp.dot(a, b, preferred_element_type=F32)


def _dot_nt(a, b):
    return lax.dot_general(a, b, (((1,), (1,)), ((), ())), preferred_element_type=F32)


def _sigmoid(x):
    return 0.5 * jnp.tanh(0.5 * x) + 0.5


def _rms(x, g):
    return x * lax.rsqrt(jnp.mean(x * x, axis=-1, keepdims=True) + RMS_EPS) * g


def _proj_kernel(x_ref, g_ref, wsp_ref, wqkv_ref, wgate_ref, mu_ref, sp_ref, qkv_ref, gate_ref,
                 carry_ref, *, tiles_per_seq):
    tm = x_ref.shape[0]
    step = pl.program_id(0)

    @pl.when(step == 0)
    def _():
        carry_ref[...] = jnp.zeros_like(carry_ref)

    h = _rms(x_ref[...], g_ref[...]).astype(BF16)
    qkv_ref[...] = _dot(h, wqkv_ref[...]).astype(BF16)
    gate_ref[...] = _dot(h, wgate_ref[...]).astype(BF16)

    p = _dot(h, wsp_ref[...])
    seq_start = lax.rem(step, tiles_per_seq) == 0
    carry = jnp.where(seq_start, 0.0, carry_ref[0:1, :])
    row = lax.broadcasted_iota(jnp.int32, (tm, 1), 0)
    prev = jnp.where(row == 0, carry, pltpu.roll(p, 1, 0))
    carry_ref[0:1, :] = p[tm - 1:tm, :]
    sp_ref[...] = p + mu_ref[...] * (prev - p)


def _proj(x2d, g, w_sp, w_qkv, w_gate, mu, tm, seq):
    t = x2d.shape[0]
    const = lambda i: (0, 0)
    weight = lambda n: pl.BlockSpec((D_MODEL, n), const, pipeline_mode=pl.Buffered(1))
    return pl.pallas_call(
        functools.partial(_proj_kernel, tiles_per_seq=seq // tm),
        grid=(t // tm,),
        in_specs=[
            pl.BlockSpec((tm, D_MODEL), lambda i: (i, 0)),
            pl.BlockSpec((1, D_MODEL), const),
            weight(SHIFT_PAD), weight(QKV_B), weight(GATES),
            pl.BlockSpec((1, SHIFT_PAD), const),
        ],
        scratch_shapes=[pltpu.VMEM((8, SHIFT_PAD), F32)],
        out_specs=[
            pl.BlockSpec((tm, SHIFT_PAD), lambda i: (i, 0)),
            pl.BlockSpec((tm, QKV_B), lambda i: (i, 0)),
            pl.BlockSpec((tm, GATES), lambda i: (i, 0)),
        ],
        out_shape=[
            jax.ShapeDtypeStruct((t, SHIFT_PAD), F32),
            jax.ShapeDtypeStruct((t, QKV_B), BF16),
            jax.ShapeDtypeStruct((t, GATES), BF16),
        ],
        compiler_params=pltpu.CompilerParams(
            dimension_semantics=("arbitrary",), vmem_limit_bytes=VMEM_LIMIT),
        name="proj",
    )(x2d, g, w_sp, w_qkv, w_gate, mu)


def _split2(x):
    hi = x.astype(BF16)
    lo = (x - hi.astype(F32)).astype(BF16)
    return hi, lo


SEG_LANES = 2 * LANES


def _segsum(x, seg):
    xb = x.astype(BF16)
    return jnp.concatenate(
        [_dot(xb[:, o:o + SEG_LANES], seg) for o in range(0, x.shape[1], SEG_LANES)], axis=1)


N_STREAMS = 4


def _rwkv_kernel(sp_ref, w0_ref, wda_ref, a0_ref, wg_ref, kk_ref, ka_ref, rk_ref,
                 lnw_ref, lnb_ref, seg_ref, tri_ref, o_ref, h_ref, *, tt):
    @pl.when(pl.program_id(1) == 0)
    def _():
        h_ref[...] = jnp.zeros_like(h_ref)

    params = (w0_ref, wda_ref, a0_ref, wg_ref, kk_ref, ka_ref, rk_ref, lnw_ref, lnb_ref,
              seg_ref, tri_ref)
    live = [_rwkv_stream(k, sp_ref, o_ref, h_ref, *params, tt=tt) for k in range(N_STREAMS)]
    done = object()
    while live:
        live = [gen for gen in live if next(gen, done) is not done]


def _rwkv_stream(k_seq, sp_ref, o_ref, h_ref, w0_ref, wda_ref, a0_ref, wg_ref, kk_ref, ka_ref,
                 rk_ref, lnw_ref, lnb_ref, seg_ref, tri_ref, *, tt):
    n_chunks = tt // CHUNK
    xs = sp_ref[k_seq, 0]
    r = xs[:, 0:WIDTH]
    k = xs[:, WIDTH:2 * WIDTH]
    v = xs[:, 2 * WIDTH:3 * WIDTH]
    da = xs[:, 3 * WIDTH:3 * WIDTH + LANES]
    gd = xs[:, 3 * WIDTH + LANES:SHIFT_PAD]
    lane = lax.broadcasted_iota(jnp.int32, (1, LANES), 1)
    lo_half = lane < HEAD_DIM
    da_act = jnp.where(lo_half, jnp.tanh(da), da).astype(BF16)
    lora = _dot(da_act, wda_ref[...])
    g = _dot(_sigmoid(gd).astype(BF16), wg_ref[...])

    logdec = -EXP_NEG_HALF * _sigmoid(w0_ref[...] + lora[:, 0:WIDTH])
    a = _sigmoid(a0_ref[...] + lora[:, WIDTH:2 * WIDTH])

    seg = seg_ref[...]
    kkf = k * kk_ref[...]
    kk_sq = _segsum(kkf * kkf, seg)
    yield
    kk = kkf * lax.rsqrt(jnp.maximum(kk_sq, 1e-24))
    k2 = k * (1.0 + (a - 1.0) * ka_ref[...])
    bonus = _segsum(r * k2 * rk_ref[...], seg) * v

    l1, l2 = _split2(logdec)
    tri = tri_ref[...]
    cum = _dot(tri, l1) + _dot(tri, l2)
    yield

    kka = kk * a
    p_inv = jnp.exp(-cum)
    a_t = -kk * jnp.exp(cum - logdec)
    r_t = r * jnp.exp(cum)
    b_t = kka * p_inv
    k_t = k2 * p_inv

    r64 = lax.broadcasted_iota(jnp.int32, (CHUNK, LANES), 0)
    c64 = lax.broadcasted_iota(jnp.int32, (CHUNK, LANES), 1) & (CHUNK - 1)
    strict = c64 < r64
    incl = c64 <= r64
    eye_side = (c64 == r64).astype(F32)
    r128 = lax.broadcasted_iota(jnp.int32, (LANES, LANES), 0)
    c128 = lax.broadcasted_iota(jnp.int32, (LANES, LANES), 1)
    eye128 = (r128 == c128).astype(F32)
    same_head = (r128 >> 6) == (c128 >> 6)
    hi_half = jnp.logical_not(lo_half)
    zeros_c = jnp.zeros((CHUNK, LANES), F32)

    def _stack2(zz):
        zb = zz.astype(BF16)
        return jnp.concatenate([jnp.where(lo_half, zb, 0), jnp.where(hi_half, zb, 0)], axis=0)

    chains = [(c, pr) for c in range(n_chunks) for pr in range(N_PAIRS)]

    def _piece(arr, c, pr):
        return arr[c * CHUNK:(c + 1) * CHUNK, pr * LANES:(pr + 1) * LANES]

    x_ab, x_ak, x_rb, x_rk = {}, {}, {}, {}
    for ch in chains:
        lq = jnp.concatenate([_piece(a_t, *ch), _piece(r_t, *ch)], axis=0).astype(BF16)
        rhs = jnp.concatenate([_stack2(_piece(b_t, *ch)), _stack2(_piece(k_t, *ch))], axis=0)
        sc = _dot_nt(lq, rhs)
        x_ab[ch] = jnp.where(strict, sc[0:CHUNK, 0:LANES], 0.0)
        x_ak[ch] = jnp.where(strict, sc[0:CHUNK, LANES:2 * LANES], 0.0)
        x_rb[ch] = jnp.where(incl, sc[CHUNK:2 * CHUNK, 0:LANES], 0.0)
        x_rk[ch] = jnp.where(incl, sc[CHUNK:2 * CHUNK, LANES:2 * LANES], 0.0)
    yield

    x_t = {ch: eye_side + x_ab[ch] for ch in chains}
    x_pw = {ch: _dot(x_ab[ch].astype(BF16), _stack2(x_ab[ch])) for ch in chains}
    av = {ch: _dot(x_ak[ch].astype(BF16), _stack2(_piece(v, *ch))) for ch in chains}
    yield
    for _ in range(4):
        for ch in chains:
            both = jnp.concatenate([x_t[ch], x_pw[ch]], axis=0).astype(BF16)
            out = _dot(both, _stack2(x_pw[ch]))
            x_t[ch] = x_t[ch] + out[0:CHUNK]
            x_pw[ch] = out[CHUNK:2 * CHUNK]
        yield
    for ch in chains:
        x_t[ch] = x_t[ch] + _dot(x_t[ch].astype(BF16), _stack2(x_pw[ch]))
    yield

    tw = {}
    for ch in chains:
        rhs = jnp.concatenate([_stack2(_piece(a_t, *ch)), _stack2(av[ch])], axis=1)
        tw[ch] = _dot(x_t[ch].astype(BF16), rhs)
    yield
    qeff, yloc, phi, psi = {}, {}, {}, {}
    for ch in chains:
        ta, w_loc, v_p = tw[ch][:, 0:LANES], tw[ch][:, LANES:2 * LANES], _piece(v, *ch)
        rhs = jnp.concatenate(
            [jnp.concatenate([_stack2(ta), _stack2(w_loc)], axis=1),
             jnp.concatenate([jnp.zeros((LANES, LANES), BF16), _stack2(v_p)], axis=1)], axis=0)
        ag = _dot(jnp.concatenate([x_rb[ch], x_rk[ch]], axis=1).astype(BF16), rhs)
        qeff[ch] = _piece(r_t, *ch) + ag[:, 0:LANES]
        yloc[ch] = ag[:, LANES:2 * LANES]
        cum_last = _piece(cum, *ch)[CHUNK - 1:CHUNK, :]
        p_end = jnp.exp(cum_last)
        kk_t = jnp.concatenate([_piece(b_t, *ch) * p_end, _piece(k_t, *ch) * p_end],
                               axis=0).astype(BF16).T
        rhs = jnp.concatenate(
            [tw[ch], jnp.concatenate([zeros_c, v_p], axis=1)], axis=0).astype(BF16)
        pp = _dot(kk_t, rhs)
        phi[ch] = eye128 * p_end + jnp.where(same_head, pp[:, 0:LANES], 0.0)
        psi[ch] = jnp.where(same_head, pp[:, LANES:2 * LANES], 0.0)
    yield

    y_rows = []
    states = [h_ref[k_seq, pr] for pr in range(N_PAIRS)]
    for c in range(n_chunks):
        y_c = []
        for pr in range(N_PAIRS):
            ch = (c, pr)
            lhs = jnp.concatenate([qeff[ch], phi[ch]], axis=0).astype(BF16)
            out = _dot(lhs, states[pr].astype(BF16))
            y_c.append(out[0:CHUNK] + yloc[ch])
            states[pr] = out[CHUNK:CHUNK + LANES] + psi[ch]
        y_rows.append(jnp.concatenate(y_c, axis=1))
        yield
    for pr in range(N_PAIRS):
        h_ref[k_seq, pr] = states[pr]
    y = jnp.concatenate(y_rows, axis=0)

    mean = _segsum(y, seg) * (1.0 / HEAD_DIM)
    yield
    d = y - mean
    var = _segsum(d * d, seg) * (1.0 / HEAD_DIM)
    yield
    yn = d * lax.rsqrt(var + GN_EPS) * lnw_ref[...] + lnb_ref[...]
    o_ref[k_seq, 0] = ((yn + bonus) * g).astype(BF16)


def _rwkv(sp, w0, wda, a0, wg, k_k, k_a, r_k, ln_w, ln_b, seg, tri, tt):
    b, s, _ = sp.shape
    assert b % N_STREAMS == 0
    rows = b // N_STREAMS
    const2 = lambda i, j: (0, 0)
    row = lambda n: pl.BlockSpec((1, n), const2)
    out = pl.pallas_call(
        functools.partial(_rwkv_kernel, tt=tt),
        grid=(rows, s // tt),
        in_specs=[
            pl.BlockSpec((N_STREAMS, 1, tt, SHIFT_PAD), lambda i, j: (0, i, j, 0)),
            row(WIDTH),
            pl.BlockSpec((LANES, 2 * WIDTH), const2),
            row(WIDTH),
            pl.BlockSpec((2 * LANES, WIDTH), const2),
            row(WIDTH), row(WIDTH), row(WIDTH), row(WIDTH), row(WIDTH),
            pl.BlockSpec((SEG_LANES, SEG_LANES), const2),
            pl.BlockSpec((tt, tt), const2),
        ],
        out_specs=pl.BlockSpec((N_STREAMS, 1, tt, WIDTH), lambda i, j: (0, i, j, 0)),
        out_shape=jax.ShapeDtypeStruct((N_STREAMS, rows, s, WIDTH), BF16),
        scratch_shapes=[
            pltpu.VMEM((N_STREAMS, N_PAIRS, LANES, LANES), F32),
        ],
        compiler_params=pltpu.CompilerParams(
            dimension_semantics=("arbitrary", "arbitrary"), vmem_limit_bytes=VMEM_LIMIT),
        name="rwkv",
    )(sp.reshape(N_STREAMS, rows, s, SHIFT_PAD), w0, wda, a0, wg, k_k, k_a, r_k, ln_w, ln_b,
      seg, tri)
    return out.reshape(b, s, WIDTH)


NEG_BIG = -(2.0 ** 30)
N_BLOCKS_MAX = 16
V_ROWS = HEAD_DIM + 16
LOG2E = 1.4426950408889634
Q_GROUP = 2


def _aligned(i, m):
    return i * m if isinstance(i, int) else pl.multiple_of(i * m, m)


def _moba_kernel(q_ref, k_ref, v_ref, slope_ref, o_ref, kx_ref, vt_ref, qt_ref, bias_ref, *, seq):
    nb = seq // MOBA_BLOCK
    bs = MOBA_BLOCK
    scale = HEAD_DIM ** -0.5
    heads = range(N_HEADS)

    rr = lax.broadcasted_iota(jnp.int32, (bs, LANES), 0)
    ll = lax.broadcasted_iota(jnp.int32, (bs, LANES), 1)
    ones_rows = jnp.ones((V_ROWS - HEAD_DIM, bs), BF16)
    for n in range(nb):
        aug = jnp.where(ll < 4, 1.0,
              jnp.where(ll < 6, rr.astype(F32),
              jnp.where(ll < 8, float(n * bs),
              jnp.where(ll == 8 + n, 1.0, 0.0)))).astype(BF16)
        rows = slice(n * bs, (n + 1) * bs)
        for pr in range(N_PAIRS):
            ln = slice(pr * LANES, (pr + 1) * LANES)
            kx_ref[pr, n, :, 0:LANES] = k_ref[0, rows, ln]
            kx_ref[pr, n, :, LANES:2 * LANES] = aug
            v_t = v_ref[0, rows, ln].T
            for half in range(2):
                vt_ref[2 * pr + half, n] = jnp.concatenate(
                    [v_t[half * HEAD_DIM:(half + 1) * HEAD_DIM], ones_rows], axis=0)

    bi = lax.broadcasted_iota(jnp.int32, (N_BLOCKS_MAX, seq), 0)
    si = lax.broadcasted_iota(jnp.int32, (N_BLOCKS_MAX, seq), 1)
    avg = jnp.where((si >= bi * bs) & (si < (bi + 1) * bs), 1.0 / bs, 0.0).astype(BF16)
    kmean = _dot(avg, k_ref[0])
    km = jnp.concatenate([jnp.broadcast_to(kmean[n:n + 1, :], (N_HEADS, WIDTH))
                          for n in range(N_BLOCKS_MAX)], axis=0)
    row_head = lax.broadcasted_iota(jnp.int32, (LANES, WIDTH), 0) & (N_HEADS - 1)
    lane_head = lax.broadcasted_iota(jnp.int32, (LANES, WIDTH), 1) >> 6
    km_hi, km_lo = _split2(jnp.where(row_head == lane_head, km, 0.0))

    pr_i = lax.broadcasted_iota(jnp.int32, (LANES, LANES), 0)
    pc_i = lax.broadcasted_iota(jnp.int32, (LANES, LANES), 1)
    perm = (pc_i == ((pr_i & (N_BLOCKS_MAX - 1)) << 3) + (pr_i >> 4)).astype(BF16)
    zeros_cnt = jnp.zeros((N_HEADS, bs), F32)

    gates = {}
    for i in range(nb):
        q_blk = q_ref[0, i * bs:(i + 1) * bs, :]
        q_t = [q_blk[:, pr * LANES:(pr + 1) * LANES].T for pr in range(N_PAIRS)]
        for pr in range(N_PAIRS):
            qt_ref[pr, i] = q_t[pr]
        if i <= MOBA_TOP_K:
            bias_ref[i] = jnp.zeros((N_BLOCKS_MAX * N_HEADS, bs), BF16)
            continue
        q_tb = jnp.concatenate(q_t, axis=0)
        gates[i] = _dot(km_hi, q_tb) + _dot(km_lo, q_tb)
    biases = {}
    for i, gate in gates.items():
        slabs = [gate[n * N_HEADS:(n + 1) * N_HEADS] for n in range(i)]
        cnt = [zeros_cnt] * i
        for n in range(i):
            for m in range(n):
                m_wins = slabs[m] >= slabs[n]
                cnt[n] = cnt[n] + jnp.where(m_wins, 1.0, 0.0)
                cnt[m] = cnt[m] + jnp.where(m_wins, 0.0, 1.0)
        bias = [jnp.where(cnt[n] < MOBA_TOP_K, 0.0, NEG_BIG) for n in range(i)]
        bias += [zeros_cnt] * (N_BLOCKS_MAX - i)
        biases[i] = jnp.concatenate(bias, axis=0).astype(BF16)
    for i, bias in biases.items():
        bias_ref[i] = _dot(perm, bias).astype(BF16)

    r128 = lax.broadcasted_iota(jnp.int32, (LANES, 1), 0)
    row_masks = (r128 < HEAD_DIM, r128 >= HEAD_DIM)
    r8 = lax.broadcasted_iota(jnp.int32, (8, bs), 0)
    tl = lax.broadcasted_iota(jnp.int32, (8, bs), 1).astype(F32)
    key_row = lax.broadcasted_iota(jnp.int32, (bs, bs), 0)
    qry_col = lax.broadcasted_iota(jnp.int32, (bs, bs), 1)
    causal = key_row <= qry_col
    zeros_pad = jnp.zeros((LANES - 8 - N_BLOCKS_MAX, bs), F32)

    def q_block(i, odd):
        start = _aligned(i, bs)
        q_t = [qt_ref[pr, i].astype(F32) for pr in range(N_PAIRS)]
        bias = bias_ref[i].astype(F32)
        t0 = jnp.asarray(i * bs, F32)

        q_aug = []
        for h in heads:
            c = slope_ref[h:h + 1, :] * LOG2E
            c_hi = c.astype(BF16).astype(F32)
            ctl = c * tl
            ctl_hi = ctl.astype(BF16).astype(F32)
            ct0 = c * t0
            ct0_hi = ct0.astype(BF16).astype(F32)
            aug8 = jnp.where(r8 == 0, -ctl_hi,
                   jnp.where(r8 == 1, ctl_hi - ctl,
                   jnp.where(r8 == 2, -ct0_hi,
                   jnp.where(r8 == 3, ct0_hi - ct0,
                   jnp.where((r8 == 4) | (r8 == 6), c_hi, c - c_hi)))))
            top = jnp.where(row_masks[h % 2], q_t[h // 2], 0.0) * (scale * LOG2E)
            sel = bias[h * N_BLOCKS_MAX:(h + 1) * N_BLOCKS_MAX]
            q_aug.append(jnp.concatenate([top, aug8, sel, zeros_pad], axis=0).astype(BF16))

        def scores(j):
            kx = [kx_ref[pr, j] for pr in range(N_PAIRS)]
            return [_dot(kx[h // 2], q_aug[h]) for h in heads]

        def weighted_values(j, ps):
            return [_dot(vt_ref[h, j], ps[h]) for h in heads]

        s_own = scores(i)
        s_left = scores(i - 1) if odd else None
        yield
        m0, p0 = [], []
        for h in heads:
            sh = jnp.where(causal, s_own[h], NEG_BIG)
            m0.append(jnp.max(sh, axis=0, keepdims=True))
            p0.append(jnp.exp2(sh - m0[h]).astype(BF16))
        pv = weighted_values(i, p0)
        if odd:
            alpha1, p1 = [], []
            for h in heads:
                mh = jnp.maximum(m0[h], jnp.max(s_left[h], axis=0, keepdims=True))
                alpha1.append(jnp.exp2(m0[h] - mh))
                p1.append(jnp.exp2(s_left[h] - mh).astype(BF16))
                m0[h] = mh
            pv1 = weighted_values(i - 1, p1)
        yield
        acc0 = [pv[h][0:HEAD_DIM] for h in heads]
        l0 = [pv[h][HEAD_DIM:HEAD_DIM + 1] for h in heads]
        if odd:
            acc0 = [alpha1[h] * acc0[h] + pv1[h][0:HEAD_DIM] for h in heads]
            l0 = [alpha1[h] * l0[h] + pv1[h][HEAD_DIM:HEAD_DIM + 1] for h in heads]

        def step(js, st):
            m, l, acc = (list(x) for x in st)
            s_all = [scores(j) for j in js]
            ps, alphas = [], []
            for s in s_all:
                p_blk, a_blk = [], []
                for h in heads:
                    mh = jnp.maximum(m[h], jnp.max(s[h], axis=0, keepdims=True))
                    a_blk.append(jnp.exp2(m[h] - mh))
                    p_blk.append(jnp.exp2(s[h] - mh).astype(BF16))
                    m[h] = mh
                ps.append(p_blk)
                alphas.append(a_blk)
            pvs = [weighted_values(j, p_blk) for j, p_blk in zip(js, ps)]
            for a_blk, pv in zip(alphas, pvs):
                for h in heads:
                    acc[h] = a_blk[h] * acc[h] + pv[h][0:HEAD_DIM]
                    l[h] = a_blk[h] * l[h] + pv[h][HEAD_DIM:HEAD_DIM + 1]
            return tuple(tuple(x) for x in (m, l, acc))

        st = (tuple(m0), tuple(l0), tuple(acc0))
        _, l, acc = lax.fori_loop(0, i >> 1, lambda t, st: step([2 * t, 2 * t + 1], st), st)
        pairs = []
        for pr in range(N_PAIRS):
            a, b = 2 * pr, 2 * pr + 1
            out_t = jnp.concatenate([acc[a] / l[a], acc[b] / l[b]], axis=0)
            pairs.append(out_t.astype(BF16).T)
        o_ref[0, pl.ds(start, bs), :] = jnp.concatenate(pairs, axis=1)

    def q_block_group(u, carry):
        live = [q_block(Q_GROUP * u + g, g % 2 == 1) for g in range(Q_GROUP)]
        done = object()
        while live:
            live = [gen for gen in live if next(gen, done) is not done]
        return carry

    assert nb % Q_GROUP == 0 and Q_GROUP % 2 == 0
    lax.fori_loop(0, nb // Q_GROUP, q_block_group, 0)


def _moba(qkv, slopes):
    b, s, _ = qkv.shape
    assert s % MOBA_BLOCK == 0 and s // MOBA_BLOCK <= N_BLOCKS_MAX
    nb = s // MOBA_BLOCK
    col = lambda c: pl.BlockSpec((1, s, WIDTH), lambda i: (i, 0, c))
    return pl.pallas_call(
        functools.partial(_moba_kernel, seq=s),
        grid=(b,),
        in_specs=[col(0), col(1), col(2), pl.BlockSpec((N_HEADS, MOBA_BLOCK), lambda i: (0, 0))],
        out_specs=pl.BlockSpec((1, s, WIDTH), lambda i: (i, 0, 0)),
        out_shape=jax.ShapeDtypeStruct((b, s, WIDTH), BF16),
        scratch_shapes=[
            pltpu.VMEM((N_PAIRS, nb, MOBA_BLOCK, 2 * LANES), BF16),
            pltpu.VMEM((N_HEADS, nb, V_ROWS, MOBA_BLOCK), BF16),
            pltpu.VMEM((N_PAIRS, nb, LANES, MOBA_BLOCK), BF16),
            pltpu.VMEM((nb, N_BLOCKS_MAX * N_HEADS, MOBA_BLOCK), BF16),
        ],
        compiler_params=pltpu.CompilerParams(
            dimension_semantics=("arbitrary",), vmem_limit_bytes=MOBA_VMEM_LIMIT),
        name="moba",
    )(qkv, qkv, qkv, slopes)


FFN_CHUNKS = ((0, 1024), (1024, 1024), (2048, 768))


POST_STREAMS = 2


def _post_rows(rows, x_ref, ya_ref, yb_ref, gate_ref, wua_ref, wub_ref, wo_ref, gffn_ref,
               wfi_ref, wfo_ref, gfin_ref, o_ref):
    y_a = _dot(ya_ref[rows, :], wua_ref[...])
    y_b = _dot(yb_ref[rows, :], wub_ref[...])
    yield
    ga = _sigmoid(gate_ref[rows, 0:D_MODEL].astype(F32))
    gb = _sigmoid(gate_ref[rows, D_MODEL:2 * D_MODEL].astype(F32))
    mixed = (ga * y_a + gb * y_b).astype(BF16)
    x1 = x_ref[rows, :] + _dot(mixed, wo_ref[...])
    yield
    h = _rms(x1, gffn_ref[...]).astype(BF16)
    acc = x1
    for off, n in FFN_CHUNKS:
        gg = _dot(h, wfi_ref[:, off:off + n])
        uu = _dot(h, wfi_ref[:, FFN_HIDDEN + off:FFN_HIDDEN + off + n])
        yield
        act = (gg * _sigmoid(gg) * uu).astype(BF16)
        acc = acc + _dot(act, wfo_ref[off:off + n, :])
        yield
    o_ref[rows, :] = _rms(acc, gfin_ref[...])


def _post_kernel(*refs):
    tm = refs[0].shape[0]
    part = tm // POST_STREAMS
    live = [_post_rows(slice(k * part, (k + 1) * part), *refs) for k in range(POST_STREAMS)]
    done = object()
    while live:
        live = [gen for gen in live if next(gen, done) is not done]


def _post(x2d, ya, yb, gates, wua, wub, wo, gffn, wfi, wfo, gfin, tm):
    t = x2d.shape[0]
    const = lambda i: (0, 0)
    tile = lambda n: pl.BlockSpec((tm, n), lambda i: (i, 0))
    weight = lambda a: pl.BlockSpec(a.shape, const, pipeline_mode=pl.Buffered(1))
    return pl.pallas_call(
        _post_kernel,
        grid=(t // tm,),
        in_specs=[tile(D_MODEL), tile(WIDTH), tile(WIDTH), tile(GATES),
                  weight(wua), weight(wub), weight(wo), weight(gffn),
                  weight(wfi), weight(wfo), weight(gfin)],
        out_specs=tile(D_MODEL),
        out_shape=jax.ShapeDtypeStruct((t, D_MODEL), F32),
        compiler_params=pltpu.CompilerParams(
            dimension_semantics=("arbitrary",), vmem_limit_bytes=VMEM_LIMIT),
        name="post",
    )(x2d, ya, yb, gates, wua, wub, wo, gffn, wfi, wfo, gfin)


def _layer(x, norm_mix_g, w_in, mu_shift, w0, w_decay_up, a0, w_iclr_up, w_gate_up,
           k_k, k_a, r_k, ln_x_w, ln_x_b, w_up_a, w_up_b, w_o, norm_ffn_g,
           w_ffn_in, w_ffn_out, out_g):
    b, s, _ = x.shape
    t = b * s
    x2d = x.reshape(t, D_MODEL)

    pad = SHIFT_PAD - SHIFT_WIDTH
    w_sp = jnp.pad(w_in[:, :SHIFT_WIDTH].astype(BF16), ((0, 0), (0, pad)))
    w_qkv = w_in[:, SHIFT_WIDTH:SHIFT_WIDTH + QKV_B].astype(BF16)
    w_gate = w_in[:, SHIFT_WIDTH + QKV_B:].astype(BF16)
    mu = jnp.pad(mu_shift, (0, pad)).reshape(1, SHIFT_PAD)
    wda = jnp.zeros((LANES, 2 * WIDTH), F32)
    wda = wda.at[:DECAY_LORA, :WIDTH].set(w_decay_up).at[DECAY_LORA:, WIDTH:].set(w_iclr_up)
    wg = jnp.pad(w_gate_up, ((0, 2 * LANES - GATE_LORA), (0, 0)))
    row = lambda a: a.reshape(1, -1)

    tt = TIME_TILE
    hid = jnp.arange(SEG_LANES) // HEAD_DIM
    seg = (hid[:, None] == hid[None, :]).astype(BF16)
    ti = jnp.arange(tt)
    tri = ((ti[:, None] >= ti[None, :]) & (ti[:, None] // CHUNK == ti[None, :] // CHUNK)).astype(BF16)

    sp, qkv, gates = _proj(x2d, row(norm_mix_g), w_sp, w_qkv, w_gate, mu, tm=PROJ_TILE,
                            seq=s)
    ya = _rwkv(sp.reshape(b, s, SHIFT_PAD), row(w0), wda.astype(BF16), row(a0),
               wg.astype(BF16), row(k_k), row(k_a), row(r_k), row(ln_x_w), row(ln_x_b),
               seg, tri, tt)
    slopes = jnp.exp2(-8.0 * jnp.arange(1, N_HEADS + 1, dtype=F32) / N_HEADS)
    slopes = jnp.broadcast_to(slopes[:, None], (N_HEADS, MOBA_BLOCK))
    yb = _moba(qkv.reshape(b, s, QKV_B), slopes)
    out = _post(x2d, ya.reshape(t, WIDTH), yb.reshape(t, WIDTH), gates,
                w_up_a.astype(BF16), w_up_b.astype(BF16), w_o.astype(BF16), row(norm_ffn_g),
                w_ffn_in.astype(BF16), w_ffn_out.astype(BF16), row(out_g), tm=TOKEN_TILE)
    return out.reshape(b, s, D_MODEL)


def kernel(x, norm_mix_g, w_in, mu_shift, w0, w_decay_up, a0, w_iclr_up, w_gate_up, k_k, k_a, r_k, ln_x_w, ln_x_b, w_up_a, w_up_b, w_o, norm_ffn_g, w_ffn_in, w_ffn_out, norm_final_g):
    depth = w_in.shape[0]
    assert depth == 1, "the fused post kernel applies the final norm after the only layer"
    return _layer(x, norm_mix_g[0], w_in[0], mu_shift[0], w0[0], w_decay_up[0], a0[0],
                  w_iclr_up[0], w_gate_up[0], k_k[0], k_a[0], r_k[0].reshape(-1), ln_x_w[0],
                  ln_x_b[0], w_up_a[0], w_up_b[0], w_o[0], norm_ffn_g[0], w_ffn_in[0],
                  w_ffn_out[0], norm_final_g)
```

```python
import functools

import jax
import jax.numpy as jnp
from jax import lax
from jax.experimental import pallas as pl
from jax.experimental.pallas import tpu as pltpu

F32 = jnp.float32
BF16 = jnp.bfloat16

D_MODEL = 1024
HEAD_DIM = 64
N_HEADS = 8
WIDTH = N_HEADS * HEAD_DIM
N_PAIRS = N_HEADS // 2
LANES = 128
DECAY_LORA = 64
ICLR_LORA = 64
GATE_LORA = 160
GN_EPS = 64e-5
EXP_NEG_HALF = 0.6065306597126334
RMS_EPS = 1e-6
MOBA_BLOCK = 256
MOBA_TOP_K = 3
FFN_HIDDEN = 2816
SHIFT_WIDTH = 3 * WIDTH + DECAY_LORA + ICLR_LORA + GATE_LORA
SHIFT_PAD = 1920
QKV_B = 3 * WIDTH
GATES = 2 * D_MODEL
PROJ_PAD = SHIFT_PAD + QKV_B + GATES

CHUNK = 64
PROJ_TILE = 1024
TOKEN_TILE = 512
TIME_TILE = 256
VMEM_LIMIT = 56 * 1024 * 1024
MOBA_VMEM_LIMIT = 60 * 1024 * 1024


def _dot(a, b):
    return jnp.dot(a, b, preferred_element_type=F32)


def _dot_nt(a, b):
    return lax.dot_general(a, b, (((1,), (1,)), ((), ())), preferred_element_type=F32)


def _sigmoid(x):
    return 0.5 * jnp.tanh(0.5 * x) + 0.5


def _rms(x, g):
    return x * lax.rsqrt(jnp.mean(x * x, axis=-1, keepdims=True) + RMS_EPS) * g


def _proj_kernel(x_ref, g_ref, wsp_ref, wqkv_ref, wgate_ref, mu_ref, sp_ref, qkv_ref, gate_ref,
                 carry_ref, *, tiles_per_seq):
    tm = x_ref.shape[0]
    step = pl.program_id(0)

    @pl.when(step == 0)
    def _():
        carry_ref[...] = jnp.zeros_like(carry_ref)

    h = _rms(x_ref[...], g_ref[...]).astype(BF16)
    qkv_ref[...] = _dot(h, wqkv_ref[...]).astype(BF16)
    gate_ref[...] = _dot(h, wgate_ref[...]).astype(BF16)

    p = _dot(h, wsp_ref[...])
    seq_start = lax.rem(step, tiles_per_seq) == 0
    carry = jnp.where(seq_start, 0.0, carry_ref[0:1, :])
    row = lax.broadcasted_iota(jnp.int32, (tm, 1), 0)
    prev = jnp.where(row == 0, carry, pltpu.roll(p, 1, 0))
    carry_ref[0:1, :] = p[tm - 1:tm, :]
    sp_ref[...] = (p + mu_ref[...] * (prev - p)).astype(BF16)


def _proj(x2d, g, w_sp, w_qkv, w_gate, mu, tm, seq):
    t = x2d.shape[0]
    const = lambda i: (0, 0)
    weight = lambda n: pl.BlockSpec((D_MODEL, n), const, pipeline_mode=pl.Buffered(1))
    return pl.pallas_call(
        functools.partial(_proj_kernel, tiles_per_seq=seq // tm),
        grid=(t // tm,),
        in_specs=[
            pl.BlockSpec((tm, D_MODEL), lambda i: (i, 0)),
            pl.BlockSpec((1, D_MODEL), const),
            weight(SHIFT_PAD), weight(QKV_B), weight(GATES),
            pl.BlockSpec((1, SHIFT_PAD), const),
        ],
        scratch_shapes=[pltpu.VMEM((8, SHIFT_PAD), F32)],
        out_specs=[
            pl.BlockSpec((tm, SHIFT_PAD), lambda i: (i, 0)),
            pl.BlockSpec((tm, QKV_B), lambda i: (i, 0)),
            pl.BlockSpec((tm, GATES), lambda i: (i, 0)),
        ],
        out_shape=[
            jax.ShapeDtypeStruct((t, SHIFT_PAD), BF16),
            jax.ShapeDtypeStruct((t, QKV_B), BF16),
            jax.ShapeDtypeStruct((t, GATES), BF16),
        ],
        compiler_params=pltpu.CompilerParams(
            dimension_semantics=("arbitrary",), vmem_limit_bytes=VMEM_LIMIT),
        name="proj",
    )(x2d, g, w_sp, w_qkv, w_gate, mu)


def _split2(x):
    hi = x.astype(BF16)
    lo = (x - hi.astype(F32)).astype(BF16)
    return hi, lo


SEG_LANES = 2 * LANES


def _segsum(x, seg):
    xb = x.astype(BF16)
    return jnp.concatenate(
        [_dot(xb[:, o:o + SEG_LANES], seg) for o in range(0, x.shape[1], SEG_LANES)], axis=1)


N_STREAMS = 4


def _rwkv_kernel(sp_ref, w0_ref, wda_ref, a0_ref, wg_ref, kk_ref, ka_ref, rk_ref,
                 lnw_ref, lnb_ref, seg_ref, tri_ref, o_ref, h_ref, *, tt):
    @pl.when(pl.program_id(1) == 0)
    def _():
        h_ref[...] = jnp.zeros_like(h_ref)

    params = (w0_ref, wda_ref, a0_ref, wg_ref, kk_ref, ka_ref, rk_ref, lnw_ref, lnb_ref,
              seg_ref, tri_ref)
    live = [_rwkv_stream(k, sp_ref, o_ref, h_ref, *params, tt=tt) for k in range(N_STREAMS)]
    done = object()
    while live:
        live = [gen for gen in live if next(gen, done) is not done]


def _rwkv_stream(k_seq, sp_ref, o_ref, h_ref, w0_ref, wda_ref, a0_ref, wg_ref, kk_ref, ka_ref,
                 rk_ref, lnw_ref, lnb_ref, seg_ref, tri_ref, *, tt):
    n_chunks = tt // CHUNK
    xs = sp_ref[k_seq, 0].astype(F32)
    r = xs[:, 0:WIDTH]
    k = xs[:, WIDTH:2 * WIDTH]
    v = xs[:, 2 * WIDTH:3 * WIDTH]
    da = xs[:, 3 * WIDTH:3 * WIDTH + LANES]
    gd = xs[:, 3 * WIDTH + LANES:SHIFT_PAD]
    lane = lax.broadcasted_iota(jnp.int32, (1, LANES), 1)
    lo_half = lane < HEAD_DIM
    da_act = jnp.where(lo_half, jnp.tanh(da), da).astype(BF16)
    lora = _dot(da_act, wda_ref[...])
    g = _dot(_sigmoid(gd).astype(BF16), wg_ref[...])

    logdec = -EXP_NEG_HALF * _sigmoid(w0_ref[...] + lora[:, 0:WIDTH])
    a = _sigmoid(a0_ref[...] + lora[:, WIDTH:2 * WIDTH])

    seg = seg_ref[...]
    kkf = k * kk_ref[...]
    kk_sq = _segsum(kkf * kkf, seg)
    yield
    kk = kkf * lax.rsqrt(jnp.maximum(kk_sq, 1e-24))
    k2 = k * (1.0 + (a - 1.0) * ka_ref[...])
    bonus = _segsum(r * k2 * rk_ref[...], seg) * v

    l1, l2 = _split2(logdec)
    tri = tri_ref[...]
    cum = _dot(tri, l1) + _dot(tri, l2)
    yield

    kka = kk * a
    p_inv = jnp.exp(-cum)
    a_t = -kk * jnp.exp(cum - logdec)
    r_t = r * jnp.exp(cum)
    b_t = kka * p_inv
    k_t = k2 * p_inv

    r64 = lax.broadcasted_iota(jnp.int32, (CHUNK, LANES), 0)
    c64 = lax.broadcasted_iota(jnp.int32, (CHUNK, LANES), 1) & (CHUNK - 1)
    strict = c64 < r64
    incl = c64 <= r64
    eye_side = (c64 == r64).astype(F32)
    r128 = lax.broadcasted_iota(jnp.int32, (LANES, LANES), 0)
    c128 = lax.broadcasted_iota(jnp.int32, (LANES, LANES), 1)
    eye128 = (r128 == c128).astype(F32)
    same_head = (r128 >> 6) == (c128 >> 6)
    hi_half = jnp.logical_not(lo_half)
    zeros_c = jnp.zeros((CHUNK, LANES), F32)

    def _stack2(zz):
        zb = zz.astype(BF16)
        return jnp.concatenate([jnp.where(lo_half, zb, 0), jnp.where(hi_half, zb, 0)], axis=0)

    chains = [(c, pr) for c in range(n_chunks) for pr in range(N_PAIRS)]

    def _piece(arr, c, pr):
        return arr[c * CHUNK:(c + 1) * CHUNK, pr * LANES:(pr + 1) * LANES]

    x_ab, x_ak, x_rb, x_rk = {}, {}, {}, {}
    for ch in chains:
        lq = jnp.concatenate([_piece(a_t, *ch), _piece(r_t, *ch)], axis=0).astype(BF16)
        rhs = jnp.concatenate([_stack2(_piece(b_t, *ch)), _stack2(_piece(k_t, *ch))], axis=0)
        sc = _dot_nt(lq, rhs)
        x_ab[ch] = jnp.where(strict, sc[0:CHUNK, 0:LANES], 0.0)
        x_ak[ch] = jnp.where(strict, sc[0:CHUNK, LANES:2 * LANES], 0.0)
        x_rb[ch] = jnp.where(incl, sc[CHUNK:2 * CHUNK, 0:LANES], 0.0)
        x_rk[ch] = jnp.where(incl, sc[CHUNK:2 * CHUNK, LANES:2 * LANES], 0.0)
    yield

    x_t = {ch: eye_side + x_ab[ch] for ch in chains}
    x_pw = {ch: _dot(x_ab[ch].astype(BF16), _stack2(x_ab[ch])) for ch in chains}
    av = {ch: _dot(x_ak[ch].astype(BF16), _stack2(_piece(v, *ch))) for ch in chains}
    yield
    for _ in range(4):
        for ch in chains:
            both = jnp.concatenate([x_t[ch], x_pw[ch]], axis=0).astype(BF16)
            out = _dot(both, _stack2(x_pw[ch]))
            x_t[ch] = x_t[ch] + out[0:CHUNK]
            x_pw[ch] = out[CHUNK:2 * CHUNK]
        yield
    for ch in chains:
        x_t[ch] = x_t[ch] + _dot(x_t[ch].astype(BF16), _stack2(x_pw[ch]))
    yield

    tw = {}
    for ch in chains:
        rhs = jnp.concatenate([_stack2(_piece(a_t, *ch)), _stack2(av[ch])], axis=1)
        tw[ch] = _dot(x_t[ch].astype(BF16), rhs)
    yield
    qeff, yloc, phi, psi = {}, {}, {}, {}
    for ch in chains:
        ta, w_loc, v_p = tw[ch][:, 0:LANES], tw[ch][:, LANES:2 * LANES], _piece(v, *ch)
        rhs = jnp.concatenate(
            [jnp.concatenate([_stack2(ta), _stack2(w_loc)], axis=1),
             jnp.concatenate([jnp.zeros((LANES, LANES), BF16), _stack2(v_p)], axis=1)], axis=0)
        ag = _dot(jnp.concatenate([x_rb[ch], x_rk[ch]], axis=1).astype(BF16), rhs)
        qeff[ch] = _piece(r_t, *ch) + ag[:, 0:LANES]
        yloc[ch] = ag[:, LANES:2 * LANES]
        cum_last = _piece(cum, *ch)[CHUNK - 1:CHUNK, :]
        p_end = jnp.exp(cum_last)
        kk_t = jnp.concatenate([_piece(b_t, *ch) * p_end, _piece(k_t, *ch) * p_end],
                               axis=0).astype(BF16).T
        rhs = jnp.concatenate(
            [tw[ch], jnp.concatenate([zeros_c, v_p], axis=1)], axis=0).astype(BF16)
        pp = _dot(kk_t, rhs)
        phi[ch] = eye128 * p_end + jnp.where(same_head, pp[:, 0:LANES], 0.0)
        psi[ch] = jnp.where(same_head, pp[:, LANES:2 * LANES], 0.0)
    yield

    y_rows = []
    states = [h_ref[k_seq, pr] for pr in range(N_PAIRS)]
    for c in range(n_chunks):
        y_c = []
        for pr in range(N_PAIRS):
            ch = (c, pr)
            lhs = jnp.concatenate([qeff[ch], phi[ch]], axis=0).astype(BF16)
            out = _dot(lhs, states[pr].astype(BF16))
            y_c.append(out[0:CHUNK] + yloc[ch])
            states[pr] = out[CHUNK:CHUNK + LANES] + psi[ch]
        y_rows.append(jnp.concatenate(y_c, axis=1))
        yield
    for pr in range(N_PAIRS):
        h_ref[k_seq, pr] = states[pr]
    y = jnp.concatenate(y_rows, axis=0)

    mean = _segsum(y, seg) * (1.0 / HEAD_DIM)
    yield
    d = y - mean
    var = _segsum(d * d, seg) * (1.0 / HEAD_DIM)
    yield
    yn = d * lax.rsqrt(var + GN_EPS) * lnw_ref[...] + lnb_ref[...]
    o_ref[k_seq, 0] = ((yn + bonus) * g).astype(BF16)


def _rwkv(sp, w0, wda, a0, wg, k_k, k_a, r_k, ln_w, ln_b, seg, tri, tt):
    b, s, _ = sp.shape
    assert b % N_STREAMS == 0
    rows = b // N_STREAMS
    const2 = lambda i, j: (0, 0)
    row = lambda n: pl.BlockSpec((1, n), const2)
    out = pl.pallas_call(
        functools.partial(_rwkv_kernel, tt=tt),
        grid=(rows, s // tt),
        in_specs=[
            pl.BlockSpec((N_STREAMS, 1, tt, SHIFT_PAD), lambda i, j: (0, i, j, 0)),
            row(WIDTH),
            pl.BlockSpec((LANES, 2 * WIDTH), const2),
            row(WIDTH),
            pl.BlockSpec((2 * LANES, WIDTH), const2),
            row(WIDTH), row(WIDTH), row(WIDTH), row(WIDTH), row(WIDTH),
            pl.BlockSpec((SEG_LANES, SEG_LANES), const2),
            pl.BlockSpec((tt, tt), const2),
        ],
        out_specs=pl.BlockSpec((N_STREAMS, 1, tt, WIDTH), lambda i, j: (0, i, j, 0)),
        out_shape=jax.ShapeDtypeStruct((N_STREAMS, rows, s, WIDTH), BF16),
        scratch_shapes=[
            pltpu.VMEM((N_STREAMS, N_PAIRS, LANES, LANES), F32),
        ],
        compiler_params=pltpu.CompilerParams(
            dimension_semantics=("arbitrary", "arbitrary"), vmem_limit_bytes=VMEM_LIMIT),
        name="rwkv",
    )(sp.reshape(N_STREAMS, rows, s, SHIFT_PAD), w0, wda, a0, wg, k_k, k_a, r_k, ln_w, ln_b,
      seg, tri)
    return out.reshape(b, s, WIDTH)


NEG_BIG = -(2.0 ** 30)
N_BLOCKS_MAX = 16
V_ROWS = HEAD_DIM + 16
LOG2E = 1.4426950408889634
Q_GROUP = 2


def _aligned(i, m):
    return i * m if isinstance(i, int) else pl.multiple_of(i * m, m)


def _moba_kernel(q_ref, k_ref, v_ref, slope_ref, o_ref, kx_ref, vt_ref, qt_ref, bias_ref, *, seq):
    nb = seq // MOBA_BLOCK
    bs = MOBA_BLOCK
    scale = HEAD_DIM ** -0.5
    heads = range(N_HEADS)

    rr = lax.broadcasted_iota(jnp.int32, (bs, LANES), 0)
    ll = lax.broadcasted_iota(jnp.int32, (bs, LANES), 1)
    ones_rows = jnp.ones((V_ROWS - HEAD_DIM, bs), BF16)
    for n in range(nb):
        aug = jnp.where(ll < 4, 1.0,
              jnp.where(ll < 6, rr.astype(F32),
              jnp.where(ll < 8, float(n * bs),
              jnp.where(ll == 8 + n, 1.0, 0.0)))).astype(BF16)
        rows = slice(n * bs, (n + 1) * bs)
        for pr in range(N_PAIRS):
            ln = slice(pr * LANES, (pr + 1) * LANES)
            kx_ref[pr, n, :, 0:LANES] = k_ref[0, rows, ln]
            kx_ref[pr, n, :, LANES:2 * LANES] = aug
            v_t = v_ref[0, rows, ln].T
            for half in range(2):
                vt_ref[2 * pr + half, n] = jnp.concatenate(
                    [v_t[half * HEAD_DIM:(half + 1) * HEAD_DIM], ones_rows], axis=0)

    bi = lax.broadcasted_iota(jnp.int32, (N_BLOCKS_MAX, seq), 0)
    si = lax.broadcasted_iota(jnp.int32, (N_BLOCKS_MAX, seq), 1)
    avg = jnp.where((si >= bi * bs) & (si < (bi + 1) * bs), 1.0 / bs, 0.0).astype(BF16)
    kmean = _dot(avg, k_ref[0])
    km = jnp.concatenate([jnp.broadcast_to(kmean[n:n + 1, :], (N_HEADS, WIDTH))
                          for n in range(N_BLOCKS_MAX)], axis=0)
    row_head = lax.broadcasted_iota(jnp.int32, (LANES, WIDTH), 0) & (N_HEADS - 1)
    lane_head = lax.broadcasted_iota(jnp.int32, (LANES, WIDTH), 1) >> 6
    km_hi, km_lo = _split2(jnp.where(row_head == lane_head, km, 0.0))

    pr_i = lax.broadcasted_iota(jnp.int32, (LANES, LANES), 0)
    pc_i = lax.broadcasted_iota(jnp.int32, (LANES, LANES), 1)
    perm = (pc_i == ((pr_i & (N_BLOCKS_MAX - 1)) << 3) + (pr_i >> 4)).astype(BF16)
    zeros_cnt = jnp.zeros((N_HEADS, bs), F32)

    gates = {}
    for i in range(nb):
        q_blk = q_ref[0, i * bs:(i + 1) * bs, :]
        q_t = [q_blk[:, pr * LANES:(pr + 1) * LANES].T for pr in range(N_PAIRS)]
        for pr in range(N_PAIRS):
            qt_ref[pr, i] = q_t[pr]
        if i <= MOBA_TOP_K:
            bias_ref[i] = jnp.zeros((N_BLOCKS_MAX * N_HEADS, bs), BF16)
            continue
        q_tb = jnp.concatenate(q_t, axis=0)
        gates[i] = _dot(km_hi, q_tb) + _dot(km_lo, q_tb)
    biases = {}
    for i, gate in gates.items():
        slabs = [gate[n * N_HEADS:(n + 1) * N_HEADS] for n in range(i)]
        cnt = [zeros_cnt] * i
        for n in range(i):
            for m in range(n):
                m_wins = slabs[m] >= slabs[n]
                cnt[n] = cnt[n] + jnp.where(m_wins, 1.0, 0.0)
                cnt[m] = cnt[m] + jnp.where(m_wins, 0.0, 1.0)
        bias = [jnp.where(cnt[n] < MOBA_TOP_K, 0.0, NEG_BIG) for n in range(i)]
        bias += [zeros_cnt] * (N_BLOCKS_MAX - i)
        biases[i] = jnp.concatenate(bias, axis=0).astype(BF16)
    for i, bias in biases.items():
        bias_ref[i] = _dot(perm, bias).astype(BF16)

    r128 = lax.broadcasted_iota(jnp.int32, (LANES, 1), 0)
    row_masks = (r128 < HEAD_DIM, r128 >= HEAD_DIM)
    r8 = lax.broadcasted_iota(jnp.int32, (8, bs), 0)
    tl = lax.broadcasted_iota(jnp.int32, (8, bs), 1).astype(F32)
    key_row = lax.broadcasted_iota(jnp.int32, (bs, bs), 0)
    qry_col = lax.broadcasted_iota(jnp.int32, (bs, bs), 1)
    causal = key_row <= qry_col
    zeros_pad = jnp.zeros((LANES - 8 - N_BLOCKS_MAX, bs), F32)

    def q_block(i, odd):
        start = _aligned(i, bs)
        q_t = [qt_ref[pr, i].astype(F32) for pr in range(N_PAIRS)]
        bias = bias_ref[i].astype(F32)
        t0 = jnp.asarray(i * bs, F32)

        q_aug = []
        for h in heads:
            c = slope_ref[h:h + 1, :] * LOG2E
            c_hi = c.astype(BF16).astype(F32)
            ctl = c * tl
            ctl_hi = ctl.astype(BF16).astype(F32)
            ct0 = c * t0
            ct0_hi = ct0.astype(BF16).astype(F32)
            aug8 = jnp.where(r8 == 0, -ctl_hi,
                   jnp.where(r8 == 1, ctl_hi - ctl,
                   jnp.where(r8 == 2, -ct0_hi,
                   jnp.where(r8 == 3, ct0_hi - ct0,
                   jnp.where((r8 == 4) | (r8 == 6), c_hi, c - c_hi)))))
            top = jnp.where(row_masks[h % 2], q_t[h // 2], 0.0) * (scale * LOG2E)
            sel = bias[h * N_BLOCKS_MAX:(h + 1) * N_BLOCKS_MAX]
            q_aug.append(jnp.concatenate([top, aug8, sel, zeros_pad], axis=0).astype(BF16))

        def scores(j):
            kx = [kx_ref[pr, j] for pr in range(N_PAIRS)]
            return [_dot(kx[h // 2], q_aug[h]) for h in heads]

        def weighted_values(j, ps):
            return [_dot(vt_ref[h, j], ps[h]) for h in heads]

        s_own = scores(i)
        s_left = scores(i - 1) if odd else None
        yield
        m0, p0 = [], []
        for h in heads:
            sh = jnp.where(causal, s_own[h], NEG_BIG)
            m0.append(jnp.max(sh, axis=0, keepdims=True))
            p0.append(jnp.exp2(sh - m0[h]).astype(BF16))
        pv = weighted_values(i, p0)
        if odd:
            alpha1, p1 = [], []
            for h in heads:
                mh = jnp.maximum(m0[h], jnp.max(s_left[h], axis=0, keepdims=True))
                alpha1.append(jnp.exp2(m0[h] - mh))
                p1.append(jnp.exp2(s_left[h] - mh).astype(BF16))
                m0[h] = mh
            pv1 = weighted_values(i - 1, p1)
        yield
        acc0 = [pv[h][0:HEAD_DIM] for h in heads]
        l0 = [pv[h][HEAD_DIM:HEAD_DIM + 1] for h in heads]
        if odd:
            acc0 = [alpha1[h] * acc0[h] + pv1[h][0:HEAD_DIM] for h in heads]
            l0 = [alpha1[h] * l0[h] + pv1[h][HEAD_DIM:HEAD_DIM + 1] for h in heads]

        def step(js, st):
            m, l, acc = (list(x) for x in st)
            s_all = [scores(j) for j in js]
            ps, alphas = [], []
            for s in s_all:
                p_blk, a_blk = [], []
                for h in heads:
                    mh = jnp.maximum(m[h], jnp.max(s[h], axis=0, keepdims=True))
                    a_blk.append(jnp.exp2(m[h] - mh))
                    p_blk.append(jnp.exp2(s[h] - mh).astype(BF16))
                    m[h] = mh
                ps.append(p_blk)
                alphas.append(a_blk)
            pvs = [weighted_values(j, p_blk) for j, p_blk in zip(js, ps)]
            for a_blk, pv in zip(alphas, pvs):
                for h in heads:
                    acc[h] = a_blk[h] * acc[h] + pv[h][0:HEAD_DIM]
                    l[h] = a_blk[h] * l[h] + pv[h][HEAD_DIM:HEAD_DIM + 1]
            return tuple(tuple(x) for x in (m, l, acc))

        st = (tuple(m0), tuple(l0), tuple(acc0))
        _, l, acc = lax.fori_loop(0, i >> 1, lambda t, st: step([2 * t, 2 * t + 1], st), st)
        pairs = []
        for pr in range(N_PAIRS):
            a, b = 2 * pr, 2 * pr + 1
            out_t = jnp.concatenate([acc[a] / l[a], acc[b] / l[b]], axis=0)
            pairs.append(out_t.astype(BF16).T)
        o_ref[0, pl.ds(start, bs), :] = jnp.concatenate(pairs, axis=1)

    def q_block_group(u, carry):
        live = [q_block(Q_GROUP * u + g, g % 2 == 1) for g in range(Q_GROUP)]
        done = object()
        while live:
            live = [gen for gen in live if next(gen, done) is not done]
        return carry

    assert nb % Q_GROUP == 0 and Q_GROUP % 2 == 0
    lax.fori_loop(0, nb // Q_GROUP, q_block_group, 0)


def _moba(qkv, slopes):
    b, s, _ = qkv.shape
    assert s % MOBA_BLOCK == 0 and s // MOBA_BLOCK <= N_BLOCKS_MAX
    nb = s // MOBA_BLOCK
    col = lambda c: pl.BlockSpec((1, s, WIDTH), lambda i: (i, 0, c))
    return pl.pallas_call(
        functools.partial(_moba_kernel, seq=s),
        grid=(b,),
        in_specs=[col(0), col(1), col(2), pl.BlockSpec((N_HEADS, MOBA_BLOCK), lambda i: (0, 0))],
        out_specs=pl.BlockSpec((1, s, WIDTH), lambda i: (i, 0, 0)),
        out_shape=jax.ShapeDtypeStruct((b, s, WIDTH), BF16),
        scratch_shapes=[
            pltpu.VMEM((N_PAIRS, nb, MOBA_BLOCK, 2 * LANES), BF16),
            pltpu.VMEM((N_HEADS, nb, V_ROWS, MOBA_BLOCK), BF16),
            pltpu.VMEM((N_PAIRS, nb, LANES, MOBA_BLOCK), BF16),
            pltpu.VMEM((nb, N_BLOCKS_MAX * N_HEADS, MOBA_BLOCK), BF16),
        ],
        compiler_params=pltpu.CompilerParams(
            dimension_semantics=("arbitrary",), vmem_limit_bytes=MOBA_VMEM_LIMIT),
        name="moba",
    )(qkv, qkv, qkv, slopes)


FFN_CHUNKS = ((0, 1024), (1024, 1024), (2048, 768))


POST_STREAMS = 2


def _post_rows(rows, x_ref, ya_ref, yb_ref, gate_ref, wua_ref, wub_ref, wo_ref, gffn_ref,
               wfi_ref, wfo_ref, gfin_ref, o_ref):
    y_a = _dot(ya_ref[rows, :], wua_ref[...])
    y_b = _dot(yb_ref[rows, :], wub_ref[...])
    yield
    ga = _sigmoid(gate_ref[rows, 0:D_MODEL].astype(F32))
    gb = _sigmoid(gate_ref[rows, D_MODEL:2 * D_MODEL].astype(F32))
    mixed = (ga * y_a + gb * y_b).astype(BF16)
    x1 = x_ref[rows, :] + _dot(mixed, wo_ref[...])
    yield
    h = _rms(x1, gffn_ref[...]).astype(BF16)
    acc = x1
    for off, n in FFN_CHUNKS:
        gg = _dot(h, wfi_ref[:, off:off + n])
        uu = _dot(h, wfi_ref[:, FFN_HIDDEN + off:FFN_HIDDEN + off + n])
        yield
        act = (gg * _sigmoid(gg) * uu).astype(BF16)
        acc = acc + _dot(act, wfo_ref[off:off + n, :])
        yield
    o_ref[rows, :] = _rms(acc, gfin_ref[...])


def _post_kernel(*refs):
    tm = refs[0].shape[0]
    part = tm // POST_STREAMS
    live = [_post_rows(slice(k * part, (k + 1) * part), *refs) for k in range(POST_STREAMS)]
    done = object()
    while live:
        live = [gen for gen in live if next(gen, done) is not done]


def _post(x2d, ya, yb, gates, wua, wub, wo, gffn, wfi, wfo, gfin, tm):
    t = x2d.shape[0]
    const = lambda i: (0, 0)
    tile = lambda n: pl.BlockSpec((tm, n), lambda i: (i, 0))
    weight = lambda a: pl.BlockSpec(a.shape, const, pipeline_mode=pl.Buffered(1))
    return pl.pallas_call(
        _post_kernel,
        grid=(t // tm,),
        in_specs=[tile(D_MODEL), tile(WIDTH), tile(WIDTH), tile(GATES),
                  weight(wua), weight(wub), weight(wo), weight(gffn),
                  weight(wfi), weight(wfo), weight(gfin)],
        out_specs=tile(D_MODEL),
        out_shape=jax.ShapeDtypeStruct((t, D_MODEL), F32),
        compiler_params=pltpu.CompilerParams(
            dimension_semantics=("arbitrary",), vmem_limit_bytes=VMEM_LIMIT),
        name="post",
    )(x2d, ya, yb, gates, wua, wub, wo, gffn, wfi, wfo, gfin)


def _layer(x, norm_mix_g, w_in, mu_shift, w0, w_decay_up, a0, w_iclr_up, w_gate_up,
           k_k, k_a, r_k, ln_x_w, ln_x_b, w_up_a, w_up_b, w_o, norm_ffn_g,
           w_ffn_in, w_ffn_out, out_g):
    b, s, _ = x.shape
    t = b * s
    x2d = x.reshape(t, D_MODEL)

    pad = SHIFT_PAD - SHIFT_WIDTH
    w_sp = jnp.pad(w_in[:, :SHIFT_WIDTH].astype(BF16), ((0, 0), (0, pad)))
    w_qkv = w_in[:, SHIFT_WIDTH:SHIFT_WIDTH + QKV_B].astype(BF16)
    w_gate = w_in[:, SHIFT_WIDTH + QKV_B:].astype(BF16)
    mu = jnp.pad(mu_shift, (0, pad)).reshape(1, SHIFT_PAD)
    wda = jnp.zeros((LANES, 2 * WIDTH), F32)
    wda = wda.at[:DECAY_LORA, :WIDTH].set(w_decay_up).at[DECAY_LORA:, WIDTH:].set(w_iclr_up)
    wg = jnp.pad(w_gate_up, ((0, 2 * LANES - GATE_LORA), (0, 0)))
    row = lambda a: a.reshape(1, -1)

    tt = TIME_TILE
    hid = jnp.arange(SEG_LANES) // HEAD_DIM
    seg = (hid[:, None] == hid[None, :]).astype(BF16)
    ti = jnp.arange(tt)
    tri = ((ti[:, None] >= ti[None, :]) & (ti[:, None] // CHUNK == ti[None, :] // CHUNK)).astype(BF16)

    sp, qkv, gates = _proj(x2d, row(norm_mix_g), w_sp, w_qkv, w_gate, mu, tm=PROJ_TILE,
                            seq=s)
    ya = _rwkv(sp.reshape(b, s, SHIFT_PAD), row(w0), wda.astype(BF16), row(a0),
               wg.astype(BF16), row(k_k), row(k_a), row(r_k), row(ln_x_w), row(ln_x_b),
               seg, tri, tt)
    slopes = jnp.exp2(-8.0 * jnp.arange(1, N_HEADS + 1, dtype=F32) / N_HEADS)
    slopes = jnp.broadcast_to(slopes[:, None], (N_HEADS, MOBA_BLOCK))
    yb = _moba(qkv.reshape(b, s, QKV_B), slopes)
    out = _post(x2d, ya.reshape(t, WIDTH), yb.reshape(t, WIDTH), gates,
                w_up_a.astype(BF16), w_up_b.astype(BF16), w_o.astype(BF16), row(norm_ffn_g),
                w_ffn_in.astype(BF16), w_ffn_out.astype(BF16), row(out_g), tm=TOKEN_TILE)
    return out.reshape(b, s, D_MODEL)


def kernel(x, norm_mix_g, w_in, mu_shift, w0, w_decay_up, a0, w_iclr_up, w_gate_up, k_k, k_a, r_k, ln_x_w, ln_x_b, w_up_a, w_up_b, w_o, norm_ffn_g, w_ffn_in, w_ffn_out, norm_final_g):
    depth = w_in.shape[0]
    assert depth == 1, "the fused post kernel applies the final norm after the only layer"
    return _layer(x, norm_mix_g[0], w_in[0], mu_shift[0], w0[0], w_decay_up[0], a0[0],
                  w_iclr_up[0], w_gate_up[0], k_k[0], k_a[0], r_k[0].reshape(-1), ln_x_w[0],
                  ln_x_b[0], w_up_a[0], w_up_b[0], w_o[0], norm_ffn_g[0], w_ffn_in[0],
                  w_ffn_out[0], norm_final_g)
```
